```python
import math
import jax, jax.numpy as jnp
from jax import lax
import numpy as np

D_MODEL = 4096
BATCH = 4
SEQ = 2048
DEPTH = 1

HEAD_DIM = 128
NSA_HEADS = 16
NSA_KV_HEADS = 4
NSA_REP = NSA_HEADS // NSA_KV_HEADS
NSA_WIDTH = NSA_HEADS * HEAD_DIM
NSA_KV_WIDTH = NSA_KV_HEADS * HEAD_DIM
CMP_BLOCK = 32
CMP_STRIDE = 16
CMP_HIDDEN = 256
SLC_BLOCK = 64
SLC_TOPK = 16
WINDOW = 512
SLC_Q_CHUNK = 32
DIFF_HEADS = 8
DIFF_DIM = 128
DIFF_VDIM = 2 * DIFF_DIM
DIFF_WIDTH = DIFF_HEADS * DIFF_VDIM
MIX_WIDTH = NSA_WIDTH + DIFF_WIDTH
COL_SIZES = (NSA_WIDTH,
             6 * NSA_KV_WIDTH,
             3 * NSA_HEADS,
             2 * DIFF_HEADS * DIFF_DIM,
             2 * DIFF_HEADS * DIFF_DIM,
             DIFF_WIDTH)
IN_COLS = sum(COL_SIZES)
SPLIT_POINTS = tuple(int(c) for c in np.cumsum(COL_SIZES)[:-1])
D_FF = 4 * D_MODEL
Q_BLOCK = 128
LN_EPS = 1e-5
RMS_EPS = 1e-6
NEG = -1e30
DEEPNORM_ALPHA = (2.0 * DEPTH) ** 0.25
DEEPNORM_BETA = (8.0 * DEPTH) ** -0.25

kernel_name = "nsa_diffattn_parallel_heads_deepnorm"


def alibi_slopes(n_heads):
    return np.array([2.0 ** (-8.0 * (h + 1) / n_heads) for h in range(n_heads)], np.float32)


def layer_norm(x, g, b):
    xf = x.astype(jnp.float32)
    mu = jnp.mean(xf, -1, keepdims=True)
    var = jnp.mean(jnp.square(xf - mu), -1, keepdims=True)
    return ((xf - mu) * lax.rsqrt(var + LN_EPS) * g + b).astype(x.dtype)


def rms_norm(x, g):
    xf = x.astype(jnp.float32)
    return (xf * lax.rsqrt(jnp.mean(xf * xf, -1, keepdims=True) + RMS_EPS) * g).astype(x.dtype)


def masked_softmax(s, mask):
    p = jax.nn.softmax(jnp.where(mask, s, NEG), axis=-1)
    return jnp.where(mask, p, 0.0)


def nsa_compressed_branch(q, k, v, pe_k, w1_k, w2_k, pe_v, w1_v, w2_v):
    S = k.shape[2]
    n_cmp = (S - CMP_BLOCK) // CMP_STRIDE + 1
    idx = np.arange(n_cmp)[:, None] * CMP_STRIDE + np.arange(CMP_BLOCK)[None, :]

    def compress(t, pe, w1, w2):
        blk = t[:, :, idx] + pe
        blk = blk.reshape(blk.shape[:3] + (CMP_BLOCK * HEAD_DIM,))
        return jax.nn.gelu(blk @ w1) @ w2

    kc = compress(k, pe_k, w1_k, w2_k)
    vc = compress(v, pe_v, w1_v, w2_v)
    s = jnp.einsum('bgrtd,bgnd->bgrtn', q, kc).astype(jnp.float32) * (HEAD_DIM ** -0.5)
    mask = idx[None, :, -1] <= np.arange(S)[:, None]
    p = masked_softmax(s, mask)
    o = jnp.einsum('bgrtn,bgnd->bgrtd', p.astype(vc.dtype), vc)
    return o, p


def nsa_select_blocks(p_cmp):
    S = p_cmp.shape[3]
    n_cmp = p_cmp.shape[4]
    n_slc = S // SLC_BLOCK
    cs = np.arange(n_cmp) * CMP_STRIDE
    ss = np.arange(n_slc) * SLC_BLOCK
    overlap = np.clip(np.minimum(cs[:, None] + CMP_BLOCK, ss[None, :] + SLC_BLOCK)
                      - np.maximum(cs[:, None], ss[None, :]), 0, None)
    overlap = (overlap / CMP_BLOCK).astype(np.float32)
    imp = jnp.einsum('bgtn,nj->bgtj', jnp.sum(p_cmp, axis=2), overlap)
    t = np.arange(S)[:, None]
    j = np.arange(n_slc)[None, :]
    causal = j * SLC_BLOCK <= t
    tb = t // SLC_BLOCK
    forced = (j == 0) | (j == tb) | (j == tb - 1)
    score = jnp.where(causal, jnp.where(forced, 1e6, imp), NEG)
    vals, sel = lax.top_k(score, min(SLC_TOPK, n_slc))
    return sel, vals > 0.5 * NEG


def nsa_selected_branch(q, k, v, sel, valid, slopes):
    B, G, R, S, Dh = q.shape
    n_slc = S // SLC_BLOCK
    n_top = sel.shape[-1]
    nc = S // SLC_Q_CHUNK
    kb = k.reshape(B, G, n_slc, SLC_BLOCK, Dh)
    vb = v.reshape(B, G, n_slc, SLC_BLOCK, Dh)
    q_c = q.reshape(B, G, R, nc, SLC_Q_CHUNK, Dh).transpose(3, 0, 1, 2, 4, 5)
    sel_c = sel.reshape(B, G, nc, SLC_Q_CHUNK, n_top).transpose(2, 0, 1, 3, 4)
    val_c = valid.reshape(B, G, nc, SLC_Q_CHUNK, n_top).transpose(2, 0, 1, 3, 4)
    t_c = jnp.arange(S, dtype=jnp.int32).reshape(nc, SLC_Q_CHUNK)
    bi = jnp.arange(B)[:, None, None, None]
    gi = jnp.arange(G)[None, :, None, None]

    def one_chunk(args):
        qc, selc, valc, tc = args
        kg = kb[bi, gi, selc]
        vg = vb[bi, gi, selc]
        s = jnp.einsum('bgrqd,bgqkld->bgrqkl', qc, kg).astype(jnp.float32) * (Dh ** -0.5)
        pos = selc[..., None] * SLC_BLOCK + jnp.arange(SLC_BLOCK)
        dist = tc[None, None, :, None, None] - pos
        mask = valc[..., None] & (dist >= 0)
        s = s - slopes[None, :, :, None, None, None] * dist[:, :, None].astype(jnp.float32)
        n_keys = n_top * SLC_BLOCK
        p = masked_softmax(s.reshape(B, G, R, SLC_Q_CHUNK, n_keys),
                           mask.reshape(B, G, 1, SLC_Q_CHUNK, n_keys))
        return jnp.einsum('bgrqn,bgqnd->bgrqd', p.astype(vg.dtype),
                          vg.reshape(B, G, SLC_Q_CHUNK, n_keys, Dh))

    o = lax.map(one_chunk, (q_c, sel_c, val_c, t_c))
    return o.transpose(1, 2, 3, 0, 4, 5).reshape(B, G, R, S, Dh)


def nsa_window_branch(q, k, v, slopes):
    B, G, R, S, Dh = q.shape
    nb = S // Q_BLOCK
    span = WINDOW + Q_BLOCK
    kp = jnp.pad(k, ((0, 0), (0, 0), (WINDOW, 0), (0, 0)))
    vp = jnp.pad(v, ((0, 0), (0, 0), (WINDOW, 0), (0, 0)))
    q_b = q.reshape(B, G, R, nb, Q_BLOCK, Dh).transpose(3, 0, 1, 2, 4, 5)

    def one_block(args):
        qb, b = args
        start = b * Q_BLOCK
        kw = lax.dynamic_slice_in_dim(kp, start, span, axis=2)
        vw = lax.dynamic_slice_in_dim(vp, start, span, axis=2)
        t = start + jnp.arange(Q_BLOCK)
        s_pos = start - WINDOW + jnp.arange(span)
        dist = t[:, None] - s_pos[None, :]
        mask = (dist >= 0) & (dist < WINDOW) & (s_pos[None, :] >= 0)
        s = jnp.einsum('bgrqd,bgkd->bgrqk', qb, kw).astype(jnp.float32) * (Dh ** -0.5)
        s = s - slopes[:, :, None, None] * dist.astype(jnp.float32)
        p = masked_softmax(s, mask)
        return jnp.einsum('bgrqk,bgkd->bgrqd', p.astype(vw.dtype), vw)

    o = lax.map(one_block, (q_b, jnp.arange(nb)))
    return o.transpose(1, 2, 3, 0, 4, 5).reshape(B, G, R, S, Dh)


def differential_attention(q12, k12, v, lam, slopes):
    B, H, _, S, d = q12.shape
    nb = S // Q_BLOCK
    q_b = q12.reshape(B, H, 2, nb, Q_BLOCK, d).transpose(3, 0, 1, 2, 4, 5)
    s_pos = jnp.arange(S)

    def one_block(args):
        qb, b = args
        t = b * Q_BLOCK + jnp.arange(Q_BLOCK)
        dist = t[:, None] - s_pos[None, :]
        mask = dist >= 0
        s = jnp.einsum('bhmqd,bhmkd->bhmqk', qb, k12).astype(jnp.float32) * (d ** -0.5)
        s = s - slopes[:, None, None, None] * dist.astype(jnp.float32)
        p = masked_softmax(s, mask)
        a = p[:, :, 0] - lam * p[:, :, 1]
        return jnp.einsum('bhqk,bhkd->bhqd', a.astype(v.dtype), v)

    o = lax.map(one_block, (q_b, jnp.arange(nb)))
    return o.transpose(1, 2, 0, 3, 4).reshape(B, H, S, v.shape[-1])


def hybrid_layer(x, w_in, cmp_pe_k, cmp_w1_k, cmp_w2_k, cmp_pe_v, cmp_w1_v, cmp_w2_v,
                 nsa_out_g, lambda_q1, lambda_k1, lambda_q2, lambda_k2, diff_subln_g,
                 w_out, ln1_g, ln1_b, w_ff1, w_ff2, ln2_g, ln2_b, lambda_init):
    B, S, _ = x.shape
    proj = jnp.einsum('bsd,dc->bsc', x, w_in)
    q_nsa, kv_nsa, g_nsa, q_diff, k_diff, v_diff = jnp.split(proj, SPLIT_POINTS, axis=-1)

    q = q_nsa.reshape(B, S, NSA_KV_HEADS, NSA_REP, HEAD_DIM).transpose(0, 2, 3, 1, 4)
    kv = kv_nsa.reshape(B, S, 6, NSA_KV_HEADS, HEAD_DIM).transpose(2, 0, 3, 1, 4)
    gates = jax.nn.sigmoid(g_nsa.astype(jnp.float32))
    gates = gates.reshape(B, S, NSA_KV_HEADS, NSA_REP, 3).transpose(0, 2, 3, 1, 4)
    slopes_nsa = jnp.asarray(alibi_slopes(NSA_HEADS).reshape(NSA_KV_HEADS, NSA_REP))
    o_cmp, p_cmp = nsa_compressed_branch(q, kv[0], kv[1], cmp_pe_k, cmp_w1_k, cmp_w2_k,
                                         cmp_pe_v, cmp_w1_v, cmp_w2_v)
    sel, valid = nsa_select_blocks(p_cmp)
    o_slc = nsa_selected_branch(q, kv[2], kv[3], sel, valid, slopes_nsa)
    o_win = nsa_window_branch(q, kv[4], kv[5], slopes_nsa)
    o_nsa = (gates[..., 0:1] * o_cmp + gates[..., 1:2] * o_slc
             + gates[..., 2:3] * o_win).astype(x.dtype)
    o_nsa = o_nsa.transpose(0, 3, 1, 2, 4).reshape(B, S, NSA_HEADS, HEAD_DIM)
    o_nsa = rms_norm(o_nsa, nsa_out_g).reshape(B, S, NSA_WIDTH)

    q12 = q_diff.reshape(B, S, DIFF_HEADS, 2, DIFF_DIM).transpose(0, 2, 3, 1, 4)
    k12 = k_diff.reshape(B, S, DIFF_HEADS, 2, DIFF_DIM).transpose(0, 2, 3, 1, 4)
    v = v_diff.reshape(B, S, DIFF_HEADS, DIFF_VDIM).transpose(0, 2, 1, 3)
    lam = (jnp.exp(jnp.sum(lambda_q1.astype(jnp.float32) * lambda_k1.astype(jnp.float32)))
           - jnp.exp(jnp.sum(lambda_q2.astype(jnp.float32) * lambda_k2.astype(jnp.float32)))
           + lambda_init)
    slopes_diff = jnp.asarray(alibi_slopes(DIFF_HEADS))
    o_diff = differential_attention(q12, k12, v, lam, slopes_diff)
    o_diff = rms_norm(o_diff, diff_subln_g) * (1.0 - lambda_init)
    o_diff = o_diff.transpose(0, 2, 1, 3).reshape(B, S, DIFF_WIDTH)

    mix = jnp.einsum('bsc,cd->bsd', jnp.concatenate([o_nsa, o_diff], axis=-1), w_out)
    x = layer_norm(DEEPNORM_ALPHA * x + mix, ln1_g, ln1_b)

    h = jnp.square(jax.nn.relu(jnp.einsum('bsd,df->bsf', x, w_ff1)))
    h = jnp.einsum('bsf,fd->bsd', h, w_ff2)
    return layer_norm(DEEPNORM_ALPHA * x + h, ln2_g, ln2_b)


def setup_inputs(seed: int = 0) -> dict:
    key = jax.random.key(seed)
    ks = jax.random.split(key, 24)
    f32 = jnp.float32
    L = DEPTH

    def nrm(k, shape, scale):
        return jax.random.normal(k, shape, f32) * scale

    def gain(k, shape):
        return 1.0 + 0.01 * jax.random.normal(k, shape, f32)

    return {
        "x": nrm(ks[0], (BATCH, SEQ, D_MODEL), 1.0),
        "w_in": nrm(ks[1], (L, D_MODEL, IN_COLS), D_MODEL ** -0.5),
        "cmp_pe_k": nrm(ks[2], (L, CMP_BLOCK, HEAD_DIM), 0.02),
        "cmp_w1_k": nrm(ks[3], (L, CMP_BLOCK * HEAD_DIM, CMP_HIDDEN), (CMP_BLOCK * HEAD_DIM) ** -0.5),
        "cmp_w2_k": nrm(ks[4], (L, CMP_HIDDEN, HEAD_DIM), CMP_HIDDEN ** -0.5),
        "cmp_pe_v": nrm(ks[5], (L, CMP_BLOCK, HEAD_DIM), 0.02),
        "cmp_w1_v": nrm(ks[6], (L, CMP_BLOCK * HEAD_DIM, CMP_HIDDEN), (CMP_BLOCK * HEAD_DIM) ** -0.5),
        "cmp_w2_v": nrm(ks[7], (L, CMP_HIDDEN, HEAD_DIM), CMP_HIDDEN ** -0.5),
        "nsa_out_g": gain(ks[8], (L, NSA_HEADS, HEAD_DIM)),
        "lambda_q1": nrm(ks[9], (L, DIFF_DIM), 0.1),
        "lambda_k1": nrm(ks[10], (L, DIFF_DIM), 0.1),
        "lambda_q2": nrm(ks[11], (L, DIFF_DIM), 0.1),
        "lambda_k2": nrm(ks[12], (L, DIFF_DIM), 0.1),
        "diff_subln_g": gain(ks[13], (L, DIFF_VDIM)),
        "w_out": nrm(ks[14], (L, MIX_WIDTH, D_MODEL), MIX_WIDTH ** -0.5 * DEEPNORM_BETA),
        "ln1_g": gain(ks[15], (L, D_MODEL)),
        "ln1_b": nrm(ks[16], (L, D_MODEL), 0.01),
        "w_ff1": nrm(ks[17], (L, D_MODEL, D_FF), D_MODEL ** -0.5),
        "w_ff2": nrm(ks[18], (L, D_FF, D_MODEL), D_FF ** -0.5 * DEEPNORM_BETA),
        "ln2_g": gain(ks[19], (L, D_MODEL)),
        "ln2_b": nrm(ks[20], (L, D_MODEL), 0.01),
    }


def reference(x, w_in, cmp_pe_k, cmp_w1_k, cmp_w2_k, cmp_pe_v, cmp_w1_v, cmp_w2_v,
              nsa_out_g, lambda_q1, lambda_k1, lambda_q2, lambda_k2, diff_subln_g,
              w_out, ln1_g, ln1_b, w_ff1, w_ff2, ln2_g, ln2_b):
    for l in range(DEPTH):
        lambda_init = 0.8 - 0.6 * math.exp(-0.3 * l)
        x = hybrid_layer(x, w_in[l], cmp_pe_k[l], cmp_w1_k[l], cmp_w2_k[l],
                         cmp_pe_v[l], cmp_w1_v[l], cmp_w2_v[l], nsa_out_g[l],
                         lambda_q1[l], lambda_k1[l], lambda_q2[l], lambda_k2[l],
                         diff_subln_g[l], w_out[l], ln1_g[l], ln1_b[l],
                         w_ff1[l], w_ff2[l], ln2_g[l], ln2_b[l], lambda_init)
    return x
```

```python
import functools
import math

import numpy as np
import jax
import jax.numpy as jnp
from jax import lax
from jax.experimental import pallas as pl
from jax.experimental.pallas import tpu as pltpu

F32 = jnp.float32
BF16 = jnp.bfloat16

D_MODEL = 4096
HEAD_DIM = 128
NSA_HEADS = 16
NSA_GROUPS = 4
NSA_REP = 4
CMP_BLOCK = 32
CMP_STRIDE = 16
CMP_HIDDEN = 256
SLC_BLOCK = 64
SLC_TOPK = 16
WINDOW = 512
DIFF_HEADS = 8
DIFF_DIM = 128
DIFF_VDIM = 256
NSA_WIDTH = NSA_HEADS * HEAD_DIM
KV_WIDTH = 6 * NSA_GROUPS * HEAD_DIM
GATE_COLS = 3 * NSA_HEADS
GATE_START = NSA_WIDTH + KV_WIDTH
DIFF_WIDTH = DIFF_HEADS * DIFF_VDIM
MAIN_COLS = NSA_WIDTH + KV_WIDTH + 3 * DIFF_WIDTH
D_FF = 4 * D_MODEL
LN_EPS = 1e-5
RMS_EPS = 1e-6
NEG = -1e30
DEPTH = 1
DEEPNORM_ALPHA = (2.0 * DEPTH) ** 0.25
ATT_SCALE = HEAD_DIM ** -0.5

LANES = 128
VMEM_LIMIT_BYTES = 56 * 1024 * 1024

MM_BM = 1024
MM_BN = 1024
FF2_BK = 2048
LN_ROWS = 256
NSA_TQ = 128
SLC_CK = 256
WIN_SPAN = WINDOW + NSA_TQ
DIFF_TQ = 256
DIFF_CK = 256


def _params(semantics):
    return pltpu.CompilerParams(dimension_semantics=semantics,
                                vmem_limit_bytes=VMEM_LIMIT_BYTES)


def _mm_kernel(a_ref, b_ref, o_ref, *, relu2):
    acc = jnp.dot(a_ref[...], b_ref[...], preferred_element_type=F32)
    if relu2:
        acc = jnp.square(jnp.maximum(acc, 0.0))
    o_ref[...] = acc.astype(o_ref.dtype)


def _matmul(a, b, out_dtype, *, bm=MM_BM, bn=MM_BN, relu2=False, name="mm"):
    m, k = a.shape
    _, n = b.shape
    bn = min(bn, n)
    return pl.pallas_call(
        functools.partial(_mm_kernel, relu2=relu2),
        grid=(m // bm, n // bn),
        in_specs=[pl.BlockSpec((bm, k), lambda i, j: (i, 0)),
                  pl.BlockSpec((k, bn), lambda i, j: (0, j))],
        out_specs=pl.BlockSpec((bm, bn), lambda i, j: (i, j)),
        out_shape=jax.ShapeDtypeStruct((m, n), out_dtype),
        compiler_params=_params(("arbitrary", "arbitrary")),
        name=name,
    )(a, b)


def _outproj_kernel(a1_ref, a2_ref, b_ref, res_ref, o_ref):
    half = a1_ref.shape[1]
    acc = jnp.dot(a1_ref[...], b_ref[0:half, :], preferred_element_type=F32)
    acc = acc + jnp.dot(a2_ref[...], b_ref[half:2 * half, :], preferred_element_type=F32)
    o_ref[...] = DEEPNORM_ALPHA * res_ref[...] + acc


def _outproj(a1, a2, b, res, *, bm=MM_BM, bn=MM_BN // 2):
    m, k1 = a1.shape
    _, n = b.shape
    return pl.pallas_call(
        _outproj_kernel,
        grid=(m // bm, n // bn),
        in_specs=[pl.BlockSpec((bm, k1), lambda i, j: (i, 0)),
                  pl.BlockSpec((bm, k1), lambda i, j: (i, 0)),
                  pl.BlockSpec((2 * k1, bn), lambda i, j: (0, j)),
                  pl.BlockSpec((bm, bn), lambda i, j: (i, j))],
        out_specs=pl.BlockSpec((bm, bn), lambda i, j: (i, j)),
        out_shape=jax.ShapeDtypeStruct((m, n), F32),
        compiler_params=_params(("arbitrary", "arbitrary")),
        name="outproj",
    )(a1, a2, b, res)


def _ff2_kernel(a_ref, b_ref, res_ref, o_ref, acc_ref):
    kk = pl.program_id(2)

    @pl.when(kk == 0)
    def _():
        acc_ref[...] = jnp.zeros_like(acc_ref)

    acc_ref[...] += jnp.dot(a_ref[...], b_ref[...], preferred_element_type=F32)

    @pl.when(kk == pl.num_programs(2) - 1)
    def _():
        o_ref[...] = DEEPNORM_ALPHA * res_ref[...] + acc_ref[...]


def _ff2(a, b, res, *, bm=MM_BM, bn=MM_BN, bk=FF2_BK):
    m, k = a.shape
    _, n = b.shape
    return pl.pallas_call(
        _ff2_kernel,
        grid=(m // bm, n // bn, k // bk),
        in_specs=[pl.BlockSpec((bm, bk), lambda i, j, kk: (i, kk)),
                  pl.BlockSpec((bk, bn), lambda i, j, kk: (kk, j)),
                  pl.BlockSpec((bm, bn), lambda i, j, kk: (i, j))],
        out_specs=pl.BlockSpec((bm, bn), lambda i, j, kk: (i, j)),
        out_shape=jax.ShapeDtypeStruct((m, n), F32),
        scratch_shapes=[pltpu.VMEM((bm, bn), F32)],
        compiler_params=_params(("arbitrary", "arbitrary", "arbitrary")),
        name="ff2",
    )(a, b, res)


def _ln_kernel(x_ref, g_ref, b_ref, *o_refs):
    x = x_ref[...]
    mu = jnp.mean(x, axis=-1, keepdims=True)
    xc = x - mu
    var = jnp.mean(xc * xc, axis=-1, keepdims=True)
    y = xc * lax.rsqrt(var + LN_EPS) * g_ref[...] + b_ref[...]
    for o_ref in o_refs:
        o_ref[...] = y.astype(o_ref.dtype)


def _layer_norm(x, g, b, out_dtypes):
    m, d = x.shape
    row_spec = pl.BlockSpec((LN_ROWS, d), lambda i: (i, 0))
    vec_spec = pl.BlockSpec((1, d), lambda i: (0, 0))
    return pl.pallas_call(
        _ln_kernel,
        grid=(m // LN_ROWS,),
        in_specs=[row_spec, vec_spec, vec_spec],
        out_specs=[row_spec for _ in out_dtypes],
        out_shape=[jax.ShapeDtypeStruct((m, d), dt) for dt in out_dtypes],
        compiler_params=_params(("arbitrary",)),
        name="layer_norm",
    )(x, g.reshape(1, d), b.reshape(1, d))


def _gelu_tanh(x):
    c = math.sqrt(2.0 / math.pi)
    return 0.5 * x * (1.0 + jnp.tanh(c * (x + 0.044715 * (x * x * x))))


def _compress_kernel(c_ref, pe_ref, w1_ref, w2_ref, o_ref):
    half = CMP_STRIDE * HEAD_DIM
    c = c_ref[0, 0, 0]
    n_chunks = c.shape[0]
    y1 = jnp.dot(c, w1_ref[0, 0:half, :], preferred_element_type=F32)
    y2 = jnp.dot(c, w1_ref[0, half:2 * half, :], preferred_element_type=F32)
    pe8 = jnp.broadcast_to(pe_ref[0], (8, 2 * half))
    pb = jnp.dot(pe8, w1_ref[0], preferred_element_type=F32)[0:1, :]
    h = y1 + pltpu.roll(y2, n_chunks - 1, 0) + pb
    a = _gelu_tanh(h).astype(BF16)
    out = jnp.dot(a, w2_ref[0], preferred_element_type=F32)
    row = lax.broadcasted_iota(jnp.int32, out.shape, 0)
    out = jnp.where(row < n_chunks - 1, out, 0.0)
    o_ref[0, 0, 0] = out.astype(o_ref.dtype)


def _compress(chunks, pe, w1, w2):
    _, b, g, n_chunks, width = chunks.shape
    return pl.pallas_call(
        _compress_kernel,
        grid=(2, b, g),
        in_specs=[pl.BlockSpec((1, 1, 1, n_chunks, width), lambda t, i, j: (t, i, j, 0, 0)),
                  pl.BlockSpec((1, 1, 2 * width), lambda t, i, j: (t, 0, 0)),
                  pl.BlockSpec((1, 2 * width, CMP_HIDDEN), lambda t, i, j: (t, 0, 0)),
                  pl.BlockSpec((1, CMP_HIDDEN, HEAD_DIM), lambda t, i, j: (t, 0, 0))],
        out_specs=pl.BlockSpec((1, 1, 1, n_chunks, HEAD_DIM), lambda t, i, j: (t, i, j, 0, 0)),
        out_shape=jax.ShapeDtypeStruct((2, b, g, n_chunks, HEAD_DIM), BF16),
        compiler_params=_params(("arbitrary", "arbitrary", "arbitrary")),
        name="compress",
    )(chunks, pe, w1, w2)


def _nt_dot(a, b):
    return lax.dot_general(a, b, (((1,), (1,)), ((), ())), preferred_element_type=F32)


def _nsa_kernel(slopes_ref, q_ref, kc_ref, vc_ref, ks_ref, vs_ref, kw_ref, vw_ref,
                gate_ref, gain_ref, ovl_ref, expand_ref, o_ref, mask_ref):
    g = pl.program_id(1)
    i = pl.program_id(2)
    tq = NSA_TQ
    rows = NSA_REP * tq
    t0 = i * tq
    q = q_ref[0]
    qs = jnp.concatenate([q[:, r * HEAD_DIM:(r + 1) * HEAD_DIM] for r in range(NSA_REP)],
                         axis=0)
    t_col = t0 + lax.broadcasted_iota(jnp.int32, (tq, 1), 0)

    n_cmp = kc_ref.shape[2]
    s_cmp = _nt_dot(qs, kc_ref[0, 0]) * ATT_SCALE
    n_row = lax.broadcasted_iota(jnp.int32, (1, n_cmp), 1)
    cmask = (n_row * CMP_STRIDE + (CMP_BLOCK - 1)) <= t_col
    p_sum = jnp.zeros((tq, n_cmp), F32)
    p_parts = []
    for r in range(NSA_REP):
        s = jnp.where(cmask, s_cmp[r * tq:(r + 1) * tq], NEG)
        m = jnp.max(s, axis=-1, keepdims=True)
        e = jnp.where(cmask, jnp.exp(s - m), 0.0)
        l = jnp.sum(e, axis=-1, keepdims=True)
        p = e * (1.0 / jnp.maximum(l, 1e-30))
        p_sum = p_sum + p
        p_parts.append(p.astype(BF16))
    o_cmp = jnp.dot(jnp.concatenate(p_parts, axis=0), vc_ref[0, 0],
                    preferred_element_type=F32)

    n_slc = ovl_ref.shape[0]
    imp_t = lax.dot_general(ovl_ref[...], p_sum, (((1,), (1,)), ((), ())),
                            precision=lax.Precision.HIGHEST,
                            preferred_element_type=F32)
    j_t = lax.broadcasted_iota(jnp.int32, (n_slc, tq), 0)
    t_t = t0 + lax.broadcasted_iota(jnp.int32, (n_slc, tq), 1)
    causal = j_t * SLC_BLOCK <= t_t
    tb = lax.shift_right_logical(t_t, int(math.log2(SLC_BLOCK)))
    forced = jnp.where(j_t == 0, 1e6, jnp.where(j_t == tb, 1e6, jnp.where(j_t == tb - 1, 1e6, imp_t)))
    score = jnp.where(causal, forced, NEG)
    rank = jnp.zeros((n_slc, tq), F32)
    for jp in range(n_slc):
        row = score[jp:jp + 1, :]
        ge = jnp.where(row >= score, 1.0, 0.0)
        gt = jnp.where(row > score, 1.0, 0.0)
        rank = rank + jnp.where(j_t > jp, ge, gt)
    sel_t = jnp.where(causal, jnp.where(rank < SLC_TOPK, 1.0, 0.0), 0.0).astype(BF16)
    mask_ref[...] = lax.dot_general(sel_t, expand_ref[...], (((0,), (0,)), ((), ())),
                                    preferred_element_type=F32)

    slope_col = jnp.concatenate(
        [jnp.full((tq, 1), 1.0, F32) * slopes_ref[g * NSA_REP + r] for r in range(NSA_REP)],
        axis=0)

    ck = SLC_CK
    n_chunks = (t0 + tq + ck - 1) // ck

    def slc_body(c, carry):
        m_i, l_i, acc = carry
        k0 = pl.multiple_of(c * ck, ck)
        kblk = ks_ref[0, pl.ds(k0, ck), :]
        vblk = vs_ref[0, pl.ds(k0, ck), :]
        pos = k0 + lax.broadcasted_iota(jnp.int32, (1, ck), 1)
        sel = jnp.where(pos <= t_col, mask_ref[:, pl.ds(k0, ck)], 0.0)
        bias = jnp.where(sel > 0.5, 0.0, NEG)
        bias4 = jnp.concatenate([bias] * NSA_REP, axis=0)
        s = _nt_dot(qs, kblk) * ATT_SCALE + slope_col * (pos - t0).astype(F32) + bias4
        m_new = jnp.maximum(m_i, jnp.max(s, axis=-1, keepdims=True))
        alpha = jnp.exp(m_i - m_new)
        p = jnp.where(s > 0.5 * NEG, jnp.exp(s - m_new), 0.0)
        l_new = alpha * l_i + jnp.sum(p, axis=-1, keepdims=True)
        acc_new = alpha * acc + jnp.dot(p.astype(BF16), vblk, preferred_element_type=F32)
        return m_new, l_new, acc_new

    m0 = jnp.full((rows, 1), NEG, F32)
    l0 = jnp.zeros((rows, 1), F32)
    a0 = jnp.zeros((rows, HEAD_DIM), F32)
    _, l_s, acc_s = lax.fori_loop(0, n_chunks, slc_body, (m0, l0, a0))
    o_slc = acc_s * (1.0 / jnp.maximum(l_s, 1e-30))

    k0 = pl.multiple_of(jnp.maximum(t0 - WINDOW, 0), tq)
    kblk = kw_ref[0, pl.ds(k0, WIN_SPAN), :]
    vblk = vw_ref[0, pl.ds(k0, WIN_SPAN), :]
    pos = k0 + lax.broadcasted_iota(jnp.int32, (1, WIN_SPAN), 1)
    dist = t_col - pos
    bias = jnp.where(dist >= 0, jnp.where(dist < WINDOW, 0.0, NEG), NEG)
    bias4 = jnp.concatenate([bias] * NSA_REP, axis=0)
    s = _nt_dot(qs, kblk) * ATT_SCALE + slope_col * (pos - t0).astype(F32) + bias4
    m = jnp.max(s, axis=-1, keepdims=True)
    p = jnp.where(s > 0.5 * NEG, jnp.exp(s - m), 0.0)
    l = jnp.sum(p, axis=-1, keepdims=True)
    o_win = jnp.dot(p.astype(BF16), vblk, preferred_element_type=F32) * (1.0 / jnp.maximum(l, 1e-30))

    sig = jax.nn.sigmoid(gate_ref[0])
    lane = lax.broadcasted_iota(jnp.int32, sig.shape, 1)
    for r in range(NSA_REP):
        base = (g * NSA_REP + r) * 3
        gts = [jnp.sum(jnp.where(lane == base + br, sig, 0.0), axis=-1, keepdims=True)
               for br in range(3)]
        sl = slice(r * tq, (r + 1) * tq)
        o = gts[0] * o_cmp[sl] + gts[1] * o_slc[sl] + gts[2] * o_win[sl]
        o = o * lax.rsqrt(jnp.mean(o * o, axis=-1, keepdims=True) + RMS_EPS) * gain_ref[0, r:r + 1, :]
        o_ref[0, :, r * HEAD_DIM:(r + 1) * HEAD_DIM] = o.astype(o_ref.dtype)


def _nsa_attention(proj3, cmp_kv, gates3, gain, slopes, ovl_t, expand):
    b, s, _ = proj3.shape
    n_chunks = cmp_kv.shape[3]
    col0 = NSA_WIDTH // HEAD_DIM

    def kv_spec(idx):
        return pl.BlockSpec((1, s, HEAD_DIM), lambda bi, g, i, idx=idx: (bi, 0, col0 + idx * NSA_GROUPS + g))

    cmp_spec = pl.BlockSpec((1, 1, n_chunks, HEAD_DIM), lambda bi, g, i: (bi, g, 0, 0))
    in_specs = [
        pl.BlockSpec(memory_space=pltpu.SMEM),
        pl.BlockSpec((1, NSA_TQ, NSA_REP * HEAD_DIM), lambda bi, g, i: (bi, i, g)),
        cmp_spec, cmp_spec,
        kv_spec(2), kv_spec(3), kv_spec(4), kv_spec(5),
        pl.BlockSpec((1, NSA_TQ, LANES), lambda bi, g, i: (bi, i, 0)),
        pl.BlockSpec((1, NSA_REP, HEAD_DIM), lambda bi, g, i: (g, 0, 0)),
        pl.BlockSpec(ovl_t.shape, lambda bi, g, i: (0, 0)),
        pl.BlockSpec(expand.shape, lambda bi, g, i: (0, 0)),
    ]
    return pl.pallas_call(
        _nsa_kernel,
        grid=(b, NSA_GROUPS, s // NSA_TQ),
        in_specs=in_specs,
        out_specs=pl.BlockSpec((1, NSA_TQ, NSA_REP * HEAD_DIM), lambda bi, g, i: (bi, i, g)),
        out_shape=jax.ShapeDtypeStruct((b, s, NSA_WIDTH), BF16),
        scratch_shapes=[pltpu.VMEM((NSA_TQ, s), F32)],
        compiler_params=_params(("arbitrary", "arbitrary", "arbitrary")),
        name="nsa_attention",
    )(slopes, proj3, cmp_kv[0], cmp_kv[1], proj3, proj3, proj3, proj3, gates3, gain, ovl_t, expand)


def _diff_kernel(slopes_ref, lq1_ref, lk1_ref, lq2_ref, lk2_ref, q_ref, k_ref, v_ref, g_ref,
                 o_ref, *, lambda_init):
    h = pl.program_id(1)
    i = pl.program_id(2)
    tq, ck = DIFF_TQ, DIFF_CK
    t0 = i * tq
    slope = slopes_ref[h]
    q = q_ref[0]
    qm = [q[:, 0:DIFF_DIM], q[:, DIFF_DIM:2 * DIFF_DIM]]
    t_col = t0 + lax.broadcasted_iota(jnp.int32, (tq, 1), 0)
    n_chunks = (t0 + tq) // ck

    def body(c, carry):
        k0 = pl.multiple_of(c * ck, ck)
        kblk = k_ref[0, pl.ds(k0, ck), :]
        vblk = v_ref[0, pl.ds(k0, ck), :]
        pos = k0 + lax.broadcasted_iota(jnp.int32, (1, ck), 1)
        ok = pos <= t_col
        bias = slope * (pos - t0).astype(F32)
        out = []
        for mi in range(2):
            m_i, l_i, acc = carry[3 * mi:3 * mi + 3]
            s = _nt_dot(qm[mi], kblk[:, mi * DIFF_DIM:(mi + 1) * DIFF_DIM]) * ATT_SCALE + bias
            s = jnp.where(ok, s, NEG)
            m_new = jnp.maximum(m_i, jnp.max(s, axis=-1, keepdims=True))
            alpha = jnp.exp(m_i - m_new)
            p = jnp.where(ok, jnp.exp(s - m_new), 0.0)
            l_new = alpha * l_i + jnp.sum(p, axis=-1, keepdims=True)
            acc_new = alpha * acc + jnp.dot(p.astype(BF16), vblk, preferred_element_type=F32)
            out += [m_new, l_new, acc_new]
        return tuple(out)

    init = (jnp.full((tq, 1), NEG, F32), jnp.zeros((tq, 1), F32), jnp.zeros((tq, DIFF_VDIM), F32)) * 2
    _, l1, a1, _, l2, a2 = lax.fori_loop(0, n_chunks, body, init)
    lam = (jnp.exp(jnp.sum(lq1_ref[...] * lk1_ref[...], axis=-1, keepdims=True))
           - jnp.exp(jnp.sum(lq2_ref[...] * lk2_ref[...], axis=-1, keepdims=True))
           + lambda_init)
    o = a1 * (1.0 / l1) - lam * (a2 * (1.0 / l2))
    o = o * lax.rsqrt(jnp.mean(o * o, axis=-1, keepdims=True) + RMS_EPS) * g_ref[...]
    o_ref[0] = (o * (1.0 - lambda_init)).astype(o_ref.dtype)


def _diff_attention(proj3, lq1, lk1, lq2, lk2, gain, slopes, lambda_init):
    b, s, _ = proj3.shape
    qcol = (NSA_WIDTH + KV_WIDTH) // DIFF_VDIM
    kcol = qcol + DIFF_HEADS
    vcol = kcol + DIFF_HEADS
    vec = pl.BlockSpec((1, DIFF_DIM), lambda bi, h, i: (0, 0))
    in_specs = [
        pl.BlockSpec(memory_space=pltpu.SMEM),
        vec, vec, vec, vec,
        pl.BlockSpec((1, DIFF_TQ, DIFF_VDIM), lambda bi, h, i: (bi, i, qcol + h)),
        pl.BlockSpec((1, s, DIFF_VDIM), lambda bi, h, i: (bi, 0, kcol + h)),
        pl.BlockSpec((1, s, DIFF_VDIM), lambda bi, h, i: (bi, 0, vcol + h)),
        pl.BlockSpec((1, DIFF_VDIM), lambda bi, h, i: (0, 0)),
    ]
    return pl.pallas_call(
        functools.partial(_diff_kernel, lambda_init=lambda_init),
        grid=(b, DIFF_HEADS, s // DIFF_TQ),
        in_specs=in_specs,
        out_specs=pl.BlockSpec((1, DIFF_TQ, DIFF_VDIM), lambda bi, h, i: (bi, i, h)),
        out_shape=jax.ShapeDtypeStruct((b, s, DIFF_WIDTH), BF16),
        compiler_params=_params(("arbitrary", "arbitrary", "arbitrary")),
        name="diff_attention",
    )(slopes, lq1.reshape(1, -1), lk1.reshape(1, -1), lq2.reshape(1, -1), lk2.reshape(1, -1),
      proj3, proj3, proj3, gain.reshape(1, -1))


def _alibi_slopes(n_heads):
    return np.array([2.0 ** (-8.0 * (h + 1) / n_heads) for h in range(n_heads)], np.float32)


def _overlap_t(seq):
    n_cmp = seq // CMP_STRIDE
    n_slc = seq // SLC_BLOCK
    cs = np.arange(n_cmp) * CMP_STRIDE
    ss = np.arange(n_slc) * SLC_BLOCK
    ov = np.clip(np.minimum(cs[:, None] + CMP_BLOCK, ss[None, :] + SLC_BLOCK)
                 - np.maximum(cs[:, None], ss[None, :]), 0, None)
    ov[(seq - CMP_BLOCK) // CMP_STRIDE + 1:, :] = 0
    return np.ascontiguousarray((ov / CMP_BLOCK).astype(np.float32).T)


def _expand(seq):
    n_slc = seq // SLC_BLOCK
    return (np.arange(seq)[None, :] // SLC_BLOCK == np.arange(n_slc)[:, None]).astype(np.float32)


def _layer(x, w_in, cmp_pe_k, cmp_w1_k, cmp_w2_k, cmp_pe_v, cmp_w1_v, cmp_w2_v, nsa_out_g,
           lambda_q1, lambda_k1, lambda_q2, lambda_k2, diff_subln_g, w_out, ln1_g, ln1_b,
           w_ff1, w_ff2, ln2_g, ln2_b, lambda_init):
    b, s, d = x.shape
    t = b * s
    x2 = x.reshape(t, d)
    x_bf = x2.astype(BF16)

    w_main = jnp.concatenate([w_in[:, :GATE_START], w_in[:, GATE_START + GATE_COLS:]],
                             axis=1).astype(BF16)
    w_gate = jnp.pad(w_in[:, GATE_START:GATE_START + GATE_COLS],
                     ((0, 0), (0, LANES - GATE_COLS))).astype(BF16)

    proj = _matmul(x_bf, w_main, BF16, name="in_proj")
    gates = _matmul(x_bf, w_gate, F32, name="gate_proj")
    proj3 = proj.reshape(b, s, MAIN_COLS)

    n_chunks = s // CMP_STRIDE
    kv_cmp = proj3[:, :, NSA_WIDTH:NSA_WIDTH + 2 * NSA_GROUPS * HEAD_DIM]
    kv_cmp = kv_cmp.reshape(b, n_chunks, CMP_STRIDE, 2, NSA_GROUPS, HEAD_DIM)
    chunks = kv_cmp.transpose(3, 0, 4, 1, 2, 5).reshape(2, b, NSA_GROUPS, n_chunks,
                                                        CMP_STRIDE * HEAD_DIM)
    pe = jnp.stack([cmp_pe_k, cmp_pe_v]).reshape(2, 1, CMP_BLOCK * HEAD_DIM).astype(BF16)
    w1 = jnp.stack([cmp_w1_k, cmp_w1_v]).astype(BF16)
    w2 = jnp.stack([cmp_w2_k, cmp_w2_v]).astype(BF16)
    cmp_kv = _compress(chunks, pe, w1, w2)

    o_nsa = _nsa_attention(
        proj3, cmp_kv, gates.reshape(b, s, LANES),
        nsa_out_g.reshape(NSA_GROUPS, NSA_REP, HEAD_DIM),
        jnp.asarray(_alibi_slopes(NSA_HEADS)),
        jnp.asarray(_overlap_t(s)), jnp.asarray(_expand(s)).astype(BF16))
    o_diff = _diff_attention(proj3, lambda_q1, lambda_k1, lambda_q2, lambda_k2, diff_subln_g,
                             jnp.asarray(_alibi_slopes(DIFF_HEADS)), lambda_init)

    h1 = _outproj(o_nsa.reshape(t, NSA_WIDTH), o_diff.reshape(t, DIFF_WIDTH),
                  w_out.astype(BF16), x2)
    x1, x1_bf = _layer_norm(h1, ln1_g, ln1_b, (F32, BF16))
    hid = _matmul(x1_bf, w_ff1.astype(BF16), BF16, relu2=True, name="ff1")
    h2 = _ff2(hid, w_ff2.astype(BF16), x1)
    (out,) = _layer_norm(h2, ln2_g, ln2_b, (F32,))
    return out.reshape(b, s, d)


def kernel(x, w_in, cmp_pe_k, cmp_w1_k, cmp_w2_k, cmp_pe_v, cmp_w1_v, cmp_w2_v, nsa_out_g,
           lambda_q1, lambda_k1, lambda_q2, lambda_k2, diff_subln_g, w_out, ln1_g, ln1_b,
           w_ff1, w_ff2, ln2_g, ln2_b):
    for l in range(DEPTH):
        lambda_init = 0.8 - 0.6 * math.exp(-0.3 * l)
        x = _layer(x, w_in[l], cmp_pe_k[l], cmp_w1_k[l], cmp_w2_k[l], cmp_pe_v[l], cmp_w1_v[l],
                   cmp_w2_v[l], nsa_out_g[l], lambda_q1[l], lambda_k1[l], lambda_q2[l],
                   lambda_k2[l], diff_subln_g[l], w_out[l], ln1_g[l], ln1_b[l], w_ff1[l],
                   w_ff2[l], ln2_g[l], ln2_b[l], lambda_init)
    return x
```

```python
import functools
import math

import numpy as np
import jax
import jax.numpy as jnp
from jax import lax
from jax.experimental import pallas as pl
from jax.experimental.pallas import tpu as pltpu

F32 = jnp.float32
BF16 = jnp.bfloat16

D_MODEL = 4096
HEAD_DIM = 128
NSA_HEADS = 16
NSA_GROUPS = 4
NSA_REP = 4
CMP_BLOCK = 32
CMP_STRIDE = 16
CMP_HIDDEN = 256
SLC_BLOCK = 64
SLC_TOPK = 16
WINDOW = 512
DIFF_HEADS = 8
DIFF_DIM = 128
DIFF_VDIM = 256
NSA_WIDTH = NSA_HEADS * HEAD_DIM
KV_WIDTH = 6 * NSA_GROUPS * HEAD_DIM
GATE_COLS = 3 * NSA_HEADS
GATE_START = NSA_WIDTH + KV_WIDTH
DIFF_WIDTH = DIFF_HEADS * DIFF_VDIM
PROJ_A_COLS = GATE_START
PROJ_B_COLS = 3 * DIFF_WIDTH
D_FF = 4 * D_MODEL
LN_EPS = 1e-5
RMS_EPS = 1e-6
NEG = -1e30
DEPTH = 1
DEEPNORM_ALPHA = (2.0 * DEPTH) ** 0.25
ATT_SCALE = HEAD_DIM ** -0.5
LOG2E = 1.4426950408889634
ATT_SCALE2 = ATT_SCALE * LOG2E

LANES = 128
VMEM_LIMIT_BYTES = 56 * 1024 * 1024

MM_BM = 1024
MM_BN = 512
FF2_BK = 2048
CAST_ROWS = 512
LN_ROWS = 256
NSA_TQ = 256
SLC_CK = 512
WIN_SPAN = WINDOW + NSA_TQ
DIFF_TQ = 512
DIFF_CK = 512


def _params(n_axes):
    return pltpu.CompilerParams(dimension_semantics=("arbitrary",) * n_axes,
                                vmem_limit_bytes=VMEM_LIMIT_BYTES)


def _cast_weights(wb_ref, b_ref, tail_ref, shift):
    k, bn = wb_ref.shape
    for r0 in range(0, k, CAST_ROWS):
        rows = slice(r0, r0 + CAST_ROWS)
        if shift:
            full = jnp.concatenate([b_ref[rows, :], tail_ref[rows, :]], axis=1)
            wb_ref[rows, :] = full[:, shift:shift + bn].astype(BF16)
        else:
            wb_ref[rows, :] = b_ref[rows, :].astype(BF16)


def _wmm_kernel(*refs, shift, relu2):
    if shift:
        a_ref, b_ref, tail_ref, o_ref, wb_ref = refs
    else:
        a_ref, b_ref, o_ref, wb_ref = refs
        tail_ref = None

    @pl.when(pl.program_id(1) == 0)
    def _():
        _cast_weights(wb_ref, b_ref, tail_ref, shift)

    acc = jnp.dot(a_ref[...], wb_ref[...], preferred_element_type=F32)
    if relu2:
        acc = jnp.square(jnp.maximum(acc, 0.0))
    o_ref[...] = acc.astype(o_ref.dtype)


def _wmm(a, w, out_dtype, *, col0, n, shift=0, bn=MM_BN, bm=MM_BM, relu2=False, name):
    m, k = a.shape
    bn = min(bn, n)
    jb = col0 // bn
    in_specs = [pl.BlockSpec((bm, k), lambda j, i: (i, 0)),
                pl.BlockSpec((k, bn), lambda j, i: (0, jb + j))]
    args = [a, w]
    if shift:
        per = bn // LANES
        tb = col0 // LANES
        in_specs.append(pl.BlockSpec((k, LANES), lambda j, i: (0, tb + per * (j + 1))))
        args.append(w)
    return pl.pallas_call(
        functools.partial(_wmm_kernel, shift=shift, relu2=relu2),
        grid=(n // bn, m // bm),
        in_specs=in_specs,
        out_specs=pl.BlockSpec((bm, bn), lambda j, i: (i, j)),
        out_shape=jax.ShapeDtypeStruct((m, n), out_dtype),
        scratch_shapes=[pltpu.VMEM((k, bn), BF16)],
        compiler_params=_params(2),
        name=name,
    )(*args)


def _outproj_kernel(a1_ref, a2_ref, b_ref, res_ref, o_ref, wb_ref):
    @pl.when(pl.program_id(1) == 0)
    def _():
        _cast_weights(wb_ref, b_ref, None, 0)

    half = a1_ref.shape[1]
    acc = jnp.dot(a1_ref[...], wb_ref[0:half, :], preferred_element_type=F32)
    acc = acc + jnp.dot(a2_ref[...], wb_ref[half:2 * half, :], preferred_element_type=F32)
    o_ref[...] = DEEPNORM_ALPHA * res_ref[...] + acc


def _outproj(a1, a2, w, res, *, bm=MM_BM, bn=MM_BN):
    m, k1 = a1.shape
    _, n = w.shape
    return pl.pallas_call(
        _outproj_kernel,
        grid=(n // bn, m // bm),
        in_specs=[pl.BlockSpec((bm, k1), lambda j, i: (i, 0)),
                  pl.BlockSpec((bm, k1), lambda j, i: (i, 0)),
                  pl.BlockSpec((2 * k1, bn), lambda j, i: (0, j)),
                  pl.BlockSpec((bm, bn), lambda j, i: (i, j))],
        out_specs=pl.BlockSpec((bm, bn), lambda j, i: (i, j)),
        out_shape=jax.ShapeDtypeStruct((m, n), F32),
        scratch_shapes=[pltpu.VMEM((2 * k1, bn), BF16)],
        compiler_params=_params(2),
        name="outproj",
    )(a1, a2, w, res)


def _ff2_kernel(a_ref, b_ref, res_ref, o_ref, wb_ref, acc_ref):
    kk = pl.program_id(1)
    i = pl.program_id(2)
    bm = a_ref.shape[0]

    @pl.when(i == 0)
    def _():
        _cast_weights(wb_ref, b_ref, None, 0)

    rows = pl.ds(pl.multiple_of(i * bm, bm), bm)
    part = jnp.dot(a_ref[...], wb_ref[...], preferred_element_type=F32)

    @pl.when(kk == 0)
    def _():
        acc_ref[rows, :] = part

    @pl.when(kk > 0)
    def _():
        acc_ref[rows, :] += part

    @pl.when(kk == pl.num_programs(1) - 1)
    def _():
        o_ref[...] = DEEPNORM_ALPHA * res_ref[...] + acc_ref[rows, :]


def _ff2(a, w, res, *, bm=MM_BM, bn=MM_BN, bk=FF2_BK):
    m, k = a.shape
    _, n = w.shape
    nk = k // bk

    def last_only(j, kk, i):
        return (jnp.where(kk == nk - 1, i, 0), j)

    return pl.pallas_call(
        _ff2_kernel,
        grid=(n // bn, nk, m // bm),
        in_specs=[pl.BlockSpec((bm, bk), lambda j, kk, i: (i, kk)),
                  pl.BlockSpec((bk, bn), lambda j, kk, i: (kk, j)),
                  pl.BlockSpec((bm, bn), last_only)],
        out_specs=pl.BlockSpec((bm, bn), last_only),
        out_shape=jax.ShapeDtypeStruct((m, n), F32),
        scratch_shapes=[pltpu.VMEM((bk, bn), BF16), pltpu.VMEM((m, bn), F32)],
        compiler_params=_params(3),
        name="ff2",
    )(a, w, res)


def _ln_kernel(x_ref, g_ref, b_ref, *o_refs):
    x = x_ref[...]
    mu = jnp.mean(x, axis=-1, keepdims=True)
    xc = x - mu
    var = jnp.mean(xc * xc, axis=-1, keepdims=True)
    y = xc * lax.rsqrt(var + LN_EPS) * g_ref[...] + b_ref[...]
    for o_ref in o_refs:
        o_ref[...] = y.astype(o_ref.dtype)


def _layer_norm(x, g, b, out_dtypes):
    m, d = x.shape
    row_spec = pl.BlockSpec((LN_ROWS, d), lambda i: (i, 0))
    vec_spec = pl.BlockSpec((1, d), lambda i: (0, 0))
    return pl.pallas_call(
        _ln_kernel,
        grid=(m // LN_ROWS,),
        in_specs=[row_spec, vec_spec, vec_spec],
        out_specs=[row_spec for _ in out_dtypes],
        out_shape=[jax.ShapeDtypeStruct((m, d), dt) for dt in out_dtypes],
        compiler_params=_params(1),
        name="layer_norm",
    )(x, g.reshape(1, d), b.reshape(1, d))


def _gelu_tanh(x):
    c = math.sqrt(2.0 / math.pi)
    return 0.5 * x * (1.0 + jnp.tanh(c * (x + 0.044715 * (x * x * x))))


def _compress_kernel(c_ref, pe_ref, w1_ref, w2_ref, o_ref):
    half = CMP_STRIDE * HEAD_DIM
    c = c_ref[0, 0, 0]
    n_chunks = c.shape[0]
    y1 = jnp.dot(c, w1_ref[0, 0:half, :], preferred_element_type=F32)
    y2 = jnp.dot(c, w1_ref[0, half:2 * half, :], preferred_element_type=F32)
    pe8 = jnp.broadcast_to(pe_ref[0], (8, 2 * half))
    pb = jnp.dot(pe8, w1_ref[0], preferred_element_type=F32)[0:1, :]
    h = y1 + pltpu.roll(y2, n_chunks - 1, 0) + pb
    a = _gelu_tanh(h).astype(BF16)
    out = jnp.dot(a, w2_ref[0], preferred_element_type=F32)
    row = lax.broadcasted_iota(jnp.int32, out.shape, 0)
    out = jnp.where(row < n_chunks - 1, out, 0.0)
    o_ref[0, 0, 0] = out.astype(o_ref.dtype)


def _compress(chunks, pe, w1, w2):
    _, b, g, n_chunks, width = chunks.shape
    return pl.pallas_call(
        _compress_kernel,
        grid=(2, b, g),
        in_specs=[pl.BlockSpec((1, 1, 1, n_chunks, width), lambda t, i, j: (t, i, j, 0, 0)),
                  pl.BlockSpec((1, 1, 2 * width), lambda t, i, j: (t, 0, 0)),
                  pl.BlockSpec((1, 2 * width, CMP_HIDDEN), lambda t, i, j: (t, 0, 0)),
                  pl.BlockSpec((1, CMP_HIDDEN, HEAD_DIM), lambda t, i, j: (t, 0, 0))],
        out_specs=pl.BlockSpec((1, 1, 1, n_chunks, HEAD_DIM), lambda t, i, j: (t, i, j, 0, 0)),
        out_shape=jax.ShapeDtypeStruct((2, b, g, n_chunks, HEAD_DIM), BF16),
        compiler_params=_params(3),
        name="compress",
    )(chunks, pe, w1, w2)


def _nt_dot(a, b):
    return lax.dot_general(a, b, (((1,), (1,)), ((), ())), preferred_element_type=F32)


def _lane_fold(x, op):
    out = x[:, 0:LANES]
    for c0 in range(LANES, x.shape[1], LANES):
        out = op(out, x[:, c0:c0 + LANES])
    return out


def _nsa_kernel(slopes_ref, q_ref, kc_ref, vc_ref, ks_ref, vs_ref, kw_ref, vw_ref,
                gate_ref, gain_ref, ovl_ref, expand_ref, o_ref, mask_ref, s_ref):
    g = pl.program_id(1)
    i = pl.program_id(2)
    tq = NSA_TQ
    rows = NSA_REP * tq
    t0 = i * tq
    q = q_ref[0]
    qs = jnp.concatenate([q[:, r * HEAD_DIM:(r + 1) * HEAD_DIM] for r in range(NSA_REP)],
                         axis=0)
    t_col = t0 + lax.broadcasted_iota(jnp.int32, (tq, 1), 0)
    head = [slice(r * tq, (r + 1) * tq) for r in range(NSA_REP)]
    slope2 = [slopes_ref[g * NSA_REP + r] * LOG2E for r in range(NSA_REP)]

    n_cmp = kc_ref.shape[2]
    s_cmp = _nt_dot(qs, kc_ref[0, 0]) * ATT_SCALE2
    n_row = lax.broadcasted_iota(jnp.int32, (1, n_cmp), 1)
    cmask = (n_row * CMP_STRIDE + (CMP_BLOCK - 1)) <= t_col
    p_sum = jnp.zeros((tq, n_cmp), F32)
    p_parts = []
    for r in range(NSA_REP):
        s = jnp.where(cmask, s_cmp[head[r]], NEG)
        m = jnp.max(s, axis=-1, keepdims=True)
        e = jnp.where(cmask, jnp.exp2(s - m), 0.0)
        l = jnp.sum(e, axis=-1, keepdims=True)
        p = e * (1.0 / jnp.maximum(l, 1e-30))
        p_sum = p_sum + p
        p_parts.append(p.astype(BF16))
    o_cmp = jnp.dot(jnp.concatenate(p_parts, axis=0), vc_ref[0, 0],
                    preferred_element_type=F32)

    n_slc = ovl_ref.shape[0]
    imp_t = lax.dot_general(ovl_ref[...], p_sum, (((1,), (1,)), ((), ())),
                            precision=lax.Precision.HIGHEST,
                            preferred_element_type=F32)
    j_t = lax.broadcasted_iota(jnp.int32, (n_slc, tq), 0)
    t_t = t0 + lax.broadcasted_iota(jnp.int32, (n_slc, tq), 1)
    causal = j_t * SLC_BLOCK <= t_t
    tb = lax.shift_right_logical(t_t, int(math.log2(SLC_BLOCK)))
    forced = jnp.where(j_t == 0, 1e6, jnp.where(j_t == tb, 1e6, jnp.where(j_t == tb - 1, 1e6, imp_t)))
    score = jnp.where(causal, forced, NEG)
    rank = jnp.zeros((n_slc, tq), F32)
    for jp in range(n_slc):
        row = score[jp:jp + 1, :]
        ge = jnp.where(row >= score, 1.0, 0.0)
        gt = jnp.where(row > score, 1.0, 0.0)
        rank = rank + jnp.where(j_t > jp, ge, gt)
    sel_bias = jnp.where(causal, jnp.where(rank < SLC_TOPK, 0.0, NEG), NEG).astype(BF16)
    mask_ref[...] = lax.dot_general(sel_bias, expand_ref[...], (((0,), (0,)), ((), ())),
                                    preferred_element_type=F32)

    ck = SLC_CK
    n_chunks = (t0 + tq + ck - 1) // ck
    kd = pl.multiple_of((n_chunks - 1) * ck, ck)
    pos_d = kd + lax.broadcasted_iota(jnp.int32, (1, ck), 1)
    mask_ref[:, pl.ds(kd, ck)] = jnp.where(pos_d <= t_col, mask_ref[:, pl.ds(kd, ck)], NEG)

    def slc_scores(c, mx):
        k0 = pl.multiple_of(c * ck, ck)
        s = _nt_dot(qs, ks_ref[0, pl.ds(k0, ck), :])
        posrel = ((k0 - t0) + lax.broadcasted_iota(jnp.int32, (1, ck), 1)).astype(F32)
        mb = mask_ref[:, pl.ds(k0, ck)]
        out = []
        for r in range(NSA_REP):
            t = s[head[r]] * ATT_SCALE2 + slope2[r] * posrel + mb
            s_ref[head[r], pl.ds(k0, ck)] = t
            out.append(jnp.maximum(mx[r], _lane_fold(t, jnp.maximum)))
        return tuple(out)

    mx = lax.fori_loop(0, n_chunks, slc_scores,
                       tuple(jnp.full((tq, LANES), NEG, F32) for _ in range(NSA_REP)))
    m_slc = [jnp.max(mx[r], axis=-1, keepdims=True) for r in range(NSA_REP)]

    def slc_values(c, carry):
        ls, acc = carry[:NSA_REP], carry[NSA_REP]
        k0 = pl.multiple_of(c * ck, ck)
        ps, new_ls = [], []
        for r in range(NSA_REP):
            p = jnp.exp2(s_ref[head[r], pl.ds(k0, ck)] - m_slc[r])
            new_ls.append(ls[r] + _lane_fold(p, jnp.add))
            ps.append(p.astype(BF16))
        acc = acc + jnp.dot(jnp.concatenate(ps, axis=0), vs_ref[0, pl.ds(k0, ck), :],
                            preferred_element_type=F32)
        return tuple(new_ls) + (acc,)

    init = tuple(jnp.zeros((tq, LANES), F32) for _ in range(NSA_REP)) + (jnp.zeros((rows, HEAD_DIM), F32),)
    res = lax.fori_loop(0, n_chunks, slc_values, init)
    acc_s = res[NSA_REP]
    inv_l = [1.0 / jnp.sum(res[r], axis=-1, keepdims=True) for r in range(NSA_REP)]

    k0 = pl.multiple_of(jnp.maximum(t0 - WINDOW, 0), tq)
    kblk = kw_ref[0, pl.ds(k0, WIN_SPAN), :]
    vblk = vw_ref[0, pl.ds(k0, WIN_SPAN), :]
    pos = k0 + lax.broadcasted_iota(jnp.int32, (1, WIN_SPAN), 1)
    dist = t_col - pos
    wbias = jnp.where(dist >= 0, jnp.where(dist < WINDOW, 0.0, NEG), NEG)
    posrel = (pos - t0).astype(F32)
    s_win = _nt_dot(qs, kblk)
    ps, inv_lw = [], []
    for r in range(NSA_REP):
        t = s_win[head[r]] * ATT_SCALE2 + slope2[r] * posrel + wbias
        m = jnp.max(t, axis=-1, keepdims=True)
        p = jnp.exp2(t - m)
        inv_lw.append(1.0 / jnp.sum(p, axis=-1, keepdims=True))
        ps.append(p.astype(BF16))
    o_win = jnp.dot(jnp.concatenate(ps, axis=0), vblk, preferred_element_type=F32)

    sig = jax.nn.sigmoid(gate_ref[0])
    lane = lax.broadcasted_iota(jnp.int32, sig.shape, 1)
    for r in range(NSA_REP):
        base = (g * NSA_REP + r) * 3
        gts = [jnp.sum(jnp.where(lane == base + br, sig, 0.0), axis=-1, keepdims=True)
               for br in range(3)]
        o = (gts[0] * o_cmp[head[r]] + gts[1] * (acc_s[head[r]] * inv_l[r])
             + gts[2] * (o_win[head[r]] * inv_lw[r]))
        o = o * lax.rsqrt(jnp.mean(o * o, axis=-1, keepdims=True) + RMS_EPS) * gain_ref[0, r:r + 1, :]
        o_ref[0, :, r * HEAD_DIM:(r + 1) * HEAD_DIM] = o.astype(o_ref.dtype)


def _nsa_attention(proj_a, cmp_kv, gates3, gain, slopes, ovl_t, expand):
    b, s, _ = proj_a.shape
    n_chunks = cmp_kv.shape[3]
    col0 = NSA_WIDTH // HEAD_DIM

    def kv_spec(idx):
        return pl.BlockSpec((1, s, HEAD_DIM), lambda bi, g, i, idx=idx: (bi, 0, col0 + idx * NSA_GROUPS + g))

    cmp_spec = pl.BlockSpec((1, 1, n_chunks, HEAD_DIM), lambda bi, g, i: (bi, g, 0, 0))
    in_specs = [
        pl.BlockSpec(memory_space=pltpu.SMEM),
        pl.BlockSpec((1, NSA_TQ, NSA_REP * HEAD_DIM), lambda bi, g, i: (bi, i, g)),
        cmp_spec, cmp_spec,
        kv_spec(2), kv_spec(3), kv_spec(4), kv_spec(5),
        pl.BlockSpec((1, NSA_TQ, LANES), lambda bi, g, i: (bi, i, 0)),
        pl.BlockSpec((1, NSA_REP, HEAD_DIM), lambda bi, g, i: (g, 0, 0)),
        pl.BlockSpec(ovl_t.shape, lambda bi, g, i: (0, 0)),
        pl.BlockSpec(expand.shape, lambda bi, g, i: (0, 0)),
    ]
    return pl.pallas_call(
        _nsa_kernel,
        grid=(b, NSA_GROUPS, s // NSA_TQ),
        in_specs=in_specs,
        out_specs=pl.BlockSpec((1, NSA_TQ, NSA_REP * HEAD_DIM), lambda bi, g, i: (bi, i, g)),
        out_shape=jax.ShapeDtypeStruct((b, s, NSA_WIDTH), BF16),
        scratch_shapes=[pltpu.VMEM((NSA_TQ, s), F32), pltpu.VMEM((NSA_REP * NSA_TQ, s), F32)],
        compiler_params=_params(3),
        name="nsa_attention",
    )(slopes, proj_a, cmp_kv[0], cmp_kv[1], proj_a, proj_a, proj_a, proj_a, gates3, gain, ovl_t, expand)


def _diff_kernel(slopes_ref, lq1_ref, lk1_ref, lq2_ref, lk2_ref, q_ref, k_ref, v_ref, g_ref,
                 o_ref, s_ref, *, lambda_init):
    h = pl.program_id(1)
    i = pl.program_id(2)
    tq, ck = DIFF_TQ, DIFF_CK
    t0 = i * tq
    slope2 = slopes_ref[h] * LOG2E
    q = q_ref[0]
    qm = [q[:, 0:DIFF_DIM], q[:, DIFF_DIM:2 * DIFF_DIM]]
    t_col = t0 + lax.broadcasted_iota(jnp.int32, (tq, 1), 0)

    def scores(c, mx, diagonal):
        k0 = pl.multiple_of(c * ck, ck)
        kblk = k_ref[0, pl.ds(k0, ck), :]
        pos = k0 + lax.broadcasted_iota(jnp.int32, (1, ck), 1)
        bias = slope2 * (pos - t0).astype(F32)
        if diagonal:
            bias = jnp.where(pos <= t_col, bias, NEG)
        out = []
        for mi in range(2):
            t = _nt_dot(qm[mi], kblk[:, mi * DIFF_DIM:(mi + 1) * DIFF_DIM]) * ATT_SCALE2 + bias
            s_ref[mi, :, pl.ds(k0, ck)] = t
            out.append(jnp.maximum(mx[mi], _lane_fold(t, jnp.maximum)))
        return tuple(out)

    mx = (jnp.full((tq, LANES), NEG, F32),) * 2
    mx = lax.fori_loop(0, i, lambda c, mx: scores(c, mx, False), mx)
    mx = scores(i, mx, True)
    m_row = [jnp.max(mx[mi], axis=-1, keepdims=True) for mi in range(2)]

    def values(c, carry):
        k0 = pl.multiple_of(c * ck, ck)
        vblk = v_ref[0, pl.ds(k0, ck), :]
        out = []
        for mi in range(2):
            l_i, acc = carry[2 * mi], carry[2 * mi + 1]
            p = jnp.exp2(s_ref[mi, :, pl.ds(k0, ck)] - m_row[mi])
            out += [l_i + _lane_fold(p, jnp.add),
                    acc + jnp.dot(p.astype(BF16), vblk, preferred_element_type=F32)]
        return tuple(out)

    init = (jnp.zeros((tq, LANES), F32), jnp.zeros((tq, DIFF_VDIM), F32)) * 2
    l1, a1, l2, a2 = lax.fori_loop(0, i + 1, values, init)
    lam = (jnp.exp(jnp.sum(lq1_ref[...] * lk1_ref[...], axis=-1, keepdims=True))
           - jnp.exp(jnp.sum(lq2_ref[...] * lk2_ref[...], axis=-1, keepdims=True))
           + lambda_init)
    o = (a1 * (1.0 / jnp.sum(l1, axis=-1, keepdims=True))
         - lam * (a2 * (1.0 / jnp.sum(l2, axis=-1, keepdims=True))))
    o = o * lax.rsqrt(jnp.mean(o * o, axis=-1, keepdims=True) + RMS_EPS) * g_ref[...]
    o_ref[0] = (o * (1.0 - lambda_init)).astype(o_ref.dtype)


def _diff_attention(proj_b, lq1, lk1, lq2, lk2, gain, slopes, lambda_init):
    b, s, _ = proj_b.shape
    qcol, kcol, vcol = 0, DIFF_HEADS, 2 * DIFF_HEADS
    vec = pl.BlockSpec((1, DIFF_DIM), lambda bi, h, i: (0, 0))
    in_specs = [
        pl.BlockSpec(memory_space=pltpu.SMEM),
        vec, vec, vec, vec,
        pl.BlockSpec((1, DIFF_TQ, DIFF_VDIM), lambda bi, h, i: (bi, i, qcol + h)),
        pl.BlockSpec((1, s, DIFF_VDIM), lambda bi, h, i: (bi, 0, kcol + h)),
        pl.BlockSpec((1, s, DIFF_VDIM), lambda bi, h, i: (bi, 0, vcol + h)),
        pl.BlockSpec((1, DIFF_VDIM), lambda bi, h, i: (0, 0)),
    ]
    return pl.pallas_call(
        functools.partial(_diff_kernel, lambda_init=lambda_init),
        grid=(b, DIFF_HEADS, s // DIFF_TQ),
        in_specs=in_specs,
        out_specs=pl.BlockSpec((1, DIFF_TQ, DIFF_VDIM), lambda bi, h, i: (bi, i, h)),
        out_shape=jax.ShapeDtypeStruct((b, s, DIFF_WIDTH), BF16),
        scratch_shapes=[pltpu.VMEM((2, DIFF_TQ, s), F32)],
        compiler_params=_params(3),
        name="diff_attention",
    )(slopes, lq1.reshape(1, -1), lk1.reshape(1, -1), lq2.reshape(1, -1), lk2.reshape(1, -1),
      proj_b, proj_b, proj_b, gain.reshape(1, -1))


def _alibi_slopes(n_heads):
    return np.array([2.0 ** (-8.0 * (h + 1) / n_heads) for h in range(n_heads)], np.float32)


def _overlap_t(seq):
    n_cmp = seq // CMP_STRIDE
    n_slc = seq // SLC_BLOCK
    cs = np.arange(n_cmp) * CMP_STRIDE
    ss = np.arange(n_slc) * SLC_BLOCK
    ov = np.clip(np.minimum(cs[:, None] + CMP_BLOCK, ss[None, :] + SLC_BLOCK)
                 - np.maximum(cs[:, None], ss[None, :]), 0, None)
    ov[(seq - CMP_BLOCK) // CMP_STRIDE + 1:, :] = 0
    return np.ascontiguousarray((ov / CMP_BLOCK).astype(np.float32).T)


def _expand(seq):
    n_slc = seq // SLC_BLOCK
    return (np.arange(seq)[None, :] // SLC_BLOCK == np.arange(n_slc)[:, None]).astype(np.float32)


def _layer(x, w_in, cmp_pe_k, cmp_w1_k, cmp_w2_k, cmp_pe_v, cmp_w1_v, cmp_w2_v, nsa_out_g,
           lambda_q1, lambda_k1, lambda_q2, lambda_k2, diff_subln_g, w_out, ln1_g, ln1_b,
           w_ff1, w_ff2, ln2_g, ln2_b, lambda_init):
    b, s, d = x.shape
    t = b * s
    x2 = x.reshape(t, d)
    x_bf = x2.astype(BF16)

    proj_a = _wmm(x_bf, w_in, BF16, col0=0, n=PROJ_A_COLS, name="in_proj_nsa")
    proj_b = _wmm(x_bf, w_in, BF16, col0=GATE_START, n=PROJ_B_COLS, shift=GATE_COLS,
                  name="in_proj_diff")
    gates = _wmm(x_bf, w_in, F32, col0=GATE_START, n=LANES, name="gate_proj")
    proj_a = proj_a.reshape(b, s, PROJ_A_COLS)
    proj_b = proj_b.reshape(b, s, PROJ_B_COLS)

    n_chunks = s // CMP_STRIDE
    kv_cmp = proj_a[:, :, NSA_WIDTH:NSA_WIDTH + 2 * NSA_GROUPS * HEAD_DIM]
    kv_cmp = kv_cmp.reshape(b, n_chunks, CMP_STRIDE, 2, NSA_GROUPS, HEAD_DIM)
    chunks = kv_cmp.transpose(3, 0, 4, 1, 2, 5).reshape(2, b, NSA_GROUPS, n_chunks,
                                                        CMP_STRIDE * HEAD_DIM)
    pe = jnp.stack([cmp_pe_k, cmp_pe_v]).reshape(2, 1, CMP_BLOCK * HEAD_DIM).astype(BF16)
    w1 = jnp.stack([cmp_w1_k, cmp_w1_v]).astype(BF16)
    w2 = jnp.stack([cmp_w2_k, cmp_w2_v]).astype(BF16)
    cmp_kv = _compress(chunks, pe, w1, w2)

    o_nsa = _nsa_attention(
        proj_a, cmp_kv, gates.reshape(b, s, LANES),
        nsa_out_g.reshape(NSA_GROUPS, NSA_REP, HEAD_DIM),
        jnp.asarray(_alibi_slopes(NSA_HEADS)),
        jnp.asarray(_overlap_t(s)), jnp.asarray(_expand(s)).astype(BF16))
    o_diff = _diff_attention(proj_b, lambda_q1, lambda_k1, lambda_q2, lambda_k2, diff_subln_g,
                             jnp.asarray(_alibi_slopes(DIFF_HEADS)), lambda_init)

    h1 = _outproj(o_nsa.reshape(t, NSA_WIDTH), o_diff.reshape(t, DIFF_WIDTH), w_out, x2)
    x1, x1_bf = _layer_norm(h1, ln1_g, ln1_b, (F32, BF16))
    hid = _wmm(x1_bf, w_ff1, BF16, col0=0, n=D_FF, relu2=True, name="ff1")
    h2 = _ff2(hid, w_ff2, x1)
    (out,) = _layer_norm(h2, ln2_g, ln2_b, (F32,))
    return out.reshape(b, s, d)


def kernel(x, w_in, cmp_pe_k, cmp_w1_k, cmp_w2_k, cmp_pe_v, cmp_w1_v, cmp_w2_v, nsa_out_g,
           lambda_q1, lambda_k1, lambda_q2, lambda_k2, diff_subln_g, w_out, ln1_g, ln1_b,
           w_ff1, w_ff2, ln2_g, ln2_b):
    for l in range(DEPTH):
        lambda_init = 0.8 - 0.6 * math.exp(-0.3 * l)
        x = _layer(x, w_in[l], cmp_pe_k[l], cmp_w1_k[l], cmp_w2_k[l], cmp_pe_v[l], cmp_w1_v[l],
                   cmp_w2_v[l], nsa_out_g[l], lambda_q1[l], lambda_k1[l], lambda_q2[l],
                   lambda_k2[l], diff_subln_g[l], w_out[l], ln1_g[l], ln1_b[l], w_ff1[l],
                   w_ff2[l], ln2_g[l], ln2_b[l], lambda_init)
    return x
```

```python
import functools
import math

import numpy as np
import jax
import jax.numpy as jnp
from jax import lax
from jax.experimental import pallas as pl
from jax.experimental.pallas import tpu as pltpu

F32 = jnp.float32
BF16 = jnp.bfloat16

D_MODEL = 4096
HEAD_DIM = 128
NSA_HEADS = 16
NSA_GROUPS = 4
NSA_REP = 4
CMP_BLOCK = 32
CMP_STRIDE = 16
CMP_HIDDEN = 256
SLC_BLOCK = 64
SLC_TOPK = 16
WINDOW = 512
DIFF_HEADS = 8
DIFF_DIM = 128
DIFF_VDIM = 256
NSA_WIDTH = NSA_HEADS * HEAD_DIM
KV_WIDTH = 6 * NSA_GROUPS * HEAD_DIM
GATE_COLS = 3 * NSA_HEADS
GATE_START = NSA_WIDTH + KV_WIDTH
DIFF_WIDTH = DIFF_HEADS * DIFF_VDIM
PROJ_A_COLS = GATE_START
PROJ_B_COLS = 3 * DIFF_WIDTH
D_FF = 4 * D_MODEL
LN_EPS = 1e-5
RMS_EPS = 1e-6
NEG = -1e30
DEPTH = 1
DEEPNORM_ALPHA = (2.0 * DEPTH) ** 0.25
ATT_SCALE = HEAD_DIM ** -0.5
LOG2E = 1.4426950408889634
ATT_SCALE2 = ATT_SCALE * LOG2E

LANES = 128
BF16_SUBLANES = 16
VMEM_LIMIT_BYTES = 56 * 1024 * 1024

MM_BM = 1024
MM_BN = 1024
INPROJ_BN = 512
OUTPROJ_BN = 512
FF2_BK = 2048
CAST_COLS = 512
LN_ROWS = 256
NSA_TQ = 256
SLC_CK = 512
WIN_SPAN = WINDOW + NSA_TQ
DIFF_TQ = 512
DIFF_CK = 512


def _params(n_axes):
    return pltpu.CompilerParams(dimension_semantics=("arbitrary",) * n_axes,
                                vmem_limit_bytes=VMEM_LIMIT_BYTES)


def _load_weight_rows(b_ref, tail_ref, shift, cols):
    if shift:
        return jnp.concatenate([b_ref[shift:, cols], tail_ref[0:shift, cols]], axis=0)
    return b_ref[:, cols]


def _inproj_kernel(*refs, shift):
    if shift:
        a_ref, b_ref, tail_ref, o_ref, wb_ref = refs
    else:
        a_ref, b_ref, o_ref, wb_ref = refs
        tail_ref = None

    @pl.when(pl.program_id(1) == 0)
    def _():
        k = wb_ref.shape[0]
        for c0 in range(0, k, CAST_COLS):
            blk = _load_weight_rows(b_ref, tail_ref, shift, slice(c0, c0 + CAST_COLS))
            wb_ref[c0:c0 + CAST_COLS, :] = blk.T.astype(BF16)

    o_ref[...] = jnp.dot(a_ref[...], wb_ref[...], preferred_element_type=F32).astype(o_ref.dtype)


def _inproj(a, wt, out_dtype, *, row0, n, shift=0, bn=INPROJ_BN, bm=MM_BM, name):
    m, k = a.shape
    bn = min(bn, n)
    jb = row0 // bn
    in_specs = [pl.BlockSpec((bm, k), lambda j, i: (i, 0)),
                pl.BlockSpec((bn, k), lambda j, i: (jb + j, 0))]
    args = [a, wt]
    if shift:
        per = bn // LANES
        tb = row0 // LANES
        in_specs.append(pl.BlockSpec((LANES, k), lambda j, i: (tb + per * (j + 1), 0)))
        args.append(wt)
    return pl.pallas_call(
        functools.partial(_inproj_kernel, shift=shift),
        grid=(n // bn, m // bm),
        in_specs=in_specs,
        out_specs=pl.BlockSpec((bm, bn), lambda j, i: (i, j)),
        out_shape=jax.ShapeDtypeStruct((m, n), out_dtype),
        scratch_shapes=[pltpu.VMEM((k, bn), BF16)],
        compiler_params=_params(2),
        name=name,
    )(*args)


def _mm_kernel(a_ref, b_ref, o_ref):
    acc = jnp.dot(a_ref[...], b_ref[...], preferred_element_type=F32)
    o_ref[...] = jnp.square(jnp.maximum(acc, 0.0)).astype(o_ref.dtype)


def _ff1(a, b, *, bm=MM_BM, bn=MM_BN):
    m, k = a.shape
    _, n = b.shape
    return pl.pallas_call(
        _mm_kernel,
        grid=(m // bm, n // bn),
        in_specs=[pl.BlockSpec((bm, k), lambda i, j: (i, 0)),
                  pl.BlockSpec((k, bn), lambda i, j: (0, j))],
        out_specs=pl.BlockSpec((bm, bn), lambda i, j: (i, j)),
        out_shape=jax.ShapeDtypeStruct((m, n), BF16),
        compiler_params=_params(2),
        name="ff1",
    )(a, b)


def _outproj_kernel(a1_ref, a2_ref, b_ref, res_ref, o_ref):
    half = a1_ref.shape[1]
    acc = jnp.dot(a1_ref[...], b_ref[0:half, :], preferred_element_type=F32)
    acc = acc + jnp.dot(a2_ref[...], b_ref[half:2 * half, :], preferred_element_type=F32)
    o_ref[...] = DEEPNORM_ALPHA * res_ref[...] + acc


def _outproj(a1, a2, b, res, *, bm=MM_BM, bn=OUTPROJ_BN):
    m, k1 = a1.shape
    _, n = b.shape
    return pl.pallas_call(
        _outproj_kernel,
        grid=(m // bm, n // bn),
        in_specs=[pl.BlockSpec((bm, k1), lambda i, j: (i, 0)),
                  pl.BlockSpec((bm, k1), lambda i, j: (i, 0)),
                  pl.BlockSpec((2 * k1, bn), lambda i, j: (0, j)),
                  pl.BlockSpec((bm, bn), lambda i, j: (i, j))],
        out_specs=pl.BlockSpec((bm, bn), lambda i, j: (i, j)),
        out_shape=jax.ShapeDtypeStruct((m, n), F32),
        compiler_params=_params(2),
        name="outproj",
    )(a1, a2, b, res)


def _ff2_kernel(a_ref, b_ref, res_ref, o_ref, acc_ref):
    kk = pl.program_id(2)

    @pl.when(kk == 0)
    def _():
        acc_ref[...] = jnp.zeros_like(acc_ref)

    acc_ref[...] += jnp.dot(a_ref[...], b_ref[...], preferred_element_type=F32)

    @pl.when(kk == pl.num_programs(2) - 1)
    def _():
        o_ref[...] = DEEPNORM_ALPHA * res_ref[...] + acc_ref[...]


def _ff2(a, b, res, *, bm=MM_BM, bn=MM_BN, bk=FF2_BK):
    m, k = a.shape
    _, n = b.shape
    return pl.pallas_call(
        _ff2_kernel,
        grid=(m // bm, n // bn, k // bk),
        in_specs=[pl.BlockSpec((bm, bk), lambda i, j, kk: (i, kk)),
                  pl.BlockSpec((bk, bn), lambda i, j, kk: (kk, j)),
                  pl.BlockSpec((bm, bn), lambda i, j, kk: (i, j))],
        out_specs=pl.BlockSpec((bm, bn), lambda i, j, kk: (i, j)),
        out_shape=jax.ShapeDtypeStruct((m, n), F32),
        scratch_shapes=[pltpu.VMEM((bm, bn), F32)],
        compiler_params=_params(3),
        name="ff2",
    )(a, b, res)


def _cast_blocks(src_refs, dst_refs):
    for src, dst in zip(src_refs, dst_refs):
        dst[...] = src[...].astype(dst.dtype)


def _cast_specs(arrays, n_steps, step_of):
    specs, shapes = [], []
    for arr in arrays:
        rows, cols = arr.shape
        assert rows % (n_steps * BF16_SUBLANES) == 0
        blk = (rows // n_steps, cols)
        specs.append(pl.BlockSpec(blk, lambda *ids: (step_of(*ids), 0)))
        shapes.append(jax.ShapeDtypeStruct((rows, cols), BF16))
    return specs, shapes


def _ln_kernel(x_ref, g_ref, b_ref, *o_refs):
    x = x_ref[...]
    mu = jnp.mean(x, axis=-1, keepdims=True)
    xc = x - mu
    var = jnp.mean(xc * xc, axis=-1, keepdims=True)
    y = xc * lax.rsqrt(var + LN_EPS) * g_ref[...] + b_ref[...]
    for o_ref in o_refs:
        o_ref[...] = y.astype(o_ref.dtype)


def _layer_norm(x, g, b, out_dtypes):
    m, d = x.shape
    row_spec = pl.BlockSpec((LN_ROWS, d), lambda i: (i, 0))
    vec_spec = pl.BlockSpec((1, d), lambda i: (0, 0))
    return pl.pallas_call(
        _ln_kernel,
        grid=(m // LN_ROWS,),
        in_specs=[row_spec, vec_spec, vec_spec],
        out_specs=[row_spec for _ in out_dtypes],
        out_shape=[jax.ShapeDtypeStruct((m, d), dt) for dt in out_dtypes],
        compiler_params=_params(1),
        name="layer_norm",
    )(x, g.reshape(1, d), b.reshape(1, d))


def _gelu_tanh(x):
    c = math.sqrt(2.0 / math.pi)
    return 0.5 * x * (1.0 + jnp.tanh(c * (x + 0.044715 * (x * x * x))))


def _compress_kernel(c_ref, pe_ref, w1_ref, w2_ref, o_ref):
    half = CMP_STRIDE * HEAD_DIM
    c = c_ref[0, 0, 0]
    n_chunks = c.shape[0]
    y1 = jnp.dot(c, w1_ref[0, 0:half, :], preferred_element_type=F32)
    y2 = jnp.dot(c, w1_ref[0, half:2 * half, :], preferred_element_type=F32)
    pe8 = jnp.broadcast_to(pe_ref[0], (8, 2 * half))
    pb = jnp.dot(pe8, w1_ref[0], preferred_element_type=F32)[0:1, :]
    h = y1 + pltpu.roll(y2, n_chunks - 1, 0) + pb
    a = _gelu_tanh(h).astype(BF16)
    out = jnp.dot(a, w2_ref[0], preferred_element_type=F32)
    row = lax.broadcasted_iota(jnp.int32, out.shape, 0)
    out = jnp.where(row < n_chunks - 1, out, 0.0)
    o_ref[0, 0, 0] = out.astype(o_ref.dtype)


def _compress(chunks, pe, w1, w2):
    _, b, g, n_chunks, width = chunks.shape
    return pl.pallas_call(
        _compress_kernel,
        grid=(2, b, g),
        in_specs=[pl.BlockSpec((1, 1, 1, n_chunks, width), lambda t, i, j: (t, i, j, 0, 0)),
                  pl.BlockSpec((1, 1, 2 * width), lambda t, i, j: (t, 0, 0)),
                  pl.BlockSpec((1, 2 * width, CMP_HIDDEN), lambda t, i, j: (t, 0, 0)),
                  pl.BlockSpec((1, CMP_HIDDEN, HEAD_DIM), lambda t, i, j: (t, 0, 0))],
        out_specs=pl.BlockSpec((1, 1, 1, n_chunks, HEAD_DIM), lambda t, i, j: (t, i, j, 0, 0)),
        out_shape=jax.ShapeDtypeStruct((2, b, g, n_chunks, HEAD_DIM), BF16),
        compiler_params=_params(3),
        name="compress",
    )(chunks, pe, w1, w2)


def _nt_dot(a, b):
    return lax.dot_general(a, b, (((1,), (1,)), ((), ())), preferred_element_type=F32)


def _lane_fold(x, op):
    out = x[:, 0:LANES]
    for c0 in range(LANES, x.shape[1], LANES):
        out = op(out, x[:, c0:c0 + LANES])
    return out


def _nsa_kernel(slopes_ref, q_ref, kc_ref, vc_ref, ks_ref, vs_ref, kw_ref, vw_ref,
                gate_ref, gain_ref, ovl_ref, expand_ref, *rest, n_cast):
    cast_src, o_ref, cast_dst = rest[:n_cast], rest[n_cast], rest[n_cast + 1:2 * n_cast + 1]
    mask_ref, s_ref = rest[2 * n_cast + 1:]
    _cast_blocks(cast_src, cast_dst)
    g = pl.program_id(1)
    i = pl.program_id(2)
    tq = NSA_TQ
    rows = NSA_REP * tq
    t0 = i * tq
    q = q_ref[0]
    qs = jnp.concatenate([q[:, r * HEAD_DIM:(r + 1) * HEAD_DIM] for r in range(NSA_REP)],
                         axis=0)
    t_col = t0 + lax.broadcasted_iota(jnp.int32, (tq, 1), 0)
    head = [slice(r * tq, (r + 1) * tq) for r in range(NSA_REP)]
    slope2 = [slopes_ref[g * NSA_REP + r] * LOG2E for r in range(NSA_REP)]

    n_cmp = kc_ref.shape[2]
    s_cmp = _nt_dot(qs, kc_ref[0, 0]) * ATT_SCALE2
    n_row = lax.broadcasted_iota(jnp.int32, (1, n_cmp), 1)
    cmask = (n_row * CMP_STRIDE + (CMP_BLOCK - 1)) <= t_col
    p_sum = jnp.zeros((tq, n_cmp), F32)
    p_parts = []
    for r in range(NSA_REP):
        s = jnp.where(cmask, s_cmp[head[r]], NEG)
        m = jnp.max(s, axis=-1, keepdims=True)
        e = jnp.where(cmask, jnp.exp2(s - m), 0.0)
        l = jnp.sum(e, axis=-1, keepdims=True)
        p = e * (1.0 / jnp.maximum(l, 1e-30))
        p_sum = p_sum + p
        p_parts.append(p.astype(BF16))
    o_cmp = jnp.dot(jnp.concatenate(p_parts, axis=0), vc_ref[0, 0],
                    preferred_element_type=F32)

    n_slc = ovl_ref.shape[0]
    imp_t = lax.dot_general(ovl_ref[...], p_sum, (((1,), (1,)), ((), ())),
                            precision=lax.Precision.HIGHEST,
                            preferred_element_type=F32)
    j_t = lax.broadcasted_iota(jnp.int32, (n_slc, tq), 0)
    t_t = t0 + lax.broadcasted_iota(jnp.int32, (n_slc, tq), 1)
    causal = j_t * SLC_BLOCK <= t_t
    tb = lax.shift_right_logical(t_t, int(math.log2(SLC_BLOCK)))
    forced = jnp.where(j_t == 0, 1e6, jnp.where(j_t == tb, 1e6, jnp.where(j_t == tb - 1, 1e6, imp_t)))
    score = jnp.where(causal, forced, NEG)
    rank = jnp.zeros((n_slc, tq), F32)
    for jp in range(n_slc):
        row = score[jp:jp + 1, :]
        ge = jnp.where(row >= score, 1.0, 0.0)
        gt = jnp.where(row > score, 1.0, 0.0)
        rank = rank + jnp.where(j_t > jp, ge, gt)
    sel_bias = jnp.where(causal, jnp.where(rank < SLC_TOPK, 0.0, NEG), NEG).astype(BF16)
    mask_ref[...] = lax.dot_general(sel_bias, expand_ref[...], (((0,), (0,)), ((), ())),
                                    preferred_element_type=F32)

    ck = SLC_CK
    n_chunks = (t0 + tq + ck - 1) // ck
    kd = pl.multiple_of((n_chunks - 1) * ck, ck)
    pos_d = kd + lax.broadcasted_iota(jnp.int32, (1, ck), 1)
    mask_ref[:, pl.ds(kd, ck)] = jnp.where(pos_d <= t_col, mask_ref[:, pl.ds(kd, ck)], NEG)

    def slc_scores(c, mx):
        k0 = pl.multiple_of(c * ck, ck)
        s = _nt_dot(qs, ks_ref[0, pl.ds(k0, ck), :])
        posrel = ((k0 - t0) + lax.broadcasted_iota(jnp.int32, (1, ck), 1)).astype(F32)
        mb = mask_ref[:, pl.ds(k0, ck)]
        out = []
        for r in range(NSA_REP):
            t = s[head[r]] * ATT_SCALE2 + slope2[r] * posrel + mb
            s_ref[head[r], pl.ds(k0, ck)] = t
            out.append(jnp.maximum(mx[r], _lane_fold(t, jnp.maximum)))
        return tuple(out)

    mx = lax.fori_loop(0, n_chunks, slc_scores,
                       tuple(jnp.full((tq, LANES), NEG, F32) for _ in range(NSA_REP)))
    m_slc = [jnp.max(mx[r], axis=-1, keepdims=True) for r in range(NSA_REP)]

    def slc_values(c, carry):
        ls, acc = carry[:NSA_REP], carry[NSA_REP]
        k0 = pl.multiple_of(c * ck, ck)
        ps, new_ls = [], []
        for r in range(NSA_REP):
            p = jnp.exp2(s_ref[head[r], pl.ds(k0, ck)] - m_slc[r])
            new_ls.append(ls[r] + _lane_fold(p, jnp.add))
            ps.append(p.astype(BF16))
        acc = acc + jnp.dot(jnp.concatenate(ps, axis=0), vs_ref[0, pl.ds(k0, ck), :],
                            preferred_element_type=F32)
        return tuple(new_ls) + (acc,)

    init = tuple(jnp.zeros((tq, LANES), F32) for _ in range(NSA_REP)) + (jnp.zeros((rows, HEAD_DIM), F32),)
    res = lax.fori_loop(0, n_chunks, slc_values, init)
    acc_s = res[NSA_REP]
    inv_l = [1.0 / jnp.sum(res[r], axis=-1, keepdims=True) for r in range(NSA_REP)]

    k0 = pl.multiple_of(jnp.maximum(t0 - WINDOW, 0), tq)
    kblk = kw_ref[0, pl.ds(k0, WIN_SPAN), :]
    vblk = vw_ref[0, pl.ds(k0, WIN_SPAN), :]
    pos = k0 + lax.broadcasted_iota(jnp.int32, (1, WIN_SPAN), 1)
    dist = t_col - pos
    wbias = jnp.where(dist >= 0, jnp.where(dist < WINDOW, 0.0, NEG), NEG)
    posrel = (pos - t0).astype(F32)
    s_win = _nt_dot(qs, kblk)
    ps, inv_lw = [], []
    for r in range(NSA_REP):
        t = s_win[head[r]] * ATT_SCALE2 + slope2[r] * posrel + wbias
        m = jnp.max(t, axis=-1, keepdims=True)
        p = jnp.exp2(t - m)
        inv_lw.append(1.0 / jnp.sum(p, axis=-1, keepdims=True))
        ps.append(p.astype(BF16))
    o_win = jnp.dot(jnp.concatenate(ps, axis=0), vblk, preferred_element_type=F32)

    sig = jax.nn.sigmoid(gate_ref[0])
    lane = lax.broadcasted_iota(jnp.int32, sig.shape, 1)
    for r in range(NSA_REP):
        base = (g * NSA_REP + r) * 3
        gts = [jnp.sum(jnp.where(lane == base + br, sig, 0.0), axis=-1, keepdims=True)
               for br in range(3)]
        o = (gts[0] * o_cmp[head[r]] + gts[1] * (acc_s[head[r]] * inv_l[r])
             + gts[2] * (o_win[head[r]] * inv_lw[r]))
        o = o * lax.rsqrt(jnp.mean(o * o, axis=-1, keepdims=True) + RMS_EPS) * gain_ref[0, r:r + 1, :]
        o_ref[0, :, r * HEAD_DIM:(r + 1) * HEAD_DIM] = o.astype(o_ref.dtype)


def _nsa_attention(proj_a, cmp_kv, gates3, gain, slopes, ovl_t, expand, cast_arrays):
    b, s, _ = proj_a.shape
    n_chunks = cmp_kv.shape[3]
    n_q = s // NSA_TQ
    cast_specs, cast_shapes = _cast_specs(cast_arrays, b * NSA_GROUPS * n_q,
                                          lambda bi, g, i: (bi * NSA_GROUPS + g) * n_q + i)
    col0 = NSA_WIDTH // HEAD_DIM

    def kv_spec(idx):
        return pl.BlockSpec((1, s, HEAD_DIM), lambda bi, g, i, idx=idx: (bi, 0, col0 + idx * NSA_GROUPS + g))

    cmp_spec = pl.BlockSpec((1, 1, n_chunks, HEAD_DIM), lambda bi, g, i: (bi, g, 0, 0))
    in_specs = [
        pl.BlockSpec(memory_space=pltpu.SMEM),
        pl.BlockSpec((1, NSA_TQ, NSA_REP * HEAD_DIM), lambda bi, g, i: (bi, i, g)),
        cmp_spec, cmp_spec,
        kv_spec(2), kv_spec(3), kv_spec(4), kv_spec(5),
        pl.BlockSpec((1, NSA_TQ, LANES), lambda bi, g, i: (bi, i, 0)),
        pl.BlockSpec((1, NSA_REP, HEAD_DIM), lambda bi, g, i: (g, 0, 0)),
        pl.BlockSpec(ovl_t.shape, lambda bi, g, i: (0, 0)),
        pl.BlockSpec(expand.shape, lambda bi, g, i: (0, 0)),
    ] + cast_specs
    return pl.pallas_call(
        functools.partial(_nsa_kernel, n_cast=len(cast_arrays)),
        grid=(b, NSA_GROUPS, n_q),
        in_specs=in_specs,
        out_specs=[pl.BlockSpec((1, NSA_TQ, NSA_REP * HEAD_DIM), lambda bi, g, i: (bi, i, g))] + cast_specs,
        out_shape=[jax.ShapeDtypeStruct((b, s, NSA_WIDTH), BF16)] + cast_shapes,
        scratch_shapes=[pltpu.VMEM((NSA_TQ, s), F32), pltpu.VMEM((NSA_REP * NSA_TQ, s), F32)],
        compiler_params=_params(3),
        name="nsa_attention",
    )(slopes, proj_a, cmp_kv[0], cmp_kv[1], proj_a, proj_a, proj_a, proj_a, gates3, gain, ovl_t, expand,
      *cast_arrays)


def _diff_kernel(slopes_ref, lq1_ref, lk1_ref, lq2_ref, lk2_ref, q_ref, k_ref, v_ref, g_ref,
                 *rest, n_cast, lambda_init):
    cast_src, o_ref, cast_dst = rest[:n_cast], rest[n_cast], rest[n_cast + 1:2 * n_cast + 1]
    (s_ref,) = rest[2 * n_cast + 1:]
    _cast_blocks(cast_src, cast_dst)
    h = pl.program_id(1)
    i = pl.program_id(2)
    tq, ck = DIFF_TQ, DIFF_CK
    t0 = i * tq
    slope2 = slopes_ref[h] * LOG2E
    q = q_ref[0]
    qm = [q[:, 0:DIFF_DIM], q[:, DIFF_DIM:2 * DIFF_DIM]]
    t_col = t0 + lax.broadcasted_iota(jnp.int32, (tq, 1), 0)

    def scores(c, mx, diagonal):
        k0 = pl.multiple_of(c * ck, ck)
        kblk = k_ref[0, pl.ds(k0, ck), :]
        pos = k0 + lax.broadcasted_iota(jnp.int32, (1, ck), 1)
        bias = slope2 * (pos - t0).astype(F32)
        if diagonal:
            bias = jnp.where(pos <= t_col, bias, NEG)
        out = []
        for mi in range(2):
            t = _nt_dot(qm[mi], kblk[:, mi * DIFF_DIM:(mi + 1) * DIFF_DIM]) * ATT_SCALE2 + bias
            s_ref[mi, :, pl.ds(k0, ck)] = t
            out.append(jnp.maximum(mx[mi], _lane_fold(t, jnp.maximum)))
        return tuple(out)

    mx = (jnp.full((tq, LANES), NEG, F32),) * 2
    mx = lax.fori_loop(0, i, lambda c, mx: scores(c, mx, False), mx)
    mx = scores(i, mx, True)
    m_row = [jnp.max(mx[mi], axis=-1, keepdims=True) for mi in range(2)]

    def values(c, carry):
        k0 = pl.multiple_of(c * ck, ck)
        vblk = v_ref[0, pl.ds(k0, ck), :]
        out = []
        for mi in range(2):
            l_i, acc = carry[2 * mi], carry[2 * mi + 1]
            p = jnp.exp2(s_ref[mi, :, pl.ds(k0, ck)] - m_row[mi])
            out += [l_i + _lane_fold(p, jnp.add),
                    acc + jnp.dot(p.astype(BF16), vblk, preferred_element_type=F32)]
        return tuple(out)

    init = (jnp.zeros((tq, LANES), F32), jnp.zeros((tq, DIFF_VDIM), F32)) * 2
    l1, a1, l2, a2 = lax.fori_loop(0, i + 1, values, init)
    lam = (jnp.exp(jnp.sum(lq1_ref[...] * lk1_ref[...], axis=-1, keepdims=True))
           - jnp.exp(jnp.sum(lq2_ref[...] * lk2_ref[...], axis=-1, keepdims=True))
           + lambda_init)
    o = (a1 * (1.0 / jnp.sum(l1, axis=-1, keepdims=True))
         - lam * (a2 * (1.0 / jnp.sum(l2, axis=-1, keepdims=True))))
    o = o * lax.rsqrt(jnp.mean(o * o, axis=-1, keepdims=True) + RMS_EPS) * g_ref[...]
    o_ref[0] = (o * (1.0 - lambda_init)).astype(o_ref.dtype)


def _diff_attention(proj_b, lq1, lk1, lq2, lk2, gain, slopes, lambda_init, cast_arrays):
    b, s, _ = proj_b.shape
    n_q = s // DIFF_TQ
    cast_specs, cast_shapes = _cast_specs(cast_arrays, b * DIFF_HEADS * n_q,
                                          lambda bi, h, i: (bi * DIFF_HEADS + h) * n_q + i)
    qcol, kcol, vcol = 0, DIFF_HEADS, 2 * DIFF_HEADS
    vec = pl.BlockSpec((1, DIFF_DIM), lambda bi, h, i: (0, 0))
    in_specs = [
        pl.BlockSpec(memory_space=pltpu.SMEM),
        vec, vec, vec, vec,
        pl.BlockSpec((1, DIFF_TQ, DIFF_VDIM), lambda bi, h, i: (bi, i, qcol + h)),
        pl.BlockSpec((1, s, DIFF_VDIM), lambda bi, h, i: (bi, 0, kcol + h)),
        pl.BlockSpec((1, s, DIFF_VDIM), lambda bi, h, i: (bi, 0, vcol + h)),
        pl.BlockSpec((1, DIFF_VDIM), lambda bi, h, i: (0, 0)),
    ] + cast_specs
    return pl.pallas_call(
        functools.partial(_diff_kernel, n_cast=len(cast_arrays), lambda_init=lambda_init),
        grid=(b, DIFF_HEADS, n_q),
        in_specs=in_specs,
        out_specs=[pl.BlockSpec((1, DIFF_TQ, DIFF_VDIM), lambda bi, h, i: (bi, i, h))] + cast_specs,
        out_shape=[jax.ShapeDtypeStruct((b, s, DIFF_WIDTH), BF16)] + cast_shapes,
        scratch_shapes=[pltpu.VMEM((2, DIFF_TQ, s), F32)],
        compiler_params=_params(3),
        name="diff_attention",
    )(slopes, lq1.reshape(1, -1), lk1.reshape(1, -1), lq2.reshape(1, -1), lk2.reshape(1, -1),
      proj_b, proj_b, proj_b, gain.reshape(1, -1), *cast_arrays)


def _alibi_slopes(n_heads):
    return np.array([2.0 ** (-8.0 * (h + 1) / n_heads) for h in range(n_heads)], np.float32)


def _overlap_t(seq):
    n_cmp = seq // CMP_STRIDE
    n_slc = seq // SLC_BLOCK
    cs = np.arange(n_cmp) * CMP_STRIDE
    ss = np.arange(n_slc) * SLC_BLOCK
    ov = np.clip(np.minimum(cs[:, None] + CMP_BLOCK, ss[None, :] + SLC_BLOCK)
                 - np.maximum(cs[:, None], ss[None, :]), 0, None)
    ov[(seq - CMP_BLOCK) // CMP_STRIDE + 1:, :] = 0
    return np.ascontiguousarray((ov / CMP_BLOCK).astype(np.float32).T)


def _expand(seq):
    n_slc = seq // SLC_BLOCK
    return (np.arange(seq)[None, :] // SLC_BLOCK == np.arange(n_slc)[:, None]).astype(np.float32)


def _layer(x, w_in, cmp_pe_k, cmp_w1_k, cmp_w2_k, cmp_pe_v, cmp_w1_v, cmp_w2_v, nsa_out_g,
           lambda_q1, lambda_k1, lambda_q2, lambda_k2, diff_subln_g, w_out, ln1_g, ln1_b,
           w_ff1, w_ff2, ln2_g, ln2_b, lambda_init):
    b, s, d = x.shape
    t = b * s
    x2 = x.reshape(t, d)
    x_bf = x2.astype(BF16)

    w_t = jnp.swapaxes(w_in, 0, 1)
    proj_a = _inproj(x_bf, w_t, BF16, row0=0, n=PROJ_A_COLS, name="in_proj_nsa")
    proj_b = _inproj(x_bf, w_t, BF16, row0=GATE_START, n=PROJ_B_COLS, shift=GATE_COLS,
                     name="in_proj_diff")
    gates = _inproj(x_bf, w_t, F32, row0=GATE_START, n=LANES, name="gate_proj")
    proj_a = proj_a.reshape(b, s, PROJ_A_COLS)
    proj_b = proj_b.reshape(b, s, PROJ_B_COLS)

    n_chunks = s // CMP_STRIDE
    kv_cmp = proj_a[:, :, NSA_WIDTH:NSA_WIDTH + 2 * NSA_GROUPS * HEAD_DIM]
    kv_cmp = kv_cmp.reshape(b, n_chunks, CMP_STRIDE, 2, NSA_GROUPS, HEAD_DIM)
    chunks = kv_cmp.transpose(3, 0, 4, 1, 2, 5).reshape(2, b, NSA_GROUPS, n_chunks,
                                                        CMP_STRIDE * HEAD_DIM)
    pe = jnp.stack([cmp_pe_k, cmp_pe_v]).reshape(2, 1, CMP_BLOCK * HEAD_DIM).astype(BF16)
    w1 = jnp.stack([cmp_w1_k, cmp_w1_v]).astype(BF16)
    w2 = jnp.stack([cmp_w2_k, cmp_w2_v]).astype(BF16)
    cmp_kv = _compress(chunks, pe, w1, w2)

    o_nsa, w_ff1_bf, w_out_bf = _nsa_attention(
        proj_a, cmp_kv, gates.reshape(b, s, LANES),
        nsa_out_g.reshape(NSA_GROUPS, NSA_REP, HEAD_DIM),
        jnp.asarray(_alibi_slopes(NSA_HEADS)),
        jnp.asarray(_overlap_t(s)), jnp.asarray(_expand(s)).astype(BF16), [w_ff1, w_out])
    o_diff, w_ff2_bf = _diff_attention(proj_b, lambda_q1, lambda_k1, lambda_q2, lambda_k2,
                                       diff_subln_g, jnp.asarray(_alibi_slopes(DIFF_HEADS)),
                                       lambda_init, [w_ff2])

    h1 = _outproj(o_nsa.reshape(t, NSA_WIDTH), o_diff.reshape(t, DIFF_WIDTH), w_out_bf, x2)
    x1, x1_bf = _layer_norm(h1, ln1_g, ln1_b, (F32, BF16))
    hid = _ff1(x1_bf, w_ff1_bf)
    h2 = _ff2(hid, w_ff2_bf, x1)
    (out,) = _layer_norm(h2, ln2_g, ln2_b, (F32,))
    return out.reshape(b, s, d)


def kernel(x, w_in, cmp_pe_k, cmp_w1_k, cmp_w2_k, cmp_pe_v, cmp_w1_v, cmp_w2_v, nsa_out_g,
           lambda_q1, lambda_k1, lambda_q2, lambda_k2, diff_subln_g, w_out, ln1_g, ln1_b,
           w_ff1, w_ff2, ln2_g, ln2_b):
    for l in range(DEPTH):
        lambda_init = 0.8 - 0.6 * math.exp(-0.3 * l)
        x = _layer(x, w_in[l], cmp_pe_k[l], cmp_w1_k[l], cmp_w2_k[l], cmp_pe_v[l], cmp_w1_v[l],
                   cmp_w2_v[l], nsa_out_g[l], lambda_q1[l], lambda_k1[l], lambda_q2[l],
                   lambda_k2[l], diff_subln_g[l], w_out[l], ln1_g[l], ln1_b[l], w_ff1[l],
                   w_ff2[l], ln2_g[l], ln2_b[l], lambda_init)
    return x
```

```python
import functools
import math

import numpy as np
import jax
import jax.numpy as jnp
from jax import lax
from jax.experimental import pallas as pl
from jax.experimental.pallas import tpu as pltpu

F32 = jnp.float32
BF16 = jnp.bfloat16

D_MODEL = 4096
HEAD_DIM = 128
NSA_HEADS = 16
NSA_GROUPS = 4
NSA_REP = 4
CMP_BLOCK = 32
CMP_STRIDE = 16
CMP_HIDDEN = 256
SLC_BLOCK = 64
SLC_TOPK = 16
WINDOW = 512
DIFF_HEADS = 8
DIFF_DIM = 128
DIFF_VDIM = 256
NSA_WIDTH = NSA_HEADS * HEAD_DIM
KV_WIDTH = 6 * NSA_GROUPS * HEAD_DIM
GATE_COLS = 3 * NSA_HEADS
GATE_START = NSA_WIDTH + KV_WIDTH
DIFF_WIDTH = DIFF_HEADS * DIFF_VDIM
PROJ_A_COLS = GATE_START
PROJ_B_COLS = 3 * DIFF_WIDTH
D_FF = 4 * D_MODEL
LN_EPS = 1e-5
RMS_EPS = 1e-6
NEG = -1e30
DEPTH = 1
DEEPNORM_ALPHA = (2.0 * DEPTH) ** 0.25
ATT_SCALE = HEAD_DIM ** -0.5
LOG2E = 1.4426950408889634
ATT_SCALE2 = ATT_SCALE * LOG2E

LANES = 128
BF16_SUBLANES = 16
VMEM_LIMIT_BYTES = 56 * 1024 * 1024

MM_BM = 1024
MM_BN = 1024
INPROJ_BN = 512
OUTPROJ_BN = 512
GATE_BM = 512
FF2_BK = 2048
CAST_COLS = 512
LN_ROWS = 256
NSA_TQ = 256
SLC_CK = 512
WIN_SPAN = WINDOW + NSA_TQ
DIFF_TQ = 512
DIFF_CK = 512
SLOPE_PIECES = 4


def _params(n_axes):
    return pltpu.CompilerParams(dimension_semantics=("arbitrary",) * n_axes,
                                vmem_limit_bytes=VMEM_LIMIT_BYTES)


def _load_weight_rows(b_ref, tail_ref, shift, cols):
    if shift:
        return jnp.concatenate([b_ref[shift:, cols], tail_ref[0:shift, cols]], axis=0)
    return b_ref[:, cols]


def _inproj_kernel(*refs, shift):
    if shift:
        a_ref, b_ref, tail_ref, o_ref, wb_ref = refs
    else:
        a_ref, b_ref, o_ref, wb_ref = refs
        tail_ref = None

    @pl.when(pl.program_id(1) == 0)
    def _():
        k = wb_ref.shape[0]
        for c0 in range(0, k, CAST_COLS):
            blk = _load_weight_rows(b_ref, tail_ref, shift, slice(c0, c0 + CAST_COLS))
            wb_ref[c0:c0 + CAST_COLS, :] = blk.T.astype(BF16)

    o_ref[...] = jnp.dot(a_ref[...], wb_ref[...], preferred_element_type=F32).astype(o_ref.dtype)


def _inproj(a, wt, out_dtype, *, row0, n, shift=0, bn=INPROJ_BN, bm=MM_BM, name):
    m, k = a.shape
    bn = min(bn, n)
    jb = row0 // bn
    in_specs = [pl.BlockSpec((bm, k), lambda j, i: (i, 0)),
                pl.BlockSpec((bn, k), lambda j, i: (jb + j, 0))]
    args = [a, wt]
    if shift:
        per = bn // LANES
        tb = row0 // LANES
        in_specs.append(pl.BlockSpec((LANES, k), lambda j, i: (tb + per * (j + 1), 0)))
        args.append(wt)
    return pl.pallas_call(
        functools.partial(_inproj_kernel, shift=shift),
        grid=(n // bn, m // bm),
        in_specs=in_specs,
        out_specs=pl.BlockSpec((bm, bn), lambda j, i: (i, j)),
        out_shape=jax.ShapeDtypeStruct((m, n), out_dtype),
        scratch_shapes=[pltpu.VMEM((k, bn), BF16)],
        compiler_params=_params(2),
        name=name,
    )(*args)


def _gate_cast_kernel(x_ref, wt_ref, xb_ref, g_ref, wb_ref):
    @pl.when(pl.program_id(0) == 0)
    def _():
        k = wb_ref.shape[0]
        for c0 in range(0, k, CAST_COLS):
            wb_ref[c0:c0 + CAST_COLS, :] = wt_ref[:, c0:c0 + CAST_COLS].T.astype(BF16)

    xb = x_ref[...].astype(BF16)
    xb_ref[...] = xb
    g_ref[...] = jnp.dot(xb, wb_ref[...], preferred_element_type=F32)


def _gate_cast(x, wt, *, row0, bm=GATE_BM):
    m, k = x.shape
    return pl.pallas_call(
        _gate_cast_kernel,
        grid=(m // bm,),
        in_specs=[pl.BlockSpec((bm, k), lambda i: (i, 0)),
                  pl.BlockSpec((LANES, k), lambda i: (row0 // LANES, 0))],
        out_specs=[pl.BlockSpec((bm, k), lambda i: (i, 0)),
                   pl.BlockSpec((bm, LANES), lambda i: (i, 0))],
        out_shape=[jax.ShapeDtypeStruct((m, k), BF16), jax.ShapeDtypeStruct((m, LANES), F32)],
        scratch_shapes=[pltpu.VMEM((k, LANES), BF16)],
        compiler_params=_params(1),
        name="gate_proj_cast",
    )(x, wt)


def _mm_kernel(a_ref, b_ref, o_ref):
    acc = jnp.dot(a_ref[...], b_ref[...], preferred_element_type=F32)
    o_ref[...] = jnp.square(jnp.maximum(acc, 0.0)).astype(o_ref.dtype)


def _ff1(a, b, *, bm=MM_BM, bn=MM_BN):
    m, k = a.shape
    _, n = b.shape
    return pl.pallas_call(
        _mm_kernel,
        grid=(m // bm, n // bn),
        in_specs=[pl.BlockSpec((bm, k), lambda i, j: (i, 0)),
                  pl.BlockSpec((k, bn), lambda i, j: (0, j))],
        out_specs=pl.BlockSpec((bm, bn), lambda i, j: (i, j)),
        out_shape=jax.ShapeDtypeStruct((m, n), BF16),
        compiler_params=_params(2),
        name="ff1",
    )(a, b)


def _outproj_kernel(a1_ref, a2_ref, b_ref, res_ref, o_ref):
    half = a1_ref.shape[1]
    acc = jnp.dot(a1_ref[...], b_ref[0:half, :], preferred_element_type=F32)
    acc = acc + jnp.dot(a2_ref[...], b_ref[half:2 * half, :], preferred_element_type=F32)
    o_ref[...] = DEEPNORM_ALPHA * res_ref[...] + acc


def _outproj(a1, a2, b, res, *, bm=MM_BM, bn=OUTPROJ_BN):
    m, k1 = a1.shape
    _, n = b.shape
    return pl.pallas_call(
        _outproj_kernel,
        grid=(m // bm, n // bn),
        in_specs=[pl.BlockSpec((bm, k1), lambda i, j: (i, 0)),
                  pl.BlockSpec((bm, k1), lambda i, j: (i, 0)),
                  pl.BlockSpec((2 * k1, bn), lambda i, j: (0, j)),
                  pl.BlockSpec((bm, bn), lambda i, j: (i, j))],
        out_specs=pl.BlockSpec((bm, bn), lambda i, j: (i, j)),
        out_shape=jax.ShapeDtypeStruct((m, n), F32),
        compiler_params=_params(2),
        name="outproj",
    )(a1, a2, b, res)


def _ff2_kernel(a_ref, b_ref, res_ref, o_ref):
    part = jnp.dot(a_ref[...], b_ref[...], preferred_element_type=F32)

    @pl.when(pl.program_id(2) == 0)
    def _():
        o_ref[...] = DEEPNORM_ALPHA * res_ref[...] + part

    @pl.when(pl.program_id(2) > 0)
    def _():
        o_ref[...] += part


def _ff2(a, b, res, *, bm=MM_BM, bn=MM_BN, bk=FF2_BK):
    m, k = a.shape
    _, n = b.shape
    return pl.pallas_call(
        _ff2_kernel,
        grid=(m // bm, n // bn, k // bk),
        in_specs=[pl.BlockSpec((bm, bk), lambda i, j, kk: (i, kk)),
                  pl.BlockSpec((bk, bn), lambda i, j, kk: (kk, j)),
                  pl.BlockSpec((bm, bn), lambda i, j, kk: (i, j))],
        out_specs=pl.BlockSpec((bm, bn), lambda i, j, kk: (i, j)),
        out_shape=jax.ShapeDtypeStruct((m, n), F32),
        compiler_params=_params(3),
        name="ff2",
    )(a, b, res)


def _cast_blocks(src_refs, dst_refs):
    for src, dst in zip(src_refs, dst_refs):
        dst[...] = src[...].astype(dst.dtype)


def _cast_specs(arrays, n_steps, step_of):
    specs, shapes = [], []
    for arr in arrays:
        rows, cols = arr.shape
        assert rows % (n_steps * BF16_SUBLANES) == 0
        blk = (rows // n_steps, cols)
        specs.append(pl.BlockSpec(blk, lambda *ids: (step_of(*ids), 0)))
        shapes.append(jax.ShapeDtypeStruct((rows, cols), BF16))
    return specs, shapes


def _ln_kernel(x_ref, g_ref, b_ref, *o_refs):
    x = x_ref[...]
    mu = jnp.mean(x, axis=-1, keepdims=True)
    xc = x - mu
    var = jnp.mean(xc * xc, axis=-1, keepdims=True)
    y = xc * lax.rsqrt(var + LN_EPS) * g_ref[...] + b_ref[...]
    for o_ref in o_refs:
        o_ref[...] = y.astype(o_ref.dtype)


def _layer_norm(x, g, b, out_dtypes):
    m, d = x.shape
    row_spec = pl.BlockSpec((LN_ROWS, d), lambda i: (i, 0))
    vec_spec = pl.BlockSpec((1, d), lambda i: (0, 0))
    return pl.pallas_call(
        _ln_kernel,
        grid=(m // LN_ROWS,),
        in_specs=[row_spec, vec_spec, vec_spec],
        out_specs=[row_spec for _ in out_dtypes],
        out_shape=[jax.ShapeDtypeStruct((m, d), dt) for dt in out_dtypes],
        compiler_params=_params(1),
        name="layer_norm",
    )(x, g.reshape(1, d), b.reshape(1, d))


def _gelu_tanh(x):
    c = math.sqrt(2.0 / math.pi)
    return 0.5 * x * (1.0 + jnp.tanh(c * (x + 0.044715 * (x * x * x))))


def _compress_kernel(c_ref, pe_ref, w1_ref, w2_ref, o_ref):
    half = CMP_STRIDE * HEAD_DIM
    c = c_ref[0, 0, 0]
    n_chunks = c.shape[0]
    y1 = jnp.dot(c, w1_ref[0, 0:half, :], preferred_element_type=F32)
    y2 = jnp.dot(c, w1_ref[0, half:2 * half, :], preferred_element_type=F32)
    pe8 = jnp.broadcast_to(pe_ref[0], (8, 2 * half))
    pb = jnp.dot(pe8, w1_ref[0], preferred_element_type=F32)[0:1, :]
    h = y1 + pltpu.roll(y2, n_chunks - 1, 0) + pb
    a = _gelu_tanh(h).astype(BF16)
    out = jnp.dot(a, w2_ref[0], preferred_element_type=F32)
    row = lax.broadcasted_iota(jnp.int32, out.shape, 0)
    out = jnp.where(row < n_chunks - 1, out, 0.0)
    o_ref[0, 0, 0] = out.astype(o_ref.dtype)


def _compress(chunks, pe, w1, w2):
    _, b, g, n_chunks, width = chunks.shape
    return pl.pallas_call(
        _compress_kernel,
        grid=(2, b, g),
        in_specs=[pl.BlockSpec((1, 1, 1, n_chunks, width), lambda t, i, j: (t, i, j, 0, 0)),
                  pl.BlockSpec((1, 1, 2 * width), lambda t, i, j: (t, 0, 0)),
                  pl.BlockSpec((1, 2 * width, CMP_HIDDEN), lambda t, i, j: (t, 0, 0)),
                  pl.BlockSpec((1, CMP_HIDDEN, HEAD_DIM), lambda t, i, j: (t, 0, 0))],
        out_specs=pl.BlockSpec((1, 1, 1, n_chunks, HEAD_DIM), lambda t, i, j: (t, i, j, 0, 0)),
        out_shape=jax.ShapeDtypeStruct((2, b, g, n_chunks, HEAD_DIM), BF16),
        compiler_params=_params(3),
        name="compress",
    )(chunks, pe, w1, w2)


def _nt_dot(a, b):
    return lax.dot_general(a, b, (((1,), (1,)), ((), ())), preferred_element_type=F32)


def _dot_row_halves(a, b):
    half = a.shape[0] // 2
    return jnp.concatenate([jnp.dot(a[:half], b, preferred_element_type=F32),
                            jnp.dot(a[half:], b, preferred_element_type=F32)], axis=0)


def _lane_fold(x, op):
    out = x[:, 0:LANES]
    for c0 in range(LANES, x.shape[1], LANES):
        out = op(out, x[:, c0:c0 + LANES])
    return out


def _nsa_kernel(slopes_ref, q_ref, kc_ref, vc_ref, ks_ref, vs_ref, kw_ref, vw_ref,
                gate_ref, gain_ref, ovl_ref, kext_ref, qext_ref, gsel_ref, *rest, n_cast):
    cast_src, o_ref, cast_dst = rest[:n_cast], rest[n_cast], rest[n_cast + 1:2 * n_cast + 1]
    ks_aug, vs_aug, vw_aug, s_ref, oslc_ref = rest[2 * n_cast + 1:]
    _cast_blocks(cast_src, cast_dst)
    g = pl.program_id(1)
    i = pl.program_id(2)
    tq = NSA_TQ
    rows = NSA_REP * tq
    t0 = i * tq

    @pl.when(i == 0)
    def _():
        ks_aug[:, 0:HEAD_DIM] = ks_ref[0]
        ks_aug[:, HEAD_DIM:] = kext_ref[...]
        ones = jnp.ones((ks_aug.shape[0], HEAD_DIM), BF16)
        vs_aug[:, 0:HEAD_DIM] = vs_ref[0]
        vs_aug[:, HEAD_DIM:] = ones
        vw_aug[:, 0:HEAD_DIM] = vw_ref[0]
        vw_aug[:, HEAD_DIM:] = ones

    q = q_ref[0]
    q_heads = [q[:, r * HEAD_DIM:(r + 1) * HEAD_DIM] for r in range(NSA_REP)]
    qs = jnp.concatenate(q_heads, axis=0)
    t_col = t0 + lax.broadcasted_iota(jnp.int32, (tq, 1), 0)
    head = [slice(r * tq, (r + 1) * tq) for r in range(NSA_REP)]

    n_cmp = kc_ref.shape[2]
    s_t = _nt_dot(kc_ref[0, 0], qs) * ATT_SCALE2
    n_col = lax.broadcasted_iota(jnp.int32, (n_cmp, 1), 0)
    t_row = t0 + (lax.broadcasted_iota(jnp.int32, (1, rows), 1) & (tq - 1))
    cmask = (n_col * CMP_STRIDE + (CMP_BLOCK - 1)) <= t_row
    s_t = jnp.where(cmask, s_t, NEG)
    m = jnp.max(s_t, axis=0, keepdims=True)
    e = jnp.where(cmask, jnp.exp2(s_t - m), 0.0)
    l = jnp.sum(e, axis=0, keepdims=True)
    p_t = e * (1.0 / jnp.maximum(l, 1e-30))
    p_sum_t = p_t[:, head[0]]
    for r in range(1, NSA_REP):
        p_sum_t = p_sum_t + p_t[:, head[r]]
    o_cmp = lax.dot_general(p_t.astype(BF16), vc_ref[0, 0], (((0,), (0,)), ((), ())),
                            preferred_element_type=F32)

    n_slc = ovl_ref.shape[0]
    imp_t = jnp.dot(ovl_ref[...], p_sum_t, precision=lax.Precision.HIGHEST,
                    preferred_element_type=F32)
    j_t = lax.broadcasted_iota(jnp.int32, (n_slc, tq), 0)
    t_t = t0 + lax.broadcasted_iota(jnp.int32, (n_slc, tq), 1)
    causal = j_t * SLC_BLOCK <= t_t
    tb = lax.shift_right_logical(t_t, int(math.log2(SLC_BLOCK)))
    forced = jnp.where(j_t == 0, 1e6, jnp.where(j_t == tb, 1e6, jnp.where(j_t == tb - 1, 1e6, imp_t)))
    score = jnp.where(causal, forced, NEG)
    rank = jnp.zeros((n_slc, tq), F32)
    for jp in range(n_slc):
        row = score[jp:jp + 1, :]
        ge = jnp.where(row >= score, 1.0, 0.0)
        gt = jnp.where(row > score, 1.0, 0.0)
        rank = rank + jnp.where(j_t > jp, ge, gt)
    sel_bias = jnp.where(causal, jnp.where(rank < SLC_TOPK, 0.0, NEG), NEG)

    sel_rows = jnp.concatenate([sel_bias, jnp.zeros((LANES - n_slc, tq), F32)], axis=0).T
    q_aug = jnp.concatenate(
        [jnp.concatenate([q_heads[r], (sel_rows + qext_ref[0, r:r + 1, :]).astype(BF16)], axis=1)
         for r in range(NSA_REP)], axis=0)
    ck = SLC_CK
    n_chunks = (t0 + tq + ck - 1) // ck

    def slc_branch(n):
        mx = jnp.full((rows, LANES), NEG, F32)
        for c in range(n):
            cols = slice(c * ck, (c + 1) * ck)
            t = _nt_dot(q_aug, ks_aug[cols, :]) * ATT_SCALE2
            if c == n - 1:
                pos = c * ck + lax.broadcasted_iota(jnp.int32, (1, ck), 1)
                cb = jnp.where(pos <= t_col, 0.0, NEG)
                t = t + jnp.concatenate([cb] * NSA_REP, axis=0)
            s_ref[:, cols] = t
            mx = jnp.maximum(mx, _lane_fold(t, jnp.maximum))
        m_slc = jnp.max(mx, axis=-1, keepdims=True)
        acc = jnp.zeros((rows, 2 * HEAD_DIM), F32)
        for c in range(n):
            cols = slice(c * ck, (c + 1) * ck)
            p = jnp.exp2(s_ref[:, cols] - m_slc).astype(BF16)
            acc = acc + _dot_row_halves(p, vs_aug[cols, :])
        oslc_ref[...] = acc[:, 0:HEAD_DIM] * (1.0 / acc[:, HEAD_DIM:])

    for n in range(1, ks_aug.shape[0] // ck + 1):
        pl.when(n_chunks == n)(functools.partial(slc_branch, n))
    o_slc = oslc_ref[...]

    slope2 = [slopes_ref[g * NSA_REP + r] * LOG2E for r in range(NSA_REP)]
    k0 = pl.multiple_of(jnp.maximum(t0 - WINDOW, 0), tq)
    pos = k0 + lax.broadcasted_iota(jnp.int32, (1, WIN_SPAN), 1)
    dist = t_col - pos
    wbias = jnp.where(dist >= 0, jnp.where(dist < WINDOW, 0.0, NEG), NEG)
    posrel = (pos - t0).astype(F32)
    s_win = _nt_dot(qs, kw_ref[0, pl.ds(k0, WIN_SPAN), :])
    ps = []
    for r in range(NSA_REP):
        t = s_win[head[r]] * ATT_SCALE2 + slope2[r] * posrel + wbias
        m = jnp.max(t, axis=-1, keepdims=True)
        ps.append(jnp.exp2(t - m).astype(BF16))
    acc_w = _dot_row_halves(jnp.concatenate(ps, axis=0), vw_aug[pl.ds(k0, WIN_SPAN), :])
    o_win = acc_w[:, 0:HEAD_DIM] * (1.0 / acc_w[:, HEAD_DIM:])

    sig = jax.nn.sigmoid(gate_ref[0]).astype(BF16)
    gates = jnp.dot(sig, gsel_ref[0], preferred_element_type=F32)
    for r in range(NSA_REP):
        gt = [gates[:, (3 * r + br) * LANES:(3 * r + br + 1) * LANES] for br in range(3)]
        o = gt[0] * o_cmp[head[r]] + gt[1] * o_slc[head[r]] + gt[2] * o_win[head[r]]
        o = o * lax.rsqrt(jnp.mean(o * o, axis=-1, keepdims=True) + RMS_EPS) * gain_ref[0, r:r + 1, :]
        o_ref[0, :, r * HEAD_DIM:(r + 1) * HEAD_DIM] = o.astype(o_ref.dtype)


def _nsa_attention(proj_a, cmp_kv, gates3, gain, slopes, ovl_t, kext, qext, gsel, cast_arrays):
    b, s, _ = proj_a.shape
    n_chunks = cmp_kv.shape[3]
    n_q = s // NSA_TQ
    cast_specs, cast_shapes = _cast_specs(cast_arrays, b * NSA_GROUPS * n_q,
                                          lambda bi, g, i: (bi * NSA_GROUPS + g) * n_q + i)
    col0 = NSA_WIDTH // HEAD_DIM

    def kv_spec(idx):
        return pl.BlockSpec((1, s, HEAD_DIM), lambda bi, g, i, idx=idx: (bi, 0, col0 + idx * NSA_GROUPS + g))

    cmp_spec = pl.BlockSpec((1, 1, n_chunks, HEAD_DIM), lambda bi, g, i: (bi, g, 0, 0))
    in_specs = [
        pl.BlockSpec(memory_space=pltpu.SMEM),
        pl.BlockSpec((1, NSA_TQ, NSA_REP * HEAD_DIM), lambda bi, g, i: (bi, i, g)),
        cmp_spec, cmp_spec,
        kv_spec(2), kv_spec(3), kv_spec(4), kv_spec(5),
        pl.BlockSpec((1, NSA_TQ, LANES), lambda bi, g, i: (bi, i, 0)),
        pl.BlockSpec((1, NSA_REP, HEAD_DIM), lambda bi, g, i: (g, 0, 0)),
        pl.BlockSpec(ovl_t.shape, lambda bi, g, i: (0, 0)),
        pl.BlockSpec(kext.shape, lambda bi, g, i: (0, 0)),
        pl.BlockSpec((1, NSA_REP, LANES), lambda bi, g, i: (g, 0, 0)),
        pl.BlockSpec((1,) + gsel.shape[1:], lambda bi, g, i: (g, 0, 0)),
    ] + cast_specs
    aug = pltpu.VMEM((s, 2 * HEAD_DIM), BF16)
    return pl.pallas_call(
        functools.partial(_nsa_kernel, n_cast=len(cast_arrays)),
        grid=(b, NSA_GROUPS, n_q),
        in_specs=in_specs,
        out_specs=[pl.BlockSpec((1, NSA_TQ, NSA_REP * HEAD_DIM), lambda bi, g, i: (bi, i, g))] + cast_specs,
        out_shape=[jax.ShapeDtypeStruct((b, s, NSA_WIDTH), BF16)] + cast_shapes,
        scratch_shapes=[aug, aug, aug, pltpu.VMEM((NSA_REP * NSA_TQ, s), F32),
                        pltpu.VMEM((NSA_REP * NSA_TQ, HEAD_DIM), F32)],
        compiler_params=_params(3),
        name="nsa_attention",
    )(slopes, proj_a, cmp_kv[0], cmp_kv[1], proj_a, proj_a, proj_a, proj_a, gates3, gain, ovl_t,
      kext, qext, gsel, *cast_arrays)


def _diff_kernel(lq1_ref, lk1_ref, lq2_ref, lk2_ref, q_ref, k_ref, v_ref, g_ref, kext_ref, qext_ref,
                 *rest, n_cast, lambda_init):
    cast_src, o_ref, cast_dst = rest[:n_cast], rest[n_cast], rest[n_cast + 1:2 * n_cast + 1]
    k_aug, s_ref, acc_ref = rest[2 * n_cast + 1:]
    _cast_blocks(cast_src, cast_dst)
    i = pl.program_id(2)
    tq, ck = DIFF_TQ, DIFF_CK
    t0 = i * tq

    @pl.when(i == 0)
    def _():
        for mi in range(2):
            k_aug[mi, :, 0:DIFF_DIM] = k_ref[0, :, mi * DIFF_DIM:(mi + 1) * DIFF_DIM]
            k_aug[mi, :, DIFF_DIM:] = kext_ref[...]

    q = q_ref[0]
    qext = jnp.broadcast_to(qext_ref[0], (tq, LANES)).astype(BF16)
    q_aug = [jnp.concatenate([q[:, mi * DIFF_DIM:(mi + 1) * DIFF_DIM], qext], axis=1) for mi in range(2)]
    t_col = t0 + lax.broadcasted_iota(jnp.int32, (tq, 1), 0)

    def branch(n):
        for mi in range(2):
            mx = jnp.full((tq, LANES), NEG, F32)
            for c in range(n):
                cols = slice(c * ck, (c + 1) * ck)
                t = _nt_dot(q_aug[mi], k_aug[mi, cols, :]) * ATT_SCALE2
                if c == n - 1:
                    pos = c * ck + lax.broadcasted_iota(jnp.int32, (1, ck), 1)
                    t = jnp.where(pos <= t_col, t, NEG)
                s_ref[mi, :, cols] = t
                mx = jnp.maximum(mx, _lane_fold(t, jnp.maximum))
            m_row = jnp.max(mx, axis=-1, keepdims=True)
            l_fold = jnp.zeros((tq, LANES), F32)
            acc = jnp.zeros((tq, DIFF_VDIM), F32)
            for c in range(n):
                cols = slice(c * ck, (c + 1) * ck)
                p = jnp.exp2(s_ref[mi, :, cols] - m_row)
                l_fold = l_fold + _lane_fold(p, jnp.add)
                acc = acc + _dot_row_halves(p.astype(BF16), v_ref[0, cols, :])
            acc_ref[mi] = acc * (1.0 / jnp.sum(l_fold, axis=-1, keepdims=True))

    for n in range(1, k_aug.shape[1] // ck + 1):
        pl.when(i + 1 == n)(functools.partial(branch, n))

    lam = (jnp.exp(jnp.sum(lq1_ref[...] * lk1_ref[...], axis=-1, keepdims=True))
           - jnp.exp(jnp.sum(lq2_ref[...] * lk2_ref[...], axis=-1, keepdims=True))
           + lambda_init)
    o = acc_ref[0] - lam * acc_ref[1]
    o = o * lax.rsqrt(jnp.mean(o * o, axis=-1, keepdims=True) + RMS_EPS) * g_ref[...]
    o_ref[0] = (o * (1.0 - lambda_init)).astype(o_ref.dtype)


def _diff_attention(proj_b, lq1, lk1, lq2, lk2, gain, kext, qext, lambda_init, cast_arrays):
    b, s, _ = proj_b.shape
    n_q = s // DIFF_TQ
    cast_specs, cast_shapes = _cast_specs(cast_arrays, b * DIFF_HEADS * n_q,
                                          lambda bi, h, i: (bi * DIFF_HEADS + h) * n_q + i)
    qcol, kcol, vcol = 0, DIFF_HEADS, 2 * DIFF_HEADS
    vec = pl.BlockSpec((1, DIFF_DIM), lambda bi, h, i: (0, 0))
    in_specs = [
        vec, vec, vec, vec,
        pl.BlockSpec((1, DIFF_TQ, DIFF_VDIM), lambda bi, h, i: (bi, i, qcol + h)),
        pl.BlockSpec((1, s, DIFF_VDIM), lambda bi, h, i: (bi, 0, kcol + h)),
        pl.BlockSpec((1, s, DIFF_VDIM), lambda bi, h, i: (bi, 0, vcol + h)),
        pl.BlockSpec((1, DIFF_VDIM), lambda bi, h, i: (0, 0)),
        pl.BlockSpec(kext.shape, lambda bi, h, i: (0, 0)),
        pl.BlockSpec((1, 1, LANES), lambda bi, h, i: (h, 0, 0)),
    ] + cast_specs
    return pl.pallas_call(
        functools.partial(_diff_kernel, n_cast=len(cast_arrays), lambda_init=lambda_init),
        grid=(b, DIFF_HEADS, n_q),
        in_specs=in_specs,
        out_specs=[pl.BlockSpec((1, DIFF_TQ, DIFF_VDIM), lambda bi, h, i: (bi, i, h))] + cast_specs,
        out_shape=[jax.ShapeDtypeStruct((b, s, DIFF_WIDTH), BF16)] + cast_shapes,
        scratch_shapes=[pltpu.VMEM((2, s, 2 * DIFF_DIM), BF16), pltpu.VMEM((2, DIFF_TQ, s), F32),
                        pltpu.VMEM((2, DIFF_TQ, DIFF_VDIM), F32)],
        compiler_params=_params(3),
        name="diff_attention",
    )(lq1.reshape(1, -1), lk1.reshape(1, -1), lq2.reshape(1, -1), lk2.reshape(1, -1),
      proj_b, proj_b, proj_b, gain.reshape(1, -1), kext, qext, *cast_arrays)


def _alibi_slopes(n_heads):
    return np.array([2.0 ** (-8.0 * (h + 1) / n_heads) for h in range(n_heads)], np.float32)


def _overlap_t(seq):
    n_cmp = seq // CMP_STRIDE
    n_slc = seq // SLC_BLOCK
    cs = np.arange(n_cmp) * CMP_STRIDE
    ss = np.arange(n_slc) * SLC_BLOCK
    ov = np.clip(np.minimum(cs[:, None] + CMP_BLOCK, ss[None, :] + SLC_BLOCK)
                 - np.maximum(cs[:, None], ss[None, :]), 0, None)
    ov[(seq - CMP_BLOCK) // CMP_STRIDE + 1:, :] = 0
    return np.ascontiguousarray((ov / CMP_BLOCK).astype(np.float32).T)


def _round_to_bf16(x):
    bits = np.array(x, np.float32).view(np.uint32)
    bits = (bits + np.uint32(0x7FFF) + ((bits >> np.uint32(16)) & np.uint32(1))) & np.uint32(0xFFFF0000)
    return np.float64(bits.view(np.float32))


def _bf16_pieces(x, n):
    out, rem = [], np.float64(x)
    for _ in range(n):
        piece = _round_to_bf16(rem)
        out.append(piece)
        rem = rem - piece
    return out


def _key_ext(seq):
    ext = np.zeros((seq, LANES), np.float32)
    pos = np.arange(seq)
    ext[pos, pos // SLC_BLOCK] = 1.0
    centred = pos - seq // 2
    hi = (centred // 256) * 256
    lo = centred - hi
    n_slc = seq // SLC_BLOCK
    ext[:, n_slc:n_slc + SLOPE_PIECES] = hi[:, None]
    ext[:, n_slc + SLOPE_PIECES:n_slc + 2 * SLOPE_PIECES] = lo[:, None]
    return ext


def _query_ext(slopes, n_slc):
    ext = np.zeros((len(slopes), LANES), np.float32)
    for h, slope in enumerate(slopes):
        pieces = _bf16_pieces(np.float64(slope) / ATT_SCALE, SLOPE_PIECES)
        ext[h, n_slc:n_slc + SLOPE_PIECES] = pieces
        ext[h, n_slc + SLOPE_PIECES:n_slc + 2 * SLOPE_PIECES] = pieces
    return ext


def _gate_select():
    per = 3 * NSA_REP
    sel = np.zeros((NSA_GROUPS, LANES, per * LANES), np.float32)
    for g in range(NSA_GROUPS):
        for k in range(per):
            sel[g, g * per + k, k * LANES:(k + 1) * LANES] = 1.0
    return sel


def _layer(x, w_in, cmp_pe_k, cmp_w1_k, cmp_w2_k, cmp_pe_v, cmp_w1_v, cmp_w2_v, nsa_out_g,
           lambda_q1, lambda_k1, lambda_q2, lambda_k2, diff_subln_g, w_out, ln1_g, ln1_b,
           w_ff1, w_ff2, ln2_g, ln2_b, lambda_init):
    b, s, d = x.shape
    t = b * s
    x2 = x.reshape(t, d)

    w_t = jnp.swapaxes(w_in, 0, 1)
    x_bf, gates = _gate_cast(x2, w_t, row0=GATE_START)
    proj_a = _inproj(x_bf, w_t, BF16, row0=0, n=PROJ_A_COLS, name="in_proj_nsa")
    proj_b = _inproj(x_bf, w_t, BF16, row0=GATE_START, n=PROJ_B_COLS, shift=GATE_COLS,
                     name="in_proj_diff")
    proj_a = proj_a.reshape(b, s, PROJ_A_COLS)
    proj_b = proj_b.reshape(b, s, PROJ_B_COLS)

    n_chunks = s // CMP_STRIDE
    kv_cmp = proj_a[:, :, NSA_WIDTH:NSA_WIDTH + 2 * NSA_GROUPS * HEAD_DIM]
    kv_cmp = kv_cmp.reshape(b, n_chunks, CMP_STRIDE, 2, NSA_GROUPS, HEAD_DIM)
    chunks = kv_cmp.transpose(3, 0, 4, 1, 2, 5).reshape(2, b, NSA_GROUPS, n_chunks,
                                                        CMP_STRIDE * HEAD_DIM)
    pe = jnp.stack([cmp_pe_k, cmp_pe_v]).reshape(2, 1, CMP_BLOCK * HEAD_DIM).astype(BF16)
    w1 = jnp.stack([cmp_w1_k, cmp_w1_v]).astype(BF16)
    w2 = jnp.stack([cmp_w2_k, cmp_w2_v]).astype(BF16)
    cmp_kv = _compress(chunks, pe, w1, w2)

    n_slc = s // SLC_BLOCK
    kext = jnp.asarray(_key_ext(s)).astype(BF16)
    o_nsa, w_ff1_bf, w_out_bf = _nsa_attention(
        proj_a, cmp_kv, gates.reshape(b, s, LANES),
        nsa_out_g.reshape(NSA_GROUPS, NSA_REP, HEAD_DIM),
        jnp.asarray(_alibi_slopes(NSA_HEADS)), jnp.asarray(_overlap_t(s)), kext,
        jnp.asarray(_query_ext(_alibi_slopes(NSA_HEADS), n_slc)).reshape(NSA_GROUPS, NSA_REP, LANES),
        jnp.asarray(_gate_select()).astype(BF16), [w_ff1, w_out])
    o_diff, w_ff2_bf = _diff_attention(
        proj_b, lambda_q1, lambda_k1, lambda_q2, lambda_k2, diff_subln_g, kext,
        jnp.asarray(_query_ext(_alibi_slopes(DIFF_HEADS), n_slc)).reshape(DIFF_HEADS, 1, LANES),
        lambda_init, [w_ff2])

    h1 = _outproj(o_nsa.reshape(t, NSA_WIDTH), o_diff.reshape(t, DIFF_WIDTH), w_out_bf, x2)
    x1, x1_bf = _layer_norm(h1, ln1_g, ln1_b, (F32, BF16))
    hid = _ff1(x1_bf, w_ff1_bf)
    h2 = _ff2(hid, w_ff2_bf, x1)
    (out,) = _layer_norm(h2, ln2_g, ln2_b, (F32,))
    return out.reshape(b, s, d)


def kernel(x, w_in, cmp_pe_k, cmp_w1_k, cmp_w2_k, cmp_pe_v, cmp_w1_v, cmp_w2_v, nsa_out_g,
           lambda_q1, lambda_k1, lambda_q2, lambda_k2, diff_subln_g, w_out, ln1_g, ln1_b,
           w_ff1, w_ff2, ln2_g, ln2_b):
    for l in range(DEPTH):
        lambda_init = 0.8 - 0.6 * math.exp(-0.3 * l)
        x = _layer(x, w_in[l], cmp_pe_k[l], cmp_w1_k[l], cmp_w2_k[l], cmp_pe_v[l], cmp_w1_v[l],
                   cmp_w2_v[l], nsa_out_g[l], lambda_q1[l], lambda_k1[l], lambda_q2[l],
                   lambda_k2[l], diff_subln_g[l], w_out[l], ln1_g[l], ln1_b[l], w_ff1[l],
                   w_ff2[l], ln2_g[l], ln2_b[l], lambda_init)
    return x
```

```python
import functools
import math

import numpy as np
import jax
import jax.numpy as jnp
from jax import lax
from jax.experimental import pallas as pl
from jax.experimental.pallas import tpu as pltpu

F32 = jnp.float32
BF16 = jnp.bfloat16

D_MODEL = 4096
HEAD_DIM = 128
NSA_HEADS = 16
NSA_GROUPS = 4
NSA_REP = 4
CMP_BLOCK = 32
CMP_STRIDE = 16
CMP_HIDDEN = 256
SLC_BLOCK = 64
SLC_TOPK = 16
WINDOW = 512
DIFF_HEADS = 8
DIFF_DIM = 128
DIFF_VDIM = 256
NSA_WIDTH = NSA_HEADS * HEAD_DIM
KV_WIDTH = 6 * NSA_GROUPS * HEAD_DIM
GATE_COLS = 3 * NSA_HEADS
GATE_START = NSA_WIDTH + KV_WIDTH
DIFF_WIDTH = DIFF_HEADS * DIFF_VDIM
PROJ_A_COLS = GATE_START
PROJ_B_COLS = 3 * DIFF_WIDTH
D_FF = 4 * D_MODEL
LN_EPS = 1e-5
RMS_EPS = 1e-6
NEG = -1e30
DEPTH = 1
DEEPNORM_ALPHA = (2.0 * DEPTH) ** 0.25
ATT_SCALE = HEAD_DIM ** -0.5
LOG2E = 1.4426950408889634
ATT_SCALE2 = ATT_SCALE * LOG2E

LANES = 128
BF16_SUBLANES = 16
VMEM_LIMIT_BYTES = 56 * 1024 * 1024

MM_BM = 1024
MM_BN = 1024
INPROJ_BN = 512
OUTPROJ_BN = 512
GATE_BM = 512
FF2_BK = 2048
CAST_COLS = 512
LN_ROWS = 256
NSA_TQ = 256
SLC_CK = 512
WIN_SPAN = WINDOW + NSA_TQ
DIFF_TQ = 512
DIFF_CK = 512
SLOPE_PIECES = 4


def _params(n_axes):
    return pltpu.CompilerParams(dimension_semantics=("arbitrary",) * n_axes,
                                vmem_limit_bytes=VMEM_LIMIT_BYTES)


def _load_weight_rows(b_ref, tail_ref, shift, cols):
    if shift:
        return jnp.concatenate([b_ref[shift:, cols], tail_ref[0:shift, cols]], axis=0)
    return b_ref[:, cols]


def _inproj_kernel(*refs, shift, n_inner):
    if shift:
        a_ref, b_ref, tail_ref, o_ref, wb_ref = refs
    else:
        a_ref, b_ref, o_ref, wb_ref = refs
        tail_ref = None
    j = pl.program_id(0)
    i = pl.program_id(1)

    def prepare_slice():
        width = wb_ref.shape[1] // n_inner
        cols = pl.ds(pl.multiple_of(i * width, width), width)
        blk = _load_weight_rows(b_ref, tail_ref, shift, cols)
        wb_ref[j % 2, cols, :] = blk.T.astype(BF16)

    pl.when(j == 0)(prepare_slice)

    @pl.when(j > 0)
    def _():
        prepare_slice()
        o_ref[...] = jnp.dot(a_ref[...], wb_ref[(j + 1) % 2],
                             preferred_element_type=F32).astype(o_ref.dtype)


def _inproj(a, wt, out_dtype, *, row0, n, shift=0, bn=INPROJ_BN, bm=MM_BM, name):
    m, k = a.shape
    n_tiles = n // bn
    assert k % ((m // bm) * LANES) == 0
    jb = row0 // bn

    def tile(j):
        return jnp.minimum(j, n_tiles - 1)

    def row_tile(j, i):
        return jnp.where(j == 0, 0, i)

    in_specs = [pl.BlockSpec((bm, k), lambda j, i: (row_tile(j, i), 0)),
                pl.BlockSpec((bn, k), lambda j, i: (jb + tile(j), 0))]
    args = [a, wt]
    if shift:
        per = bn // LANES
        tb = row0 // LANES
        in_specs.append(pl.BlockSpec((LANES, k), lambda j, i: (tb + per * (tile(j) + 1), 0)))
        args.append(wt)
    return pl.pallas_call(
        functools.partial(_inproj_kernel, shift=shift, n_inner=m // bm),
        grid=(n_tiles + 1, m // bm),
        in_specs=in_specs,
        out_specs=pl.BlockSpec((bm, bn), lambda j, i: (row_tile(j, i), jnp.maximum(j - 1, 0))),
        out_shape=jax.ShapeDtypeStruct((m, n), out_dtype),
        scratch_shapes=[pltpu.VMEM((2, k, bn), BF16)],
        compiler_params=_params(2),
        name=name,
    )(*args)


def _gate_cast_kernel(x_ref, wt_ref, xb_ref, g_ref, wb_ref):
    @pl.when(pl.program_id(0) == 0)
    def _():
        k = wb_ref.shape[0]
        for c0 in range(0, k, CAST_COLS):
            wb_ref[c0:c0 + CAST_COLS, :] = wt_ref[:, c0:c0 + CAST_COLS].T.astype(BF16)

    xb = x_ref[...].astype(BF16)
    xb_ref[...] = xb
    g_ref[...] = jnp.dot(xb, wb_ref[...], preferred_element_type=F32)


def _gate_cast(x, wt, *, row0, bm=GATE_BM):
    m, k = x.shape
    return pl.pallas_call(
        _gate_cast_kernel,
        grid=(m // bm,),
        in_specs=[pl.BlockSpec((bm, k), lambda i: (i, 0)),
                  pl.BlockSpec((LANES, k), lambda i: (row0 // LANES, 0))],
        out_specs=[pl.BlockSpec((bm, k), lambda i: (i, 0)),
                   pl.BlockSpec((bm, LANES), lambda i: (i, 0))],
        out_shape=[jax.ShapeDtypeStruct((m, k), BF16), jax.ShapeDtypeStruct((m, LANES), F32)],
        scratch_shapes=[pltpu.VMEM((k, LANES), BF16)],
        compiler_params=_params(1),
        name="gate_proj_cast",
    )(x, wt)


def _mm_kernel(a_ref, b_ref, o_ref):
    acc = jnp.dot(a_ref[...], b_ref[...], preferred_element_type=F32)
    o_ref[...] = jnp.square(jnp.maximum(acc, 0.0)).astype(o_ref.dtype)


def _ff1(a, b, *, bm=MM_BM, bn=MM_BN):
    m, k = a.shape
    _, n = b.shape
    return pl.pallas_call(
        _mm_kernel,
        grid=(m // bm, n // bn),
        in_specs=[pl.BlockSpec((bm, k), lambda i, j: (i, 0)),
                  pl.BlockSpec((k, bn), lambda i, j: (0, j))],
        out_specs=pl.BlockSpec((bm, bn), lambda i, j: (i, j)),
        out_shape=jax.ShapeDtypeStruct((m, n), BF16),
        compiler_params=_params(2),
        name="ff1",
    )(a, b)


def _outproj_kernel(a1_ref, a2_ref, b_ref, res_ref, o_ref):
    half = a1_ref.shape[1]
    acc = jnp.dot(a1_ref[...], b_ref[0:half, :], preferred_element_type=F32)
    acc = acc + jnp.dot(a2_ref[...], b_ref[half:2 * half, :], preferred_element_type=F32)
    o_ref[...] = DEEPNORM_ALPHA * res_ref[...] + acc


def _outproj(a1, a2, b, res, *, bm=MM_BM, bn=OUTPROJ_BN):
    m, k1 = a1.shape
    _, n = b.shape
    return pl.pallas_call(
        _outproj_kernel,
        grid=(m // bm, n // bn),
        in_specs=[pl.BlockSpec((bm, k1), lambda i, j: (i, 0)),
                  pl.BlockSpec((bm, k1), lambda i, j: (i, 0)),
                  pl.BlockSpec((2 * k1, bn), lambda i, j: (0, j)),
                  pl.BlockSpec((bm, bn), lambda i, j: (i, j))],
        out_specs=pl.BlockSpec((bm, bn), lambda i, j: (i, j)),
        out_shape=jax.ShapeDtypeStruct((m, n), F32),
        compiler_params=_params(2),
        name="outproj",
    )(a1, a2, b, res)


def _ff2_kernel(a_ref, b_ref, res_ref, o_ref, acc_ref):
    kk = pl.program_id(2)

    @pl.when(kk == 0)
    def _():
        acc_ref[...] = jnp.zeros_like(acc_ref)

    acc_ref[...] += jnp.dot(a_ref[...], b_ref[...], preferred_element_type=F32)

    @pl.when(kk == pl.num_programs(2) - 1)
    def _():
        o_ref[...] = DEEPNORM_ALPHA * res_ref[...] + acc_ref[...]


def _ff2(a, b, res, *, bm=MM_BM, bn=MM_BN, bk=FF2_BK):
    m, k = a.shape
    _, n = b.shape
    return pl.pallas_call(
        _ff2_kernel,
        grid=(m // bm, n // bn, k // bk),
        in_specs=[pl.BlockSpec((bm, bk), lambda i, j, kk: (i, kk)),
                  pl.BlockSpec((bk, bn), lambda i, j, kk: (kk, j)),
                  pl.BlockSpec((bm, bn), lambda i, j, kk: (i, j))],
        out_specs=pl.BlockSpec((bm, bn), lambda i, j, kk: (i, j)),
        out_shape=jax.ShapeDtypeStruct((m, n), F32),
        scratch_shapes=[pltpu.VMEM((bm, bn), F32)],
        compiler_params=_params(3),
        name="ff2",
    )(a, b, res)


def _cast_blocks(src_refs, dst_refs):
    for src, dst in zip(src_refs, dst_refs):
        dst[...] = src[...].astype(dst.dtype)


def _cast_specs(arrays, n_steps, step_of):
    specs, shapes = [], []
    for arr in arrays:
        rows, cols = arr.shape
        assert rows % (n_steps * BF16_SUBLANES) == 0
        blk = (rows // n_steps, cols)
        specs.append(pl.BlockSpec(blk, lambda *ids: (step_of(*ids), 0)))
        shapes.append(jax.ShapeDtypeStruct((rows, cols), BF16))
    return specs, shapes


def _ln_kernel(x_ref, g_ref, b_ref, *o_refs):
    x = x_ref[...]
    mu = jnp.mean(x, axis=-1, keepdims=True)
    xc = x - mu
    var = jnp.mean(xc * xc, axis=-1, keepdims=True)
    y = xc * lax.rsqrt(var + LN_EPS) * g_ref[...] + b_ref[...]
    for o_ref in o_refs:
        o_ref[...] = y.astype(o_ref.dtype)


def _layer_norm(x, g, b, out_dtypes):
    m, d = x.shape
    row_spec = pl.BlockSpec((LN_ROWS, d), lambda i: (i, 0))
    vec_spec = pl.BlockSpec((1, d), lambda i: (0, 0))
    return pl.pallas_call(
        _ln_kernel,
        grid=(m // LN_ROWS,),
        in_specs=[row_spec, vec_spec, vec_spec],
        out_specs=[row_spec for _ in out_dtypes],
        out_shape=[jax.ShapeDtypeStruct((m, d), dt) for dt in out_dtypes],
        compiler_params=_params(1),
        name="layer_norm",
    )(x, g.reshape(1, d), b.reshape(1, d))


def _gelu_tanh(x):
    c = math.sqrt(2.0 / math.pi)
    return 0.5 * x * (1.0 + jnp.tanh(c * (x + 0.044715 * (x * x * x))))


def _compress_kernel(c_ref, pe_ref, w1_ref, w2_ref, o_ref):
    half = CMP_STRIDE * HEAD_DIM
    c = c_ref[0, 0, 0]
    n_chunks = c.shape[0]
    y1 = jnp.dot(c, w1_ref[0, 0:half, :], preferred_element_type=F32)
    y2 = jnp.dot(c, w1_ref[0, half:2 * half, :], preferred_element_type=F32)
    pe8 = jnp.broadcast_to(pe_ref[0], (8, 2 * half))
    pb = jnp.dot(pe8, w1_ref[0], preferred_element_type=F32)[0:1, :]
    h = y1 + pltpu.roll(y2, n_chunks - 1, 0) + pb
    a = _gelu_tanh(h).astype(BF16)
    out = jnp.dot(a, w2_ref[0], preferred_element_type=F32)
    row = lax.broadcasted_iota(jnp.int32, out.shape, 0)
    out = jnp.where(row < n_chunks - 1, out, 0.0)
    o_ref[0, 0, 0] = out.astype(o_ref.dtype)


def _compress(chunks, pe, w1, w2):
    _, b, g, n_chunks, width = chunks.shape
    return pl.pallas_call(
        _compress_kernel,
        grid=(2, b, g),
        in_specs=[pl.BlockSpec((1, 1, 1, n_chunks, width), lambda t, i, j: (t, i, j, 0, 0)),
                  pl.BlockSpec((1, 1, 2 * width), lambda t, i, j: (t, 0, 0)),
                  pl.BlockSpec((1, 2 * width, CMP_HIDDEN), lambda t, i, j: (t, 0, 0)),
                  pl.BlockSpec((1, CMP_HIDDEN, HEAD_DIM), lambda t, i, j: (t, 0, 0))],
        out_specs=pl.BlockSpec((1, 1, 1, n_chunks, HEAD_DIM), lambda t, i, j: (t, i, j, 0, 0)),
        out_shape=jax.ShapeDtypeStruct((2, b, g, n_chunks, HEAD_DIM), BF16),
        compiler_params=_params(3),
        name="compress",
    )(chunks, pe, w1, w2)


def _nt_dot(a, b):
    return lax.dot_general(a, b, (((1,), (1,)), ((), ())), preferred_element_type=F32)


def _dot_row_halves(a, b):
    half = a.shape[0] // 2
    return jnp.concatenate([jnp.dot(a[:half], b, preferred_element_type=F32),
                            jnp.dot(a[half:], b, preferred_element_type=F32)], axis=0)


def _lane_fold(x, op):
    out = x[:, 0:LANES]
    for c0 in range(LANES, x.shape[1], LANES):
        out = op(out, x[:, c0:c0 + LANES])
    return out


def _nsa_kernel(slopes_ref, q_ref, kc_ref, vc_ref, ks_ref, vs_ref, kw_ref, vw_ref,
                gate_ref, gain_ref, ovl_ref, kext_ref, qext_ref, gsel_ref, *rest, n_cast):
    cast_src, o_ref, cast_dst = rest[:n_cast], rest[n_cast], rest[n_cast + 1:2 * n_cast + 1]
    ks_aug, vs_aug, vw_aug, s_ref, oslc_ref = rest[2 * n_cast + 1:]
    _cast_blocks(cast_src, cast_dst)
    g = pl.program_id(1)
    i = pl.program_id(2)
    tq = NSA_TQ
    rows = NSA_REP * tq
    t0 = i * tq

    @pl.when(i == 0)
    def _():
        ks_aug[:, 0:HEAD_DIM] = ks_ref[0]
        ks_aug[:, HEAD_DIM:] = kext_ref[...]
        ones = jnp.ones((ks_aug.shape[0], HEAD_DIM), BF16)
        vs_aug[:, 0:HEAD_DIM] = vs_ref[0]
        vs_aug[:, HEAD_DIM:] = ones
        vw_aug[:, 0:HEAD_DIM] = vw_ref[0]
        vw_aug[:, HEAD_DIM:] = ones

    q = q_ref[0]
    q_heads = [q[:, r * HEAD_DIM:(r + 1) * HEAD_DIM] for r in range(NSA_REP)]
    qs = jnp.concatenate(q_heads, axis=0)
    t_col = t0 + lax.broadcasted_iota(jnp.int32, (tq, 1), 0)
    head = [slice(r * tq, (r + 1) * tq) for r in range(NSA_REP)]

    n_cmp = kc_ref.shape[2]
    s_t = _nt_dot(kc_ref[0, 0], qs) * ATT_SCALE2
    n_col = lax.broadcasted_iota(jnp.int32, (n_cmp, 1), 0)
    t_row = t0 + (lax.broadcasted_iota(jnp.int32, (1, rows), 1) & (tq - 1))
    cmask = (n_col * CMP_STRIDE + (CMP_BLOCK - 1)) <= t_row
    s_t = jnp.where(cmask, s_t, NEG)
    m = jnp.max(s_t, axis=0, keepdims=True)
    e = jnp.where(cmask, jnp.exp2(s_t - m), 0.0)
    l = jnp.sum(e, axis=0, keepdims=True)
    p_t = e * (1.0 / jnp.maximum(l, 1e-30))
    p_sum_t = p_t[:, head[0]]
    for r in range(1, NSA_REP):
        p_sum_t = p_sum_t + p_t[:, head[r]]
    o_cmp = lax.dot_general(p_t.astype(BF16), vc_ref[0, 0], (((0,), (0,)), ((), ())),
                            preferred_element_type=F32)

    n_slc = ovl_ref.shape[0]
    imp_t = jnp.dot(ovl_ref[...], p_sum_t, precision=lax.Precision.HIGHEST,
                    preferred_element_type=F32)
    j_t = lax.broadcasted_iota(jnp.int32, (n_slc, tq), 0)
    t_t = t0 + lax.broadcasted_iota(jnp.int32, (n_slc, tq), 1)
    causal = j_t * SLC_BLOCK <= t_t
    tb = lax.shift_right_logical(t_t, int(math.log2(SLC_BLOCK)))
    forced = jnp.where(j_t == 0, 1e6, jnp.where(j_t == tb, 1e6, jnp.where(j_t == tb - 1, 1e6, imp_t)))
    score = jnp.where(causal, forced, NEG)
    rank = jnp.zeros((n_slc, tq), F32)
    for jp in range(n_slc):
        row = score[jp:jp + 1, :]
        ge = jnp.where(row >= score, 1.0, 0.0)
        gt = jnp.where(row > score, 1.0, 0.0)
        rank = rank + jnp.where(j_t > jp, ge, gt)
    sel_bias = jnp.where(causal, jnp.where(rank < SLC_TOPK, 0.0, NEG), NEG)

    sel_rows = jnp.concatenate([sel_bias, jnp.zeros((LANES - n_slc, tq), F32)], axis=0).T
    q_aug = jnp.concatenate(
        [jnp.concatenate([q_heads[r], (sel_rows + qext_ref[0, r:r + 1, :]).astype(BF16)], axis=1)
         for r in range(NSA_REP)], axis=0)
    ck = SLC_CK
    n_chunks = (t0 + tq + ck - 1) // ck

    def slc_branch(n):
        mx = jnp.full((rows, LANES), NEG, F32)
        for c in range(n):
            cols = slice(c * ck, (c + 1) * ck)
            t = _nt_dot(q_aug, ks_aug[cols, :]) * ATT_SCALE2
            if c == n - 1:
                pos = c * ck + lax.broadcasted_iota(jnp.int32, (1, ck), 1)
                cb = jnp.where(pos <= t_col, 0.0, NEG)
                t = t + jnp.concatenate([cb] * NSA_REP, axis=0)
            s_ref[:, cols] = t
            mx = jnp.maximum(mx, _lane_fold(t, jnp.maximum))
        m_slc = jnp.max(mx, axis=-1, keepdims=True)
        acc = jnp.zeros((rows, 2 * HEAD_DIM), F32)
        for c in range(n):
            cols = slice(c * ck, (c + 1) * ck)
            p = jnp.exp2(s_ref[:, cols] - m_slc).astype(BF16)
            acc = acc + _dot_row_halves(p, vs_aug[cols, :])
        oslc_ref[...] = acc[:, 0:HEAD_DIM] * (1.0 / acc[:, HEAD_DIM:])

    for n in range(1, ks_aug.shape[0] // ck + 1):
        pl.when(n_chunks == n)(functools.partial(slc_branch, n))
    o_slc = oslc_ref[...]

    slope2 = [slopes_ref[g * NSA_REP + r] * LOG2E for r in range(NSA_REP)]
    k0 = pl.multiple_of(jnp.maximum(t0 - WINDOW, 0), tq)
    pos = k0 + lax.broadcasted_iota(jnp.int32, (1, WIN_SPAN), 1)
    dist = t_col - pos
    wbias = jnp.where(dist >= 0, jnp.where(dist < WINDOW, 0.0, NEG), NEG)
    posrel = (pos - t0).astype(F32)
    s_win = _nt_dot(qs, kw_ref[0, pl.ds(k0, WIN_SPAN), :])
    ps = []
    for r in range(NSA_REP):
        t = s_win[head[r]] * ATT_SCALE2 + slope2[r] * posrel + wbias
        m = jnp.max(t, axis=-1, keepdims=True)
        ps.append(jnp.exp2(t - m).astype(BF16))
    acc_w = _dot_row_halves(jnp.concatenate(ps, axis=0), vw_aug[pl.ds(k0, WIN_SPAN), :])
    o_win = acc_w[:, 0:HEAD_DIM] * (1.0 / acc_w[:, HEAD_DIM:])

    sig = jax.nn.sigmoid(gate_ref[0]).astype(BF16)
    gates = jnp.dot(sig, gsel_ref[0], preferred_element_type=F32)
    for r in range(NSA_REP):
        gt = [gates[:, (3 * r + br) * LANES:(3 * r + br + 1) * LANES] for br in range(3)]
        o = gt[0] * o_cmp[head[r]] + gt[1] * o_slc[head[r]] + gt[2] * o_win[head[r]]
        o = o * lax.rsqrt(jnp.mean(o * o, axis=-1, keepdims=True) + RMS_EPS) * gain_ref[0, r:r + 1, :]
        o_ref[0, :, r * HEAD_DIM:(r + 1) * HEAD_DIM] = o.astype(o_ref.dtype)


def _nsa_attention(proj_a, cmp_kv, gates3, gain, slopes, ovl_t, kext, qext, gsel, cast_arrays):
    b, s, _ = proj_a.shape
    n_chunks = cmp_kv.shape[3]
    n_q = s // NSA_TQ
    cast_specs, cast_shapes = _cast_specs(cast_arrays, b * NSA_GROUPS * n_q,
                                          lambda bi, g, i: (bi * NSA_GROUPS + g) * n_q + i)
    col0 = NSA_WIDTH // HEAD_DIM

    def kv_spec(idx):
        return pl.BlockSpec((1, s, HEAD_DIM), lambda bi, g, i, idx=idx: (bi, 0, col0 + idx * NSA_GROUPS + g))

    cmp_spec = pl.BlockSpec((1, 1, n_chunks, HEAD_DIM), lambda bi, g, i: (bi, g, 0, 0))
    in_specs = [
        pl.BlockSpec(memory_space=pltpu.SMEM),
        pl.BlockSpec((1, NSA_TQ, NSA_REP * HEAD_DIM), lambda bi, g, i: (bi, i, g)),
        cmp_spec, cmp_spec,
        kv_spec(2), kv_spec(3), kv_spec(4), kv_spec(5),
        pl.BlockSpec((1, NSA_TQ, LANES), lambda bi, g, i: (bi, i, 0)),
        pl.BlockSpec((1, NSA_REP, HEAD_DIM), lambda bi, g, i: (g, 0, 0)),
        pl.BlockSpec(ovl_t.shape, lambda bi, g, i: (0, 0)),
        pl.BlockSpec(kext.shape, lambda bi, g, i: (0, 0)),
        pl.BlockSpec((1, NSA_REP, LANES), lambda bi, g, i: (g, 0, 0)),
        pl.BlockSpec((1,) + gsel.shape[1:], lambda bi, g, i: (g, 0, 0)),
    ] + cast_specs
    aug = pltpu.VMEM((s, 2 * HEAD_DIM), BF16)
    return pl.pallas_call(
        functools.partial(_nsa_kernel, n_cast=len(cast_arrays)),
        grid=(b, NSA_GROUPS, n_q),
        in_specs=in_specs,
        out_specs=[pl.BlockSpec((1, NSA_TQ, NSA_REP * HEAD_DIM), lambda bi, g, i: (bi, i, g))] + cast_specs,
        out_shape=[jax.ShapeDtypeStruct((b, s, NSA_WIDTH), BF16)] + cast_shapes,
        scratch_shapes=[aug, aug, aug, pltpu.VMEM((NSA_REP * NSA_TQ, s), F32),
                        pltpu.VMEM((NSA_REP * NSA_TQ, HEAD_DIM), F32)],
        compiler_params=_params(3),
        name="nsa_attention",
    )(slopes, proj_a, cmp_kv[0], cmp_kv[1], proj_a, proj_a, proj_a, proj_a, gates3, gain, ovl_t,
      kext, qext, gsel, *cast_arrays)


def _diff_kernel(lq1_ref, lk1_ref, lq2_ref, lk2_ref, q_ref, k_ref, v_ref, g_ref, kext_ref, qext_ref,
                 *rest, n_cast, lambda_init):
    cast_src, o_ref, cast_dst = rest[:n_cast], rest[n_cast], rest[n_cast + 1:2 * n_cast + 1]
    k_aug, s_ref, acc_ref = rest[2 * n_cast + 1:]
    _cast_blocks(cast_src, cast_dst)
    i = pl.program_id(2)
    tq, ck = DIFF_TQ, DIFF_CK
    t0 = i * tq

    @pl.when(i == 0)
    def _():
        for mi in range(2):
            k_aug[mi, :, 0:DIFF_DIM] = k_ref[0, :, mi * DIFF_DIM:(mi + 1) * DIFF_DIM]
            k_aug[mi, :, DIFF_DIM:] = kext_ref[...]

    q = q_ref[0]
    qext = jnp.broadcast_to(qext_ref[0], (tq, LANES)).astype(BF16)
    q_aug = [jnp.concatenate([q[:, mi * DIFF_DIM:(mi + 1) * DIFF_DIM], qext], axis=1) for mi in range(2)]
    t_col = t0 + lax.broadcasted_iota(jnp.int32, (tq, 1), 0)

    def branch(n):
        for mi in range(2):
            mx = jnp.full((tq, LANES), NEG, F32)
            for c in range(n):
                cols = slice(c * ck, (c + 1) * ck)
                t = _nt_dot(q_aug[mi], k_aug[mi, cols, :]) * ATT_SCALE2
                if c == n - 1:
                    pos = c * ck + lax.broadcasted_iota(jnp.int32, (1, ck), 1)
                    t = jnp.where(pos <= t_col, t, NEG)
                s_ref[mi, :, cols] = t
                mx = jnp.maximum(mx, _lane_fold(t, jnp.maximum))
            m_row = jnp.max(mx, axis=-1, keepdims=True)
            l_fold = jnp.zeros((tq, LANES), F32)
            acc = jnp.zeros((tq, DIFF_VDIM), F32)
            for c in range(n):
                cols = slice(c * ck, (c + 1) * ck)
                p = jnp.exp2(s_ref[mi, :, cols] - m_row)
                l_fold = l_fold + _lane_fold(p, jnp.add)
                acc = acc + _dot_row_halves(p.astype(BF16), v_ref[0, cols, :])
            acc_ref[mi] = acc * (1.0 / jnp.sum(l_fold, axis=-1, keepdims=True))

    for n in range(1, k_aug.shape[1] // ck + 1):
        pl.when(i + 1 == n)(functools.partial(branch, n))

    lam = (jnp.exp(jnp.sum(lq1_ref[...] * lk1_ref[...], axis=-1, keepdims=True))
           - jnp.exp(jnp.sum(lq2_ref[...] * lk2_ref[...], axis=-1, keepdims=True))
           + lambda_init)
    o = acc_ref[0] - lam * acc_ref[1]
    o = o * lax.rsqrt(jnp.mean(o * o, axis=-1, keepdims=True) + RMS_EPS) * g_ref[...]
    o_ref[0] = (o * (1.0 - lambda_init)).astype(o_ref.dtype)


def _diff_attention(proj_b, lq1, lk1, lq2, lk2, gain, kext, qext, lambda_init, cast_arrays):
    b, s, _ = proj_b.shape
    n_q = s // DIFF_TQ
    cast_specs, cast_shapes = _cast_specs(cast_arrays, b * DIFF_HEADS * n_q,
                                          lambda bi, h, i: (bi * DIFF_HEADS + h) * n_q + i)
    qcol, kcol, vcol = 0, DIFF_HEADS, 2 * DIFF_HEADS
    vec = pl.BlockSpec((1, DIFF_DIM), lambda bi, h, i: (0, 0))
    in_specs = [
        vec, vec, vec, vec,
        pl.BlockSpec((1, DIFF_TQ, DIFF_VDIM), lambda bi, h, i: (bi, i, qcol + h)),
        pl.BlockSpec((1, s, DIFF_VDIM), lambda bi, h, i: (bi, 0, kcol + h)),
        pl.BlockSpec((1, s, DIFF_VDIM), lambda bi, h, i: (bi, 0, vcol + h)),
        pl.BlockSpec((1, DIFF_VDIM), lambda bi, h, i: (0, 0)),
        pl.BlockSpec(kext.shape, lambda bi, h, i: (0, 0)),
        pl.BlockSpec((1, 1, LANES), lambda bi, h, i: (h, 0, 0)),
    ] + cast_specs
    return pl.pallas_call(
        functools.partial(_diff_kernel, n_cast=len(cast_arrays), lambda_init=lambda_init),
        grid=(b, DIFF_HEADS, n_q),
        in_specs=in_specs,
        out_specs=[pl.BlockSpec((1, DIFF_TQ, DIFF_VDIM), lambda bi, h, i: (bi, i, h))] + cast_specs,
        out_shape=[jax.ShapeDtypeStruct((b, s, DIFF_WIDTH), BF16)] + cast_shapes,
        scratch_shapes=[pltpu.VMEM((2, s, 2 * DIFF_DIM), BF16), pltpu.VMEM((2, DIFF_TQ, s), F32),
                        pltpu.VMEM((2, DIFF_TQ, DIFF_VDIM), F32)],
        compiler_params=_params(3),
        name="diff_attention",
    )(lq1.reshape(1, -1), lk1.reshape(1, -1), lq2.reshape(1, -1), lk2.reshape(1, -1),
      proj_b, proj_b, proj_b, gain.reshape(1, -1), kext, qext, *cast_arrays)


def _alibi_slopes(n_heads):
    return np.array([2.0 ** (-8.0 * (h + 1) / n_heads) for h in range(n_heads)], np.float32)


def _overlap_t(seq):
    n_cmp = seq // CMP_STRIDE
    n_slc = seq // SLC_BLOCK
    cs = np.arange(n_cmp) * CMP_STRIDE
    ss = np.arange(n_slc) * SLC_BLOCK
    ov = np.clip(np.minimum(cs[:, None] + CMP_BLOCK, ss[None, :] + SLC_BLOCK)
                 - np.maximum(cs[:, None], ss[None, :]), 0, None)
    ov[(seq - CMP_BLOCK) // CMP_STRIDE + 1:, :] = 0
    return np.ascontiguousarray((ov / CMP_BLOCK).astype(np.float32).T)


def _round_to_bf16(x):
    bits = np.array(x, np.float32).view(np.uint32)
    bits = (bits + np.uint32(0x7FFF) + ((bits >> np.uint32(16)) & np.uint32(1))) & np.uint32(0xFFFF0000)
    return np.float64(bits.view(np.float32))


def _bf16_pieces(x, n):
    out, rem = [], np.float64(x)
    for _ in range(n):
        piece = _round_to_bf16(rem)
        out.append(piece)
        rem = rem - piece
    return out


def _key_ext(seq):
    ext = np.zeros((seq, LANES), np.float32)
    pos = np.arange(seq)
    ext[pos, pos // SLC_BLOCK] = 1.0
    centred = pos - seq // 2
    hi = (centred // 256) * 256
    lo = centred - hi
    n_slc = seq // SLC_BLOCK
    ext[:, n_slc:n_slc + SLOPE_PIECES] = hi[:, None]
    ext[:, n_slc + SLOPE_PIECES:n_slc + 2 * SLOPE_PIECES] = lo[:, None]
    return ext


def _query_ext(slopes, n_slc):
    ext = np.zeros((len(slopes), LANES), np.float32)
    for h, slope in enumerate(slopes):
        pieces = _bf16_pieces(np.float64(slope) / ATT_SCALE, SLOPE_PIECES)
        ext[h, n_slc:n_slc + SLOPE_PIECES] = pieces
        ext[h, n_slc + SLOPE_PIECES:n_slc + 2 * SLOPE_PIECES] = pieces
    return ext


def _gate_select():
    per = 3 * NSA_REP
    sel = np.zeros((NSA_GROUPS, LANES, per * LANES), np.float32)
    for g in range(NSA_GROUPS):
        for k in range(per):
            sel[g, g * per + k, k * LANES:(k + 1) * LANES] = 1.0
    return sel


def _layer(x, w_in, cmp_pe_k, cmp_w1_k, cmp_w2_k, cmp_pe_v, cmp_w1_v, cmp_w2_v, nsa_out_g,
           lambda_q1, lambda_k1, lambda_q2, lambda_k2, diff_subln_g, w_out, ln1_g, ln1_b,
           w_ff1, w_ff2, ln2_g, ln2_b, lambda_init):
    b, s, d = x.shape
    t = b * s
    x2 = x.reshape(t, d)

    w_t = jnp.swapaxes(w_in, 0, 1)
    x_bf, gates = _gate_cast(x2, w_t, row0=GATE_START)
    proj_a = _inproj(x_bf, w_t, BF16, row0=0, n=PROJ_A_COLS, name="in_proj_nsa")
    proj_b = _inproj(x_bf, w_t, BF16, row0=GATE_START, n=PROJ_B_COLS, shift=GATE_COLS,
                     name="in_proj_diff")
    proj_a = proj_a.reshape(b, s, PROJ_A_COLS)
    proj_b = proj_b.reshape(b, s, PROJ_B_COLS)

    n_chunks = s // CMP_STRIDE
    kv_cmp = proj_a[:, :, NSA_WIDTH:NSA_WIDTH + 2 * NSA_GROUPS * HEAD_DIM]
    kv_cmp = kv_cmp.reshape(b, n_chunks, CMP_STRIDE, 2, NSA_GROUPS, HEAD_DIM)
    chunks = kv_cmp.transpose(3, 0, 4, 1, 2, 5).reshape(2, b, NSA_GROUPS, n_chunks,
                                                        CMP_STRIDE * HEAD_DIM)
    pe = jnp.stack([cmp_pe_k, cmp_pe_v]).reshape(2, 1, CMP_BLOCK * HEAD_DIM).astype(BF16)
    w1 = jnp.stack([cmp_w1_k, cmp_w1_v]).astype(BF16)
    w2 = jnp.stack([cmp_w2_k, cmp_w2_v]).astype(BF16)
    cmp_kv = _compress(chunks, pe, w1, w2)

    n_slc = s // SLC_BLOCK
    kext = jnp.asarray(_key_ext(s)).astype(BF16)
    o_nsa, w_ff1_bf, w_out_bf = _nsa_attention(
        proj_a, cmp_kv, gates.reshape(b, s, LANES),
        nsa_out_g.reshape(NSA_GROUPS, NSA_REP, HEAD_DIM),
        jnp.asarray(_alibi_slopes(NSA_HEADS)), jnp.asarray(_overlap_t(s)), kext,
        jnp.asarray(_query_ext(_alibi_slopes(NSA_HEADS), n_slc)).reshape(NSA_GROUPS, NSA_REP, LANES),
        jnp.asarray(_gate_select()).astype(BF16), [w_ff1, w_out])
    o_diff, w_ff2_bf = _diff_attention(
        proj_b, lambda_q1, lambda_k1, lambda_q2, lambda_k2, diff_subln_g, kext,
        jnp.asarray(_query_ext(_alibi_slopes(DIFF_HEADS), n_slc)).reshape(DIFF_HEADS, 1, LANES),
        lambda_init, [w_ff2])

    h1 = _outproj(o_nsa.reshape(t, NSA_WIDTH), o_diff.reshape(t, DIFF_WIDTH), w_out_bf, x2)
    x1, x1_bf = _layer_norm(h1, ln1_g, ln1_b, (F32, BF16))
    hid = _ff1(x1_bf, w_ff1_bf)
    h2 = _ff2(hid, w_ff2_bf, x1)
    (out,) = _layer_norm(h2, ln2_g, ln2_b, (F32,))
    return out.reshape(b, s, d)


def kernel(x, w_in, cmp_pe_k, cmp_w1_k, cmp_w2_k, cmp_pe_v, cmp_w1_v, cmp_w2_v, nsa_out_g,
           lambda_q1, lambda_k1, lambda_q2, lambda_k2, diff_subln_g, w_out, ln1_g, ln1_b,
           w_ff1, w_ff2, ln2_g, ln2_b):
    for l in range(DEPTH):
        lambda_init = 0.8 - 0.6 * math.exp(-0.3 * l)
        x = _layer(x, w_in[l], cmp_pe_k[l], cmp_w1_k[l], cmp_w2_k[l], cmp_pe_v[l], cmp_w1_v[l],
                   cmp_w2_v[l], nsa_out_g[l], lambda_q1[l], lambda_k1[l], lambda_q2[l],
                   lambda_k2[l], diff_subln_g[l], w_out[l], ln1_g[l], ln1_b[l], w_ff1[l],
                   w_ff2[l], ln2_g[l], ln2_b[l], lambda_init)
    return x
```

```python
import functools
import math

import numpy as np
import jax
import jax.numpy as jnp
from jax import lax
from jax.experimental import pallas as pl
from jax.experimental.pallas import tpu as pltpu

F32 = jnp.float32
BF16 = jnp.bfloat16

D_MODEL = 4096
HEAD_DIM = 128
NSA_HEADS = 16
NSA_GROUPS = 4
NSA_REP = 4
CMP_BLOCK = 32
CMP_STRIDE = 16
CMP_HIDDEN = 256
SLC_BLOCK = 64
SLC_TOPK = 16
WINDOW = 512
DIFF_HEADS = 8
DIFF_DIM = 128
DIFF_VDIM = 256
NSA_WIDTH = NSA_HEADS * HEAD_DIM
KV_WIDTH = 6 * NSA_GROUPS * HEAD_DIM
GATE_COLS = 3 * NSA_HEADS
GATE_START = NSA_WIDTH + KV_WIDTH
DIFF_WIDTH = DIFF_HEADS * DIFF_VDIM
PROJ_A_COLS = GATE_START
PROJ_B_COLS = 3 * DIFF_WIDTH
D_FF = 4 * D_MODEL
LN_EPS = 1e-5
RMS_EPS = 1e-6
NEG = -1e30
DEPTH = 1
DEEPNORM_ALPHA = (2.0 * DEPTH) ** 0.25
ATT_SCALE = HEAD_DIM ** -0.5
LOG2E = 1.4426950408889634
ATT_SCALE2 = ATT_SCALE * LOG2E

LANES = 128
BF16_SUBLANES = 16
VMEM_LIMIT_BYTES = 56 * 1024 * 1024

MM_BM = 1024
MM_BN = 1024
INPROJ_BN = 512
OUTPROJ_BN = 512
GATE_BM = 512
FF2_BK = 2048
CAST_COLS = 512
LN_ROWS = 256
NSA_TQ = 256
SLC_CK = 512
WIN_SPAN = WINDOW + NSA_TQ
DIFF_TQ = 512
DIFF_CK = 512
SLOPE_PIECES = 4


def _params(n_axes):
    return pltpu.CompilerParams(dimension_semantics=("arbitrary",) * n_axes,
                                vmem_limit_bytes=VMEM_LIMIT_BYTES)


def _load_weight_rows(b_ref, tail_ref, shift, cols):
    if shift:
        return jnp.concatenate([b_ref[shift:, cols], tail_ref[0:shift, cols]], axis=0)
    return b_ref[:, cols]


def _inproj_kernel(*refs, shift, n_inner, scaled_tiles, chunked_tiles):
    refs = list(refs)
    a_ref, b_ref = refs[:2]
    tail_ref = refs[2] if shift else None
    o_ref = refs[3 if shift else 2]
    wb_ref = refs[-1]
    j = pl.program_id(0)
    i = pl.program_id(1)

    def prepare_slice():
        width = wb_ref.shape[1] // n_inner
        cols = pl.ds(pl.multiple_of(i * width, width), width)
        blk = _load_weight_rows(b_ref, tail_ref, shift, cols)
        blk = blk * jnp.where(j < scaled_tiles, ATT_SCALE2, 1.0)
        wb_ref[j % 2, cols, :] = blk.T.astype(BF16)

    pl.when(j == 0)(prepare_slice)

    @pl.when(j > 0)
    def _():
        prepare_slice()
        acc = jnp.dot(a_ref[...], wb_ref[(j + 1) % 2], preferred_element_type=F32)
        o_ref[...] = acc.astype(o_ref.dtype)
        if chunked_tiles:
            for grp in range(refs[-2].shape[0]):
                refs[-2][grp] = acc[:, grp * HEAD_DIM:(grp + 1) * HEAD_DIM]

    if chunked_tiles:
        first, n_ch = chunked_tiles
        c_ref, stage_ref = refs[-3], refs[-2]

        @pl.when((j - 1 >= first) & (j - 1 < first + n_ch))
        def _():
            n_groups, n_rows = c_ref.shape[2], c_ref.shape[3]
            for grp in range(n_groups):
                for l in range(CMP_STRIDE):
                    c_ref[0, 0, grp, :, l * HEAD_DIM:(l + 1) * HEAD_DIM] = stage_ref[
                        grp, pl.ds(l, n_rows, stride=CMP_STRIDE), :].astype(c_ref.dtype)


def _inproj(a, wt, out_dtype, *, row0, n, scaled_cols, shift=0, chunked=None, bn=INPROJ_BN,
            bm=MM_BM, name):
    m, k = a.shape
    n_tiles = n // bn
    n_inner = m // bm
    assert k % ((m // bm) * LANES) == 0
    jb = row0 // bn

    def tile(j):
        return jnp.minimum(j, n_tiles - 1)

    def row_tile(j, i):
        return jnp.where(j == 0, 0, i)

    in_specs = [pl.BlockSpec((bm, k), lambda j, i: (row_tile(j, i), 0)),
                pl.BlockSpec((bn, k), lambda j, i: (jb + tile(j), 0))]
    args = [a, wt]
    if shift:
        per = bn // LANES
        tb = row0 // LANES
        in_specs.append(pl.BlockSpec((LANES, k), lambda j, i: (tb + per * (tile(j) + 1), 0)))
        args.append(wt)
    out_specs = [pl.BlockSpec((bm, bn), lambda j, i: (row_tile(j, i), jnp.maximum(j - 1, 0)))]
    out_shape = [jax.ShapeDtypeStruct((m, n), out_dtype)]
    scratch = [pltpu.VMEM((2, k, bn), BF16)]
    chunked_tiles = None
    if chunked:
        col0, n_kinds, seq = chunked
        first = col0 // bn
        per_seq = seq // bm
        n_rows = bm // CMP_STRIDE
        chunked_tiles = (first, n_kinds)

        def chunk_block(j, i):
            t = j - 1
            ii = jnp.where(t < first, 0, jnp.where(t >= first + n_kinds, n_inner - 1, i))
            return (jnp.clip(t - first, 0, n_kinds - 1), ii // per_seq, 0, ii % per_seq, 0)

        out_specs.append(pl.BlockSpec((1, 1, bn // HEAD_DIM, n_rows, CMP_STRIDE * HEAD_DIM), chunk_block))
        out_shape.append(jax.ShapeDtypeStruct(
            (n_kinds, m // seq, bn // HEAD_DIM, seq // CMP_STRIDE, CMP_STRIDE * HEAD_DIM), out_dtype))
        scratch.insert(0, pltpu.VMEM((bn // HEAD_DIM, bm, HEAD_DIM), F32))
    res = pl.pallas_call(
        functools.partial(_inproj_kernel, shift=shift, n_inner=n_inner,
                          scaled_tiles=scaled_cols // bn, chunked_tiles=chunked_tiles),
        grid=(n_tiles + 1, n_inner),
        in_specs=in_specs,
        out_specs=out_specs,
        out_shape=out_shape,
        scratch_shapes=scratch,
        compiler_params=_params(2),
        name=name,
    )(*args)
    return res if chunked else res[0]


def _gate_cast_kernel(x_ref, wt_ref, xb_ref, g_ref, wb_ref):
    @pl.when(pl.program_id(0) == 0)
    def _():
        k = wb_ref.shape[0]
        for c0 in range(0, k, CAST_COLS):
            wb_ref[c0:c0 + CAST_COLS, :] = wt_ref[:, c0:c0 + CAST_COLS].T.astype(BF16)

    xb = x_ref[...].astype(BF16)
    xb_ref[...] = xb
    g_ref[...] = jnp.dot(xb, wb_ref[...], preferred_element_type=F32)


def _gate_cast(x, wt, *, row0, bm=GATE_BM):
    m, k = x.shape
    return pl.pallas_call(
        _gate_cast_kernel,
        grid=(m // bm,),
        in_specs=[pl.BlockSpec((bm, k), lambda i: (i, 0)),
                  pl.BlockSpec((LANES, k), lambda i: (row0 // LANES, 0))],
        out_specs=[pl.BlockSpec((bm, k), lambda i: (i, 0)),
                   pl.BlockSpec((bm, LANES), lambda i: (i, 0))],
        out_shape=[jax.ShapeDtypeStruct((m, k), BF16), jax.ShapeDtypeStruct((m, LANES), F32)],
        scratch_shapes=[pltpu.VMEM((k, LANES), BF16)],
        compiler_params=_params(1),
        name="gate_proj_cast",
    )(x, wt)


def _mm_kernel(a_ref, b_ref, o_ref):
    acc = jnp.dot(a_ref[...], b_ref[...], preferred_element_type=F32)
    o_ref[...] = jnp.square(jnp.maximum(acc, 0.0)).astype(o_ref.dtype)


def _ff1(a, b, *, bm=MM_BM, bn=MM_BN):
    m, k = a.shape
    _, n = b.shape
    return pl.pallas_call(
        _mm_kernel,
        grid=(m // bm, n // bn),
        in_specs=[pl.BlockSpec((bm, k), lambda i, j: (i, 0)),
                  pl.BlockSpec((k, bn), lambda i, j: (0, j))],
        out_specs=pl.BlockSpec((bm, bn), lambda i, j: (i, j)),
        out_shape=jax.ShapeDtypeStruct((m, n), BF16),
        compiler_params=_params(2),
        name="ff1",
    )(a, b)


def _outproj_kernel(a1_ref, a2_ref, b_ref, res_ref, o_ref):
    half = a1_ref.shape[1]
    acc = jnp.dot(a1_ref[...], b_ref[0:half, :], preferred_element_type=F32)
    acc = acc + jnp.dot(a2_ref[...], b_ref[half:2 * half, :], preferred_element_type=F32)
    o_ref[...] = DEEPNORM_ALPHA * res_ref[...] + acc


def _outproj(a1, a2, b, res, *, bm=MM_BM, bn=OUTPROJ_BN):
    m, k1 = a1.shape
    _, n = b.shape
    return pl.pallas_call(
        _outproj_kernel,
        grid=(m // bm, n // bn),
        in_specs=[pl.BlockSpec((bm, k1), lambda i, j: (i, 0)),
                  pl.BlockSpec((bm, k1), lambda i, j: (i, 0)),
                  pl.BlockSpec((2 * k1, bn), lambda i, j: (0, j)),
                  pl.BlockSpec((bm, bn), lambda i, j: (i, j))],
        out_specs=pl.BlockSpec((bm, bn), lambda i, j: (i, j)),
        out_shape=jax.ShapeDtypeStruct((m, n), F32),
        compiler_params=_params(2),
        name="outproj",
    )(a1, a2, b, res)


def _ff2_kernel(a_ref, b_ref, res_ref, o_ref, acc_ref):
    kk = pl.program_id(2)

    @pl.when(kk == 0)
    def _():
        acc_ref[...] = jnp.zeros_like(acc_ref)

    acc_ref[...] += jnp.dot(a_ref[...], b_ref[...], preferred_element_type=F32)

    @pl.when(kk == pl.num_programs(2) - 1)
    def _():
        o_ref[...] = DEEPNORM_ALPHA * res_ref[...] + acc_ref[...]


def _ff2(a, b, res, *, bm=MM_BM, bn=MM_BN, bk=FF2_BK):
    m, k = a.shape
    _, n = b.shape
    return pl.pallas_call(
        _ff2_kernel,
        grid=(m // bm, n // bn, k // bk),
        in_specs=[pl.BlockSpec((bm, bk), lambda i, j, kk: (i, kk)),
                  pl.BlockSpec((bk, bn), lambda i, j, kk: (kk, j)),
                  pl.BlockSpec((bm, bn), lambda i, j, kk: (i, j))],
        out_specs=pl.BlockSpec((bm, bn), lambda i, j, kk: (i, j)),
        out_shape=jax.ShapeDtypeStruct((m, n), F32),
        scratch_shapes=[pltpu.VMEM((bm, bn), F32)],
        compiler_params=_params(3),
        name="ff2",
    )(a, b, res)


def _cast_blocks(src_refs, dst_refs):
    for src, dst in zip(src_refs, dst_refs):
        dst[...] = src[...].astype(dst.dtype)


def _cast_specs(arrays, n_steps, step_of):
    specs, shapes = [], []
    for arr in arrays:
        rows, cols = arr.shape
        assert rows % (n_steps * BF16_SUBLANES) == 0
        blk = (rows // n_steps, cols)
        specs.append(pl.BlockSpec(blk, lambda *ids: (step_of(*ids), 0)))
        shapes.append(jax.ShapeDtypeStruct((rows, cols), BF16))
    return specs, shapes


def _ln_kernel(x_ref, g_ref, b_ref, *o_refs):
    x = x_ref[...]
    mu = jnp.mean(x, axis=-1, keepdims=True)
    xc = x - mu
    var = jnp.mean(xc * xc, axis=-1, keepdims=True)
    y = xc * lax.rsqrt(var + LN_EPS) * g_ref[...] + b_ref[...]
    for o_ref in o_refs:
        o_ref[...] = y.astype(o_ref.dtype)


def _layer_norm(x, g, b, out_dtypes):
    m, d = x.shape
    row_spec = pl.BlockSpec((LN_ROWS, d), lambda i: (i, 0))
    vec_spec = pl.BlockSpec((1, d), lambda i: (0, 0))
    return pl.pallas_call(
        _ln_kernel,
        grid=(m // LN_ROWS,),
        in_specs=[row_spec, vec_spec, vec_spec],
        out_specs=[row_spec for _ in out_dtypes],
        out_shape=[jax.ShapeDtypeStruct((m, d), dt) for dt in out_dtypes],
        compiler_params=_params(1),
        name="layer_norm",
    )(x, g.reshape(1, d), b.reshape(1, d))


def _gelu_tanh(x):
    c = math.sqrt(2.0 / math.pi)
    return 0.5 * x * (1.0 + jnp.tanh(c * (x + 0.044715 * (x * x * x))))


def _compress_kernel(c_ref, pe_ref, w1_ref, w2_ref, o_ref):
    half = CMP_STRIDE * HEAD_DIM
    c = c_ref[0, 0, 0]
    n_chunks = c.shape[0]
    y1 = jnp.dot(c, w1_ref[0, 0:half, :], preferred_element_type=F32)
    y2 = jnp.dot(c, w1_ref[0, half:2 * half, :], preferred_element_type=F32)
    pe8 = jnp.broadcast_to(pe_ref[0], (8, 2 * half))
    pb = jnp.dot(pe8, w1_ref[0], preferred_element_type=F32)[0:1, :]
    h = y1 + pltpu.roll(y2, n_chunks - 1, 0) + pb
    a = _gelu_tanh(h).astype(BF16)
    out = jnp.dot(a, w2_ref[0], preferred_element_type=F32)
    row = lax.broadcasted_iota(jnp.int32, out.shape, 0)
    out = jnp.where(row < n_chunks - 1, out, 0.0)
    o_ref[0, 0, 0] = out.astype(o_ref.dtype)


def _compress(chunks, pe, w1, w2):
    _, b, g, n_chunks, width = chunks.shape
    return pl.pallas_call(
        _compress_kernel,
        grid=(2, b, g),
        in_specs=[pl.BlockSpec((1, 1, 1, n_chunks, width), lambda t, i, j: (t, i, j, 0, 0)),
                  pl.BlockSpec((1, 1, 2 * width), lambda t, i, j: (t, 0, 0)),
                  pl.BlockSpec((1, 2 * width, CMP_HIDDEN), lambda t, i, j: (t, 0, 0)),
                  pl.BlockSpec((1, CMP_HIDDEN, HEAD_DIM), lambda t, i, j: (t, 0, 0))],
        out_specs=pl.BlockSpec((1, 1, 1, n_chunks, HEAD_DIM), lambda t, i, j: (t, i, j, 0, 0)),
        out_shape=jax.ShapeDtypeStruct((2, b, g, n_chunks, HEAD_DIM), BF16),
        compiler_params=_params(3),
        name="compress",
    )(chunks, pe, w1, w2)


def _nt_dot(a, b):
    return lax.dot_general(a, b, (((1,), (1,)), ((), ())), preferred_element_type=F32)


def _dot_row_halves(a, b):
    half = a.shape[0] // 2
    return jnp.concatenate([jnp.dot(a[:half], b, preferred_element_type=F32),
                            jnp.dot(a[half:], b, preferred_element_type=F32)], axis=0)


def _lane_fold(x, op):
    out = x[:, 0:LANES]
    for c0 in range(LANES, x.shape[1], LANES):
        out = op(out, x[:, c0:c0 + LANES])
    return out


def _nsa_kernel(slopes_ref, q_ref, kc_ref, vc_ref, ks_ref, vs_ref, kw_ref, vw_ref,
                gate_ref, gain_ref, ovl_ref, kext_ref, qext_ref, gsel_ref, *rest, n_cast):
    cast_src, o_ref, cast_dst = rest[:n_cast], rest[n_cast], rest[n_cast + 1:2 * n_cast + 1]
    ks_aug, vs_aug, vw_aug, s_ref, oslc_ref = rest[2 * n_cast + 1:]
    _cast_blocks(cast_src, cast_dst)
    g = pl.program_id(1)
    i = pl.program_id(2)
    tq = NSA_TQ
    rows = NSA_REP * tq
    t0 = i * tq

    @pl.when(i == 0)
    def _():
        ks_aug[:, 0:HEAD_DIM] = ks_ref[0]
        ks_aug[:, HEAD_DIM:] = kext_ref[...]
        ones = jnp.ones((ks_aug.shape[0], HEAD_DIM), BF16)
        vs_aug[:, 0:HEAD_DIM] = vs_ref[0]
        vs_aug[:, HEAD_DIM:] = ones
        vw_aug[:, 0:HEAD_DIM] = vw_ref[0]
        vw_aug[:, HEAD_DIM:] = ones

    q = q_ref[0]
    q_heads = [q[:, r * HEAD_DIM:(r + 1) * HEAD_DIM] for r in range(NSA_REP)]
    qs = jnp.concatenate(q_heads, axis=0)
    t_col = t0 + lax.broadcasted_iota(jnp.int32, (tq, 1), 0)
    head = [slice(r * tq, (r + 1) * tq) for r in range(NSA_REP)]

    n_cmp = kc_ref.shape[2]
    s_t = _nt_dot(kc_ref[0, 0], qs)
    n_col = lax.broadcasted_iota(jnp.int32, (n_cmp, 1), 0)
    t_row = t0 + (lax.broadcasted_iota(jnp.int32, (1, rows), 1) & (tq - 1))
    cmask = (n_col * CMP_STRIDE + (CMP_BLOCK - 1)) <= t_row
    s_t = jnp.where(cmask, s_t, NEG)
    m = jnp.max(s_t, axis=0, keepdims=True)
    e = jnp.where(cmask, jnp.exp2(s_t - m), 0.0)
    l = jnp.sum(e, axis=0, keepdims=True)
    p_t = e * (1.0 / jnp.maximum(l, 1e-30))
    p_sum_t = p_t[:, head[0]]
    for r in range(1, NSA_REP):
        p_sum_t = p_sum_t + p_t[:, head[r]]
    o_cmp = lax.dot_general(p_t.astype(BF16), vc_ref[0, 0], (((0,), (0,)), ((), ())),
                            preferred_element_type=F32)

    n_slc = ovl_ref.shape[0]
    imp_t = jnp.dot(ovl_ref[...], p_sum_t, precision=lax.Precision.HIGHEST,
                    preferred_element_type=F32)
    j_t = lax.broadcasted_iota(jnp.int32, (n_slc, tq), 0)
    t_t = t0 + lax.broadcasted_iota(jnp.int32, (n_slc, tq), 1)
    causal = j_t * SLC_BLOCK <= t_t
    tb = lax.shift_right_logical(t_t, int(math.log2(SLC_BLOCK)))
    forced = jnp.where(j_t == 0, 1e6, jnp.where(j_t == tb, 1e6, jnp.where(j_t == tb - 1, 1e6, imp_t)))
    score = jnp.where(causal, forced, NEG)
    rank = jnp.zeros((n_slc, tq), F32)
    for jp in range(n_slc):
        row = score[jp:jp + 1, :]
        ge = jnp.where(row >= score, 1.0, 0.0)
        gt = jnp.where(row > score, 1.0, 0.0)
        rank = rank + jnp.where(j_t > jp, ge, gt)
    sel_bias = jnp.where(causal, jnp.where(rank < SLC_TOPK, 0.0, NEG), NEG)

    sel_rows = jnp.concatenate([sel_bias, jnp.zeros((LANES - n_slc, tq), F32)], axis=0).T
    q_aug = jnp.concatenate(
        [jnp.concatenate([q_heads[r], (sel_rows + qext_ref[0, r:r + 1, :]).astype(BF16)], axis=1)
         for r in range(NSA_REP)], axis=0)
    ck = SLC_CK
    n_chunks = (t0 + tq + ck - 1) // ck

    def slc_branch(n):
        mx = jnp.full((rows, LANES), NEG, F32)
        for c in range(n):
            cols = slice(c * ck, (c + 1) * ck)
            t = _nt_dot(q_aug, ks_aug[cols, :])
            if c == n - 1:
                pos = c * ck + lax.broadcasted_iota(jnp.int32, (1, ck), 1)
                cb = jnp.where(pos <= t_col, 0.0, NEG)
                t = t + jnp.concatenate([cb] * NSA_REP, axis=0)
            s_ref[:, cols] = t
            mx = jnp.maximum(mx, _lane_fold(t, jnp.maximum))
        m_slc = jnp.max(mx, axis=-1, keepdims=True)
        acc = jnp.zeros((rows, 2 * HEAD_DIM), F32)
        for c in range(n):
            cols = slice(c * ck, (c + 1) * ck)
            p = jnp.exp2(s_ref[:, cols] - m_slc).astype(BF16)
            acc = acc + _dot_row_halves(p, vs_aug[cols, :])
        oslc_ref[...] = acc[:, 0:HEAD_DIM] * (1.0 / acc[:, HEAD_DIM:])

    for n in range(1, ks_aug.shape[0] // ck + 1):
        pl.when(n_chunks == n)(functools.partial(slc_branch, n))
    o_slc = oslc_ref[...]

    slope2 = [slopes_ref[g * NSA_REP + r] * LOG2E for r in range(NSA_REP)]
    k0 = pl.multiple_of(jnp.maximum(t0 - WINDOW, 0), tq)
    pos = k0 + lax.broadcasted_iota(jnp.int32, (1, WIN_SPAN), 1)
    dist = t_col - pos
    wbias = jnp.where(dist >= 0, jnp.where(dist < WINDOW, 0.0, NEG), NEG)
    posrel = (pos - t0).astype(F32)
    s_win = _nt_dot(qs, kw_ref[0, pl.ds(k0, WIN_SPAN), :])
    ps = []
    for r in range(NSA_REP):
        t = s_win[head[r]] + slope2[r] * posrel + wbias
        m = jnp.max(t, axis=-1, keepdims=True)
        ps.append(jnp.exp2(t - m).astype(BF16))
    acc_w = _dot_row_halves(jnp.concatenate(ps, axis=0), vw_aug[pl.ds(k0, WIN_SPAN), :])
    o_win = acc_w[:, 0:HEAD_DIM] * (1.0 / acc_w[:, HEAD_DIM:])

    sig = jax.nn.sigmoid(gate_ref[0]).astype(BF16)
    gates = jnp.dot(sig, gsel_ref[0], preferred_element_type=F32)
    for r in range(NSA_REP):
        gt = [gates[:, (3 * r + br) * LANES:(3 * r + br + 1) * LANES] for br in range(3)]
        o = gt[0] * o_cmp[head[r]] + gt[1] * o_slc[head[r]] + gt[2] * o_win[head[r]]
        o = o * lax.rsqrt(jnp.mean(o * o, axis=-1, keepdims=True) + RMS_EPS) * gain_ref[0, r:r + 1, :]
        o_ref[0, :, r * HEAD_DIM:(r + 1) * HEAD_DIM] = o.astype(o_ref.dtype)


def _nsa_attention(proj_a, cmp_kv, gates3, gain, slopes, ovl_t, kext, qext, gsel, cast_arrays):
    b, s, _ = proj_a.shape
    n_chunks = cmp_kv.shape[3]
    n_q = s // NSA_TQ
    cast_specs, cast_shapes = _cast_specs(cast_arrays, b * NSA_GROUPS * n_q,
                                          lambda bi, g, i: (bi * NSA_GROUPS + g) * n_q + i)
    col0 = NSA_WIDTH // HEAD_DIM

    def kv_spec(idx):
        return pl.BlockSpec((1, s, HEAD_DIM), lambda bi, g, i, idx=idx: (bi, 0, col0 + idx * NSA_GROUPS + g))

    cmp_spec = pl.BlockSpec((1, 1, n_chunks, HEAD_DIM), lambda bi, g, i: (bi, g, 0, 0))
    in_specs = [
        pl.BlockSpec(memory_space=pltpu.SMEM),
        pl.BlockSpec((1, NSA_TQ, NSA_REP * HEAD_DIM), lambda bi, g, i: (bi, i, g)),
        cmp_spec, cmp_spec,
        kv_spec(2), kv_spec(3), kv_spec(4), kv_spec(5),
        pl.BlockSpec((1, NSA_TQ, LANES), lambda bi, g, i: (bi, i, 0)),
        pl.BlockSpec((1, NSA_REP, HEAD_DIM), lambda bi, g, i: (g, 0, 0)),
        pl.BlockSpec(ovl_t.shape, lambda bi, g, i: (0, 0)),
        pl.BlockSpec(kext.shape, lambda bi, g, i: (0, 0)),
        pl.BlockSpec((1, NSA_REP, LANES), lambda bi, g, i: (g, 0, 0)),
        pl.BlockSpec((1,) + gsel.shape[1:], lambda bi, g, i: (g, 0, 0)),
    ] + cast_specs
    aug = pltpu.VMEM((s, 2 * HEAD_DIM), BF16)
    return pl.pallas_call(
        functools.partial(_nsa_kernel, n_cast=len(cast_arrays)),
        grid=(b, NSA_GROUPS, n_q),
        in_specs=in_specs,
        out_specs=[pl.BlockSpec((1, NSA_TQ, NSA_REP * HEAD_DIM), lambda bi, g, i: (bi, i, g))] + cast_specs,
        out_shape=[jax.ShapeDtypeStruct((b, s, NSA_WIDTH), BF16)] + cast_shapes,
        scratch_shapes=[aug, aug, aug, pltpu.VMEM((NSA_REP * NSA_TQ, s), F32),
                        pltpu.VMEM((NSA_REP * NSA_TQ, HEAD_DIM), F32)],
        compiler_params=_params(3),
        name="nsa_attention",
    )(slopes, proj_a, cmp_kv[0], cmp_kv[1], proj_a, proj_a, proj_a, proj_a, gates3, gain, ovl_t,
      kext, qext, gsel, *cast_arrays)


def _diff_kernel(lq1_ref, lk1_ref, lq2_ref, lk2_ref, q_ref, k_ref, v_ref, g_ref, kext_ref, qext_ref,
                 *rest, n_cast, lambda_init):
    cast_src, o_ref, cast_dst = rest[:n_cast], rest[n_cast], rest[n_cast + 1:2 * n_cast + 1]
    k_aug, s_ref, acc_ref = rest[2 * n_cast + 1:]
    _cast_blocks(cast_src, cast_dst)
    i = pl.program_id(2)
    tq, ck = DIFF_TQ, DIFF_CK
    t0 = i * tq

    @pl.when(i == 0)
    def _():
        for mi in range(2):
            k_aug[mi, :, 0:DIFF_DIM] = k_ref[0, :, mi * DIFF_DIM:(mi + 1) * DIFF_DIM]
            k_aug[mi, :, DIFF_DIM:] = kext_ref[...]

    q = q_ref[0]
    qext = jnp.broadcast_to(qext_ref[0], (tq, LANES)).astype(BF16)
    q_aug = [jnp.concatenate([q[:, mi * DIFF_DIM:(mi + 1) * DIFF_DIM], qext], axis=1) for mi in range(2)]
    t_col = t0 + lax.broadcasted_iota(jnp.int32, (tq, 1), 0)

    def branch(n):
        for mi in range(2):
            mx = jnp.full((tq, LANES), NEG, F32)
            for c in range(n):
                cols = slice(c * ck, (c + 1) * ck)
                t = _nt_dot(q_aug[mi], k_aug[mi, cols, :])
                if c == n - 1:
                    pos = c * ck + lax.broadcasted_iota(jnp.int32, (1, ck), 1)
                    t = jnp.where(pos <= t_col, t, NEG)
                s_ref[mi, :, cols] = t
                mx = jnp.maximum(mx, _lane_fold(t, jnp.maximum))
            m_row = jnp.max(mx, axis=-1, keepdims=True)
            l_fold = jnp.zeros((tq, LANES), F32)
            acc = jnp.zeros((tq, DIFF_VDIM), F32)
            for c in range(n):
                cols = slice(c * ck, (c + 1) * ck)
                p = jnp.exp2(s_ref[mi, :, cols] - m_row)
                l_fold = l_fold + _lane_fold(p, jnp.add)
                acc = acc + _dot_row_halves(p.astype(BF16), v_ref[0, cols, :])
            acc_ref[mi] = acc * (1.0 / jnp.sum(l_fold, axis=-1, keepdims=True))

    for n in range(1, k_aug.shape[1] // ck + 1):
        pl.when(i + 1 == n)(functools.partial(branch, n))

    lam = (jnp.exp(jnp.sum(lq1_ref[...] * lk1_ref[...], axis=-1, keepdims=True))
           - jnp.exp(jnp.sum(lq2_ref[...] * lk2_ref[...], axis=-1, keepdims=True))
           + lambda_init)
    o = acc_ref[0] - lam * acc_ref[1]
    o = o * lax.rsqrt(jnp.mean(o * o, axis=-1, keepdims=True) + RMS_EPS) * g_ref[...]
    o_ref[0] = (o * (1.0 - lambda_init)).astype(o_ref.dtype)


def _diff_attention(proj_b, lq1, lk1, lq2, lk2, gain, kext, qext, lambda_init, cast_arrays):
    b, s, _ = proj_b.shape
    n_q = s // DIFF_TQ
    cast_specs, cast_shapes = _cast_specs(cast_arrays, b * DIFF_HEADS * n_q,
                                          lambda bi, h, i: (bi * DIFF_HEADS + h) * n_q + i)
    qcol, kcol, vcol = 0, DIFF_HEADS, 2 * DIFF_HEADS
    vec = pl.BlockSpec((1, DIFF_DIM), lambda bi, h, i: (0, 0))
    in_specs = [
        vec, vec, vec, vec,
        pl.BlockSpec((1, DIFF_TQ, DIFF_VDIM), lambda bi, h, i: (bi, i, qcol + h)),
        pl.BlockSpec((1, s, DIFF_VDIM), lambda bi, h, i: (bi, 0, kcol + h)),
        pl.BlockSpec((1, s, DIFF_VDIM), lambda bi, h, i: (bi, 0, vcol + h)),
        pl.BlockSpec((1, DIFF_VDIM), lambda bi, h, i: (0, 0)),
        pl.BlockSpec(kext.shape, lambda bi, h, i: (0, 0)),
        pl.BlockSpec((1, 1, LANES), lambda bi, h, i: (h, 0, 0)),
    ] + cast_specs
    return pl.pallas_call(
        functools.partial(_diff_kernel, n_cast=len(cast_arrays), lambda_init=lambda_init),
        grid=(b, DIFF_HEADS, n_q),
        in_specs=in_specs,
        out_specs=[pl.BlockSpec((1, DIFF_TQ, DIFF_VDIM), lambda bi, h, i: (bi, i, h))] + cast_specs,
        out_shape=[jax.ShapeDtypeStruct((b, s, DIFF_WIDTH), BF16)] + cast_shapes,
        scratch_shapes=[pltpu.VMEM((2, s, 2 * DIFF_DIM), BF16), pltpu.VMEM((2, DIFF_TQ, s), F32),
                        pltpu.VMEM((2, DIFF_TQ, DIFF_VDIM), F32)],
        compiler_params=_params(3),
        name="diff_attention",
    )(lq1.reshape(1, -1), lk1.reshape(1, -1), lq2.reshape(1, -1), lk2.reshape(1, -1),
      proj_b, proj_b, proj_b, gain.reshape(1, -1), kext, qext, *cast_arrays)


def _alibi_slopes(n_heads):
    return np.array([2.0 ** (-8.0 * (h + 1) / n_heads) for h in range(n_heads)], np.float32)


def _overlap_t(seq):
    n_cmp = seq // CMP_STRIDE
    n_slc = seq // SLC_BLOCK
    cs = np.arange(n_cmp) * CMP_STRIDE
    ss = np.arange(n_slc) * SLC_BLOCK
    ov = np.clip(np.minimum(cs[:, None] + CMP_BLOCK, ss[None, :] + SLC_BLOCK)
                 - np.maximum(cs[:, None], ss[None, :]), 0, None)
    ov[(seq - CMP_BLOCK) // CMP_STRIDE + 1:, :] = 0
    return np.ascontiguousarray((ov / CMP_BLOCK).astype(np.float32).T)


def _round_to_bf16(x):
    bits = np.array(x, np.float32).view(np.uint32)
    bits = (bits + np.uint32(0x7FFF) + ((bits >> np.uint32(16)) & np.uint32(1))) & np.uint32(0xFFFF0000)
    return np.float64(bits.view(np.float32))


def _bf16_pieces(x, n):
    out, rem = [], np.float64(x)
    for _ in range(n):
        piece = _round_to_bf16(rem)
        out.append(piece)
        rem = rem - piece
    return out


def _key_ext(seq):
    ext = np.zeros((seq, LANES), np.float32)
    pos = np.arange(seq)
    ext[pos, pos // SLC_BLOCK] = 1.0
    centred = pos - seq // 2
    hi = (centred // 256) * 256
    lo = centred - hi
    n_slc = seq // SLC_BLOCK
    ext[:, n_slc:n_slc + SLOPE_PIECES] = hi[:, None]
    ext[:, n_slc + SLOPE_PIECES:n_slc + 2 * SLOPE_PIECES] = lo[:, None]
    return ext


def _query_ext(slopes, n_slc):
    ext = np.zeros((len(slopes), LANES), np.float32)
    for h, slope in enumerate(slopes):
        pieces = _bf16_pieces(np.float64(slope) * LOG2E, SLOPE_PIECES)
        ext[h, n_slc:n_slc + SLOPE_PIECES] = pieces
        ext[h, n_slc + SLOPE_PIECES:n_slc + 2 * SLOPE_PIECES] = pieces
    return ext


def _gate_select():
    per = 3 * NSA_REP
    sel = np.zeros((NSA_GROUPS, LANES, per * LANES), np.float32)
    for g in range(NSA_GROUPS):
        for k in range(per):
            sel[g, g * per + k, k * LANES:(k + 1) * LANES] = 1.0
    return sel


def _layer(x, w_in, cmp_pe_k, cmp_w1_k, cmp_w2_k, cmp_pe_v, cmp_w1_v, cmp_w2_v, nsa_out_g,
           lambda_q1, lambda_k1, lambda_q2, lambda_k2, diff_subln_g, w_out, ln1_g, ln1_b,
           w_ff1, w_ff2, ln2_g, ln2_b, lambda_init):
    b, s, d = x.shape
    t = b * s
    x2 = x.reshape(t, d)

    w_t = jnp.swapaxes(w_in, 0, 1)
    x_bf, gates = _gate_cast(x2, w_t, row0=GATE_START)
    proj_a, chunks = _inproj(x_bf, w_t, BF16, row0=0, n=PROJ_A_COLS, scaled_cols=NSA_WIDTH,
                             chunked=(NSA_WIDTH, 2, s), name="in_proj_nsa")
    proj_b = _inproj(x_bf, w_t, BF16, row0=GATE_START, n=PROJ_B_COLS, scaled_cols=DIFF_WIDTH,
                     shift=GATE_COLS, name="in_proj_diff")
    proj_a = proj_a.reshape(b, s, PROJ_A_COLS)
    proj_b = proj_b.reshape(b, s, PROJ_B_COLS)

    pe = jnp.stack([cmp_pe_k, cmp_pe_v]).reshape(2, 1, CMP_BLOCK * HEAD_DIM).astype(BF16)
    w1 = jnp.stack([cmp_w1_k, cmp_w1_v]).astype(BF16)
    w2 = jnp.stack([cmp_w2_k, cmp_w2_v]).astype(BF16)
    cmp_kv = _compress(chunks, pe, w1, w2)

    n_slc = s // SLC_BLOCK
    kext = jnp.asarray(_key_ext(s)).astype(BF16)
    o_nsa, w_ff1_bf, w_out_bf = _nsa_attention(
        proj_a, cmp_kv, gates.reshape(b, s, LANES),
        nsa_out_g.reshape(NSA_GROUPS, NSA_REP, HEAD_DIM),
        jnp.asarray(_alibi_slopes(NSA_HEADS)), jnp.asarray(_overlap_t(s)), kext,
        jnp.asarray(_query_ext(_alibi_slopes(NSA_HEADS), n_slc)).reshape(NSA_GROUPS, NSA_REP, LANES),
        jnp.asarray(_gate_select()).astype(BF16), [w_ff1, w_out])
    o_diff, w_ff2_bf = _diff_attention(
        proj_b, lambda_q1, lambda_k1, lambda_q2, lambda_k2, diff_subln_g, kext,
        jnp.asarray(_query_ext(_alibi_slopes(DIFF_HEADS), n_slc)).reshape(DIFF_HEADS, 1, LANES),
        lambda_init, [w_ff2])

    h1 = _outproj(o_nsa.reshape(t, NSA_WIDTH), o_diff.reshape(t, DIFF_WIDTH), w_out_bf, x2)
    x1, x1_bf = _layer_norm(h1, ln1_g, ln1_b, (F32, BF16))
    hid = _ff1(x1_bf, w_ff1_bf)
    h2 = _ff2(hid, w_ff2_bf, x1)
    (out,) = _layer_norm(h2, ln2_g, ln2_b, (F32,))
    return out.reshape(b, s, d)


def kernel(x, w_in, cmp_pe_k, cmp_w1_k, cmp_w2_k, cmp_pe_v, cmp_w1_v, cmp_w2_v, nsa_out_g,
           lambda_q1, lambda_k1, lambda_q2, lambda_k2, diff_subln_g, w_out, ln1_g, ln1_b,
           w_ff1, w_ff2, ln2_g, ln2_b):
    for l in range(DEPTH):
        lambda_init = 0.8 - 0.6 * math.exp(-0.3 * l)
        x = _layer(x, w_in[l], cmp_pe_k[l], cmp_w1_k[l], cmp_w2_k[l], cmp_pe_v[l], cmp_w1_v[l],
                   cmp_w2_v[l], nsa_out_g[l], lambda_q1[l], lambda_k1[l], lambda_q2[l],
                   lambda_k2[l], diff_subln_g[l], w_out[l], ln1_g[l], ln1_b[l], w_ff1[l],
                   w_ff2[l], ln2_g[l], ln2_b[l], lambda_init)
    return x
```

```python
import functools
import math

import numpy as np
import jax
import jax.numpy as jnp
from jax import lax
from jax.experimental import pallas as pl
from jax.experimental.pallas import tpu as pltpu

F32 = jnp.float32
BF16 = jnp.bfloat16

D_MODEL = 4096
HEAD_DIM = 128
NSA_HEADS = 16
NSA_GROUPS = 4
NSA_REP = 4
CMP_BLOCK = 32
CMP_STRIDE = 16
CMP_HIDDEN = 256
SLC_BLOCK = 64
SLC_TOPK = 16
WINDOW = 512
DIFF_HEADS = 8
DIFF_DIM = 128
DIFF_VDIM = 256
NSA_WIDTH = NSA_HEADS * HEAD_DIM
KV_WIDTH = 6 * NSA_GROUPS * HEAD_DIM
GATE_COLS = 3 * NSA_HEADS
GATE_START = NSA_WIDTH + KV_WIDTH
DIFF_WIDTH = DIFF_HEADS * DIFF_VDIM
PROJ_A_COLS = GATE_START
PROJ_B_COLS = 3 * DIFF_WIDTH
D_FF = 4 * D_MODEL
LN_EPS = 1e-5
RMS_EPS = 1e-6
NEG = -1e30
DEPTH = 1
DEEPNORM_ALPHA = (2.0 * DEPTH) ** 0.25
ATT_SCALE = HEAD_DIM ** -0.5
LOG2E = 1.4426950408889634
ATT_SCALE2 = ATT_SCALE * LOG2E

LANES = 128
BF16_SUBLANES = 16
VMEM_LIMIT_BYTES = 56 * 1024 * 1024

MM_BM = 1024
MM_BN = 1024
INPROJ_BN = 512
OUTPROJ_BN = 512
GATE_BM = 512
FF2_BK = 2048
CAST_COLS = 512
LN_ROWS = 256
NSA_TQ = 256
SLC_CK = 512
WIN_SPAN = WINDOW + NSA_TQ
DIFF_TQ = 512
DIFF_CK = 512
SLOPE_PIECES = 4


def _params(n_axes):
    return pltpu.CompilerParams(dimension_semantics=("arbitrary",) * n_axes,
                                vmem_limit_bytes=VMEM_LIMIT_BYTES)


def _load_weight_rows(b_ref, tail_ref, shift, cols):
    if shift:
        return jnp.concatenate([b_ref[shift:, cols], tail_ref[0:shift, cols]], axis=0)
    return b_ref[:, cols]


def _inproj_kernel(*refs, shift, n_inner, scaled_tiles, chunked_tiles):
    refs = list(refs)
    a_ref, b_ref = refs[:2]
    tail_ref = refs[2] if shift else None
    o_ref = refs[3 if shift else 2]
    wb_ref = refs[-1]
    j = pl.program_id(0)
    i = pl.program_id(1)

    def prepare_slice():
        width = wb_ref.shape[1] // n_inner
        cols = pl.ds(pl.multiple_of(i * width, width), width)
        blk = _load_weight_rows(b_ref, tail_ref, shift, cols)
        blk = blk * jnp.where(j < scaled_tiles, ATT_SCALE2, 1.0)
        wb_ref[j % 2, cols, :] = blk.T.astype(BF16)

    pl.when(j == 0)(prepare_slice)

    @pl.when(j > 0)
    def _():
        prepare_slice()
        acc = jnp.dot(a_ref[...], wb_ref[(j + 1) % 2], preferred_element_type=F32)
        o_ref[...] = acc.astype(o_ref.dtype)
        if chunked_tiles:
            for grp in range(refs[-2].shape[0]):
                refs[-2][grp] = acc[:, grp * HEAD_DIM:(grp + 1) * HEAD_DIM]

    if chunked_tiles:
        first, n_ch = chunked_tiles
        c_ref, stage_ref = refs[-3], refs[-2]

        @pl.when((j - 1 >= first) & (j - 1 < first + n_ch))
        def _():
            n_groups, n_rows = c_ref.shape[2], c_ref.shape[3]
            for grp in range(n_groups):
                for l in range(CMP_STRIDE):
                    c_ref[0, 0, grp, :, l * HEAD_DIM:(l + 1) * HEAD_DIM] = stage_ref[
                        grp, pl.ds(l, n_rows, stride=CMP_STRIDE), :].astype(c_ref.dtype)


def _inproj(a, wt, out_dtype, *, row0, n, scaled_cols, shift=0, chunked=None, bn=INPROJ_BN,
            bm=MM_BM, name):
    m, k = a.shape
    n_tiles = n // bn
    n_inner = m // bm
    assert k % ((m // bm) * LANES) == 0
    jb = row0 // bn

    def tile(j):
        return jnp.minimum(j, n_tiles - 1)

    def row_tile(j, i):
        return jnp.where(j == 0, 0, i)

    in_specs = [pl.BlockSpec((bm, k), lambda j, i: (row_tile(j, i), 0)),
                pl.BlockSpec((bn, k), lambda j, i: (jb + tile(j), 0))]
    args = [a, wt]
    if shift:
        per = bn // LANES
        tb = row0 // LANES
        in_specs.append(pl.BlockSpec((LANES, k), lambda j, i: (tb + per * (tile(j) + 1), 0)))
        args.append(wt)
    out_specs = [pl.BlockSpec((bm, bn), lambda j, i: (row_tile(j, i), jnp.maximum(j - 1, 0)))]
    out_shape = [jax.ShapeDtypeStruct((m, n), out_dtype)]
    scratch = [pltpu.VMEM((2, k, bn), BF16)]
    chunked_tiles = None
    if chunked:
        col0, n_kinds, seq = chunked
        first = col0 // bn
        per_seq = seq // bm
        n_rows = bm // CMP_STRIDE
        chunked_tiles = (first, n_kinds)

        def chunk_block(j, i):
            t = j - 1
            ii = jnp.where(t < first, 0, jnp.where(t >= first + n_kinds, n_inner - 1, i))
            return (jnp.clip(t - first, 0, n_kinds - 1), ii // per_seq, 0, ii % per_seq, 0)

        out_specs.append(pl.BlockSpec((1, 1, bn // HEAD_DIM, n_rows, CMP_STRIDE * HEAD_DIM), chunk_block))
        out_shape.append(jax.ShapeDtypeStruct(
            (n_kinds, m // seq, bn // HEAD_DIM, seq // CMP_STRIDE, CMP_STRIDE * HEAD_DIM), out_dtype))
        scratch.insert(0, pltpu.VMEM((bn // HEAD_DIM, bm, HEAD_DIM), F32))
    res = pl.pallas_call(
        functools.partial(_inproj_kernel, shift=shift, n_inner=n_inner,
                          scaled_tiles=scaled_cols // bn, chunked_tiles=chunked_tiles),
        grid=(n_tiles + 1, n_inner),
        in_specs=in_specs,
        out_specs=out_specs,
        out_shape=out_shape,
        scratch_shapes=scratch,
        compiler_params=_params(2),
        name=name,
    )(*args)
    return res if chunked else res[0]


def _gate_cast_kernel(x_ref, wt_ref, xb_ref, g_ref, wb_ref):
    @pl.when(pl.program_id(0) == 0)
    def _():
        k = wb_ref.shape[0]
        for c0 in range(0, k, CAST_COLS):
            wb_ref[c0:c0 + CAST_COLS, :] = wt_ref[:, c0:c0 + CAST_COLS].T.astype(BF16)

    xb = x_ref[...].astype(BF16)
    xb_ref[...] = xb
    g_ref[...] = jnp.dot(xb, wb_ref[...], preferred_element_type=F32)


def _gate_cast(x, wt, *, row0, bm=GATE_BM):
    m, k = x.shape
    return pl.pallas_call(
        _gate_cast_kernel,
        grid=(m // bm,),
        in_specs=[pl.BlockSpec((bm, k), lambda i: (i, 0)),
                  pl.BlockSpec((LANES, k), lambda i: (row0 // LANES, 0))],
        out_specs=[pl.BlockSpec((bm, k), lambda i: (i, 0)),
                   pl.BlockSpec((bm, LANES), lambda i: (i, 0))],
        out_shape=[jax.ShapeDtypeStruct((m, k), BF16), jax.ShapeDtypeStruct((m, LANES), F32)],
        scratch_shapes=[pltpu.VMEM((k, LANES), BF16)],
        compiler_params=_params(1),
        name="gate_proj_cast",
    )(x, wt)


def _mm_kernel(a_ref, b_ref, o_ref):
    acc = jnp.dot(a_ref[...], b_ref[...], preferred_element_type=F32)
    o_ref[...] = jnp.square(jnp.maximum(acc, 0.0)).astype(o_ref.dtype)


def _ff1(a, b, *, bm=MM_BM, bn=MM_BN):
    m, k = a.shape
    _, n = b.shape
    return pl.pallas_call(
        _mm_kernel,
        grid=(m // bm, n // bn),
        in_specs=[pl.BlockSpec((bm, k), lambda i, j: (i, 0)),
                  pl.BlockSpec((k, bn), lambda i, j: (0, j))],
        out_specs=pl.BlockSpec((bm, bn), lambda i, j: (i, j)),
        out_shape=jax.ShapeDtypeStruct((m, n), BF16),
        compiler_params=_params(2),
        name="ff1",
    )(a, b)


def _outproj_kernel(a1_ref, a2_ref, b_ref, res_ref, o_ref):
    half = a1_ref.shape[1]
    acc = jnp.dot(a1_ref[...], b_ref[0:half, :], preferred_element_type=F32)
    acc = acc + jnp.dot(a2_ref[...], b_ref[half:2 * half, :], preferred_element_type=F32)
    o_ref[...] = DEEPNORM_ALPHA * res_ref[...] + acc


def _outproj(a1, a2, b, res, *, bm=MM_BM, bn=OUTPROJ_BN):
    m, k1 = a1.shape
    _, n = b.shape
    return pl.pallas_call(
        _outproj_kernel,
        grid=(m // bm, n // bn),
        in_specs=[pl.BlockSpec((bm, k1), lambda i, j: (i, 0)),
                  pl.BlockSpec((bm, k1), lambda i, j: (i, 0)),
                  pl.BlockSpec((2 * k1, bn), lambda i, j: (0, j)),
                  pl.BlockSpec((bm, bn), lambda i, j: (i, j))],
        out_specs=pl.BlockSpec((bm, bn), lambda i, j: (i, j)),
        out_shape=jax.ShapeDtypeStruct((m, n), F32),
        compiler_params=_params(2),
        name="outproj",
    )(a1, a2, b, res)


def _ff2_kernel(a_ref, b_ref, res_ref, o_ref, acc_ref):
    kk = pl.program_id(2)

    @pl.when(kk == 0)
    def _():
        acc_ref[...] = jnp.zeros_like(acc_ref)

    acc_ref[...] += jnp.dot(a_ref[...], b_ref[...], preferred_element_type=F32)

    @pl.when(kk == pl.num_programs(2) - 1)
    def _():
        o_ref[...] = DEEPNORM_ALPHA * res_ref[...] + acc_ref[...]


def _ff2(a, b, res, *, bm=MM_BM, bn=MM_BN, bk=FF2_BK):
    m, k = a.shape
    _, n = b.shape
    return pl.pallas_call(
        _ff2_kernel,
        grid=(m // bm, n // bn, k // bk),
        in_specs=[pl.BlockSpec((bm, bk), lambda i, j, kk: (i, kk)),
                  pl.BlockSpec((bk, bn), lambda i, j, kk: (kk, j)),
                  pl.BlockSpec((bm, bn), lambda i, j, kk: (i, j))],
        out_specs=pl.BlockSpec((bm, bn), lambda i, j, kk: (i, j)),
        out_shape=jax.ShapeDtypeStruct((m, n), F32),
        scratch_shapes=[pltpu.VMEM((bm, bn), F32)],
        compiler_params=_params(3),
        name="ff2",
    )(a, b, res)


def _cast_blocks(src_refs, dst_refs):
    for src, dst in zip(src_refs, dst_refs):
        dst[...] = src[...].astype(dst.dtype)


def _cast_specs(arrays, n_steps, step_of):
    specs, shapes = [], []
    for arr in arrays:
        rows, cols = arr.shape
        assert rows % (n_steps * BF16_SUBLANES) == 0
        blk = (rows // n_steps, cols)
        specs.append(pl.BlockSpec(blk, lambda *ids: (step_of(*ids), 0)))
        shapes.append(jax.ShapeDtypeStruct((rows, cols), BF16))
    return specs, shapes


def _ln_kernel(x_ref, g_ref, b_ref, *o_refs):
    x = x_ref[...]
    mu = jnp.mean(x, axis=-1, keepdims=True)
    xc = x - mu
    var = jnp.mean(xc * xc, axis=-1, keepdims=True)
    y = xc * lax.rsqrt(var + LN_EPS) * g_ref[...] + b_ref[...]
    for o_ref in o_refs:
        o_ref[...] = y.astype(o_ref.dtype)


def _layer_norm(x, g, b, out_dtypes):
    m, d = x.shape
    row_spec = pl.BlockSpec((LN_ROWS, d), lambda i: (i, 0))
    vec_spec = pl.BlockSpec((1, d), lambda i: (0, 0))
    return pl.pallas_call(
        _ln_kernel,
        grid=(m // LN_ROWS,),
        in_specs=[row_spec, vec_spec, vec_spec],
        out_specs=[row_spec for _ in out_dtypes],
        out_shape=[jax.ShapeDtypeStruct((m, d), dt) for dt in out_dtypes],
        compiler_params=_params(1),
        name="layer_norm",
    )(x, g.reshape(1, d), b.reshape(1, d))


def _gelu_tanh(x):
    c = math.sqrt(2.0 / math.pi)
    return 0.5 * x * (1.0 + jnp.tanh(c * (x + 0.044715 * (x * x * x))))


def _compress_kernel(c_ref, pe_ref, w1_ref, w2_ref, o_ref):
    half = CMP_STRIDE * HEAD_DIM
    c = c_ref[0, 0, 0]
    n_chunks = c.shape[0]
    y1 = jnp.dot(c, w1_ref[0, 0:half, :], preferred_element_type=F32)
    y2 = jnp.dot(c, w1_ref[0, half:2 * half, :], preferred_element_type=F32)
    pe8 = jnp.broadcast_to(pe_ref[0], (8, 2 * half))
    pb = jnp.dot(pe8, w1_ref[0], preferred_element_type=F32)[0:1, :]
    h = y1 + pltpu.roll(y2, n_chunks - 1, 0) + pb
    a = _gelu_tanh(h).astype(BF16)
    out = jnp.dot(a, w2_ref[0], preferred_element_type=F32)
    row = lax.broadcasted_iota(jnp.int32, out.shape, 0)
    out = jnp.where(row < n_chunks - 1, out, 0.0)
    o_ref[0, 0, 0] = out.astype(o_ref.dtype)


def _compress(chunks, pe, w1, w2):
    _, b, g, n_chunks, width = chunks.shape
    return pl.pallas_call(
        _compress_kernel,
        grid=(2, b, g),
        in_specs=[pl.BlockSpec((1, 1, 1, n_chunks, width), lambda t, i, j: (t, i, j, 0, 0)),
                  pl.BlockSpec((1, 1, 2 * width), lambda t, i, j: (t, 0, 0)),
                  pl.BlockSpec((1, 2 * width, CMP_HIDDEN), lambda t, i, j: (t, 0, 0)),
                  pl.BlockSpec((1, CMP_HIDDEN, HEAD_DIM), lambda t, i, j: (t, 0, 0))],
        out_specs=pl.BlockSpec((1, 1, 1, n_chunks, HEAD_DIM), lambda t, i, j: (t, i, j, 0, 0)),
        out_shape=jax.ShapeDtypeStruct((2, b, g, n_chunks, HEAD_DIM), BF16),
        compiler_params=_params(3),
        name="compress",
    )(chunks, pe, w1, w2)


def _nt_dot(a, b):
    return lax.dot_general(a, b, (((1,), (1,)), ((), ())), preferred_element_type=F32)


def _dot_row_halves(a, b):
    half = a.shape[0] // 2
    return jnp.concatenate([jnp.dot(a[:half], b, preferred_element_type=F32),
                            jnp.dot(a[half:], b, preferred_element_type=F32)], axis=0)


def _lane_fold(x, op):
    out = x[:, 0:LANES]
    for c0 in range(LANES, x.shape[1], LANES):
        out = op(out, x[:, c0:c0 + LANES])
    return out


def _nsa_kernel(slopes_ref, q_ref, kc_ref, vc_ref, ks_ref, vs_ref, kw_ref, vw_ref,
                gate_ref, gain_ref, kext_ref, qext_ref, gsel_ref, *rest, n_cast):
    cast_src, o_ref, cast_dst = rest[:n_cast], rest[n_cast], rest[n_cast + 1:2 * n_cast + 1]
    ks_aug, vs_aug, vw_aug, s_ref, oslc_ref, psum_ref = rest[2 * n_cast + 1:]
    _cast_blocks(cast_src, cast_dst)
    g = pl.program_id(1)
    i = pl.program_id(2)
    tq = NSA_TQ
    rows = NSA_REP * tq
    t0 = i * tq

    @pl.when(i == 0)
    def _():
        ks_aug[:, 0:HEAD_DIM] = ks_ref[0]
        ks_aug[:, HEAD_DIM:] = kext_ref[...]
        ones = jnp.ones((ks_aug.shape[0], HEAD_DIM), BF16)
        vs_aug[:, 0:HEAD_DIM] = vs_ref[0]
        vs_aug[:, HEAD_DIM:] = ones
        vw_aug[:, 0:HEAD_DIM] = vw_ref[0]
        vw_aug[:, HEAD_DIM:] = ones

    q = q_ref[0]
    q_heads = [q[:, r * HEAD_DIM:(r + 1) * HEAD_DIM] for r in range(NSA_REP)]
    qs = jnp.concatenate(q_heads, axis=0)
    t_col = t0 + lax.broadcasted_iota(jnp.int32, (tq, 1), 0)
    head = [slice(r * tq, (r + 1) * tq) for r in range(NSA_REP)]

    n_cmp = kc_ref.shape[2]
    s_t = _nt_dot(kc_ref[0, 0], qs)
    n_col = lax.broadcasted_iota(jnp.int32, (n_cmp, 1), 0)
    t_row = t0 + (lax.broadcasted_iota(jnp.int32, (1, rows), 1) & (tq - 1))
    cmask = (n_col * CMP_STRIDE + (CMP_BLOCK - 1)) <= t_row
    s_t = jnp.where(cmask, s_t, NEG)
    m = jnp.max(s_t, axis=0, keepdims=True)
    e = jnp.where(cmask, jnp.exp2(s_t - m), 0.0)
    l = jnp.sum(e, axis=0, keepdims=True)
    p_t = e * (1.0 / jnp.maximum(l, 1e-30))
    p_sum_t = p_t[:, head[0]]
    for r in range(1, NSA_REP):
        p_sum_t = p_sum_t + p_t[:, head[r]]
    o_cmp = lax.dot_general(p_t.astype(BF16), vc_ref[0, 0], (((0,), (0,)), ((), ())),
                            preferred_element_type=F32)

    n_slc = n_cmp * CMP_STRIDE // SLC_BLOCK
    per = SLC_BLOCK // CMP_STRIDE
    for hl in range(tq // LANES):
        psum_ref[hl] = p_sum_t[:, hl * LANES:(hl + 1) * LANES]
    rows_k = [jnp.concatenate([psum_ref[hl, pl.ds(k, n_slc, stride=per), :] for hl in range(tq // LANES)],
                              axis=1) for k in range(per)]
    j_t = lax.broadcasted_iota(jnp.int32, (n_slc, tq), 0)
    prev_last = jnp.where(j_t == 0, 0.0, pltpu.roll(rows_k[per - 1], 1, 0))
    imp_t = rows_k[0] + rows_k[1] + rows_k[2] + 0.5 * (rows_k[3] + prev_last)
    t_t = t0 + lax.broadcasted_iota(jnp.int32, (n_slc, tq), 1)
    causal = j_t * SLC_BLOCK <= t_t
    tb = lax.shift_right_logical(t_t, int(math.log2(SLC_BLOCK)))
    forced = jnp.where(j_t == 0, 1e6, jnp.where(j_t == tb, 1e6, jnp.where(j_t == tb - 1, 1e6, imp_t)))
    score = jnp.where(causal, forced, NEG)
    rank = jnp.zeros((n_slc, tq), F32)
    for jp in range(n_slc):
        row = score[jp:jp + 1, :]
        ge = jnp.where(row >= score, 1.0, 0.0)
        gt = jnp.where(row > score, 1.0, 0.0)
        rank = rank + jnp.where(j_t > jp, ge, gt)
    sel_bias = jnp.where(causal, jnp.where(rank < SLC_TOPK, 0.0, NEG), NEG)

    sel_rows = jnp.concatenate([sel_bias, jnp.zeros((LANES - n_slc, tq), F32)], axis=0).T
    q_aug = jnp.concatenate(
        [jnp.concatenate([q_heads[r], (sel_rows + qext_ref[0, r:r + 1, :]).astype(BF16)], axis=1)
         for r in range(NSA_REP)], axis=0)
    ck = SLC_CK
    n_chunks = (t0 + tq + ck - 1) // ck

    def slc_branch(n):
        mx = jnp.full((rows, LANES), NEG, F32)
        for c in range(n):
            cols = slice(c * ck, (c + 1) * ck)
            t = _nt_dot(q_aug, ks_aug[cols, :])
            if c == n - 1:
                pos = c * ck + lax.broadcasted_iota(jnp.int32, (1, ck), 1)
                cb = jnp.where(pos <= t_col, 0.0, NEG)
                t = t + jnp.concatenate([cb] * NSA_REP, axis=0)
            s_ref[:, cols] = t
            mx = jnp.maximum(mx, _lane_fold(t, jnp.maximum))
        m_slc = jnp.max(mx, axis=-1, keepdims=True)
        acc = jnp.zeros((rows, 2 * HEAD_DIM), F32)
        for c in range(n):
            cols = slice(c * ck, (c + 1) * ck)
            p = jnp.exp2(s_ref[:, cols] - m_slc).astype(BF16)
            acc = acc + _dot_row_halves(p, vs_aug[cols, :])
        oslc_ref[...] = acc[:, 0:HEAD_DIM] * (1.0 / acc[:, HEAD_DIM:])

    for n in range(1, ks_aug.shape[0] // ck + 1):
        pl.when(n_chunks == n)(functools.partial(slc_branch, n))
    o_slc = oslc_ref[...]

    slope2 = [slopes_ref[g * NSA_REP + r] * LOG2E for r in range(NSA_REP)]
    k0 = pl.multiple_of(jnp.maximum(t0 - WINDOW, 0), tq)
    pos = k0 + lax.broadcasted_iota(jnp.int32, (1, WIN_SPAN), 1)
    dist = t_col - pos
    wbias = jnp.where(dist >= 0, jnp.where(dist < WINDOW, 0.0, NEG), NEG)
    posrel = (pos - t0).astype(F32)
    kband = kw_ref[0, pl.ds(k0, WIN_SPAN), :]
    vband = vw_aug[pl.ds(k0, WIN_SPAN), :]
    halves = []
    for pair in range(NSA_REP // 2):
        ps = []
        for r in (2 * pair, 2 * pair + 1):
            t = _nt_dot(q_heads[r], kband) + slope2[r] * posrel + wbias
            m = jnp.max(t, axis=-1, keepdims=True)
            ps.append(jnp.exp2(t - m).astype(BF16))
        halves.append(jnp.dot(jnp.concatenate(ps, axis=0), vband, preferred_element_type=F32))
    acc_w = jnp.concatenate(halves, axis=0)
    o_win = acc_w[:, 0:HEAD_DIM] * (1.0 / acc_w[:, HEAD_DIM:])

    sig = jax.nn.sigmoid(gate_ref[0]).astype(BF16)
    gates = jnp.dot(sig, gsel_ref[0], preferred_element_type=F32)
    for r in range(NSA_REP):
        gt = [gates[:, (3 * r + br) * LANES:(3 * r + br + 1) * LANES] for br in range(3)]
        o = gt[0] * o_cmp[head[r]] + gt[1] * o_slc[head[r]] + gt[2] * o_win[head[r]]
        o = o * lax.rsqrt(jnp.mean(o * o, axis=-1, keepdims=True) + RMS_EPS) * gain_ref[0, r:r + 1, :]
        o_ref[0, :, r * HEAD_DIM:(r + 1) * HEAD_DIM] = o.astype(o_ref.dtype)


def _nsa_attention(proj_a, cmp_kv, gates3, gain, slopes, kext, qext, gsel, cast_arrays):
    b, s, _ = proj_a.shape
    n_chunks = cmp_kv.shape[3]
    n_q = s // NSA_TQ
    cast_specs, cast_shapes = _cast_specs(cast_arrays, b * NSA_GROUPS * n_q,
                                          lambda bi, g, i: (bi * NSA_GROUPS + g) * n_q + i)
    col0 = NSA_WIDTH // HEAD_DIM

    def kv_spec(idx):
        return pl.BlockSpec((1, s, HEAD_DIM), lambda bi, g, i, idx=idx: (bi, 0, col0 + idx * NSA_GROUPS + g))

    cmp_spec = pl.BlockSpec((1, 1, n_chunks, HEAD_DIM), lambda bi, g, i: (bi, g, 0, 0))
    in_specs = [
        pl.BlockSpec(memory_space=pltpu.SMEM),
        pl.BlockSpec((1, NSA_TQ, NSA_REP * HEAD_DIM), lambda bi, g, i: (bi, i, g)),
        cmp_spec, cmp_spec,
        kv_spec(2), kv_spec(3), kv_spec(4), kv_spec(5),
        pl.BlockSpec((1, NSA_TQ, LANES), lambda bi, g, i: (bi, i, 0)),
        pl.BlockSpec((1, NSA_REP, HEAD_DIM), lambda bi, g, i: (g, 0, 0)),
        pl.BlockSpec(kext.shape, lambda bi, g, i: (0, 0)),
        pl.BlockSpec((1, NSA_REP, LANES), lambda bi, g, i: (g, 0, 0)),
        pl.BlockSpec((1,) + gsel.shape[1:], lambda bi, g, i: (g, 0, 0)),
    ] + cast_specs
    aug = pltpu.VMEM((s, 2 * HEAD_DIM), BF16)
    return pl.pallas_call(
        functools.partial(_nsa_kernel, n_cast=len(cast_arrays)),
        grid=(b, NSA_GROUPS, n_q),
        in_specs=in_specs,
        out_specs=[pl.BlockSpec((1, NSA_TQ, NSA_REP * HEAD_DIM), lambda bi, g, i: (bi, i, g))] + cast_specs,
        out_shape=[jax.ShapeDtypeStruct((b, s, NSA_WIDTH), BF16)] + cast_shapes,
        scratch_shapes=[aug, aug, aug, pltpu.VMEM((NSA_REP * NSA_TQ, s), F32),
                        pltpu.VMEM((NSA_REP * NSA_TQ, HEAD_DIM), F32),
                        pltpu.VMEM((NSA_TQ // LANES, n_chunks, LANES), F32)],
        compiler_params=_params(3),
        name="nsa_attention",
    )(slopes, proj_a, cmp_kv[0], cmp_kv[1], proj_a, proj_a, proj_a, proj_a, gates3, gain,
      kext, qext, gsel, *cast_arrays)


def _diff_kernel(lq1_ref, lk1_ref, lq2_ref, lk2_ref, q_ref, k_ref, v_ref, g_ref, kext_ref, qext_ref,
                 *rest, n_cast, lambda_init):
    cast_src, o_ref, cast_dst = rest[:n_cast], rest[n_cast], rest[n_cast + 1:2 * n_cast + 1]
    k_aug, s_ref, acc_ref = rest[2 * n_cast + 1:]
    _cast_blocks(cast_src, cast_dst)
    i = pl.program_id(2)
    tq, ck = DIFF_TQ, DIFF_CK
    t0 = i * tq

    @pl.when(i == 0)
    def _():
        for mi in range(2):
            k_aug[mi, :, 0:DIFF_DIM] = k_ref[0, :, mi * DIFF_DIM:(mi + 1) * DIFF_DIM]
            k_aug[mi, :, DIFF_DIM:] = kext_ref[...]

    q = q_ref[0]
    qext = jnp.broadcast_to(qext_ref[0], (tq, LANES)).astype(BF16)
    q_aug = [jnp.concatenate([q[:, mi * DIFF_DIM:(mi + 1) * DIFF_DIM], qext], axis=1) for mi in range(2)]
    t_col = t0 + lax.broadcasted_iota(jnp.int32, (tq, 1), 0)

    def branch(n):
        m_row = []
        for mi in range(2):
            mx = jnp.full((tq, LANES), NEG, F32)
            for c in range(n):
                cols = slice(c * ck, (c + 1) * ck)
                t = _nt_dot(q_aug[mi], k_aug[mi, cols, :])
                if c == n - 1:
                    pos = c * ck + lax.broadcasted_iota(jnp.int32, (1, ck), 1)
                    t = jnp.where(pos <= t_col, t, NEG)
                s_ref[mi, :, cols] = t
                mx = jnp.maximum(mx, _lane_fold(t, jnp.maximum))
            m_row.append(jnp.max(mx, axis=-1, keepdims=True))
        for mi in range(2):
            l_fold = jnp.zeros((tq, LANES), F32)
            acc = jnp.zeros((tq, DIFF_VDIM), F32)
            for c in range(n):
                cols = slice(c * ck, (c + 1) * ck)
                p = jnp.exp2(s_ref[mi, :, cols] - m_row[mi])
                l_fold = l_fold + _lane_fold(p, jnp.add)
                acc = acc + _dot_row_halves(p.astype(BF16), v_ref[0, cols, :])
            acc_ref[mi] = acc * (1.0 / jnp.sum(l_fold, axis=-1, keepdims=True))

    for n in range(1, k_aug.shape[1] // ck + 1):
        pl.when(i + 1 == n)(functools.partial(branch, n))

    lam = (jnp.exp(jnp.sum(lq1_ref[...] * lk1_ref[...], axis=-1, keepdims=True))
           - jnp.exp(jnp.sum(lq2_ref[...] * lk2_ref[...], axis=-1, keepdims=True))
           + lambda_init)
    o = acc_ref[0] - lam * acc_ref[1]
    o = o * lax.rsqrt(jnp.mean(o * o, axis=-1, keepdims=True) + RMS_EPS) * g_ref[...]
    o_ref[0] = (o * (1.0 - lambda_init)).astype(o_ref.dtype)


def _diff_attention(proj_b, lq1, lk1, lq2, lk2, gain, kext, qext, lambda_init, cast_arrays):
    b, s, _ = proj_b.shape
    n_q = s // DIFF_TQ
    cast_specs, cast_shapes = _cast_specs(cast_arrays, b * DIFF_HEADS * n_q,
                                          lambda bi, h, i: (bi * DIFF_HEADS + h) * n_q + i)
    qcol, kcol, vcol = 0, DIFF_HEADS, 2 * DIFF_HEADS
    vec = pl.BlockSpec((1, DIFF_DIM), lambda bi, h, i: (0, 0))
    in_specs = [
        vec, vec, vec, vec,
        pl.BlockSpec((1, DIFF_TQ, DIFF_VDIM), lambda bi, h, i: (bi, i, qcol + h)),
        pl.BlockSpec((1, s, DIFF_VDIM), lambda bi, h, i: (bi, 0, kcol + h)),
        pl.BlockSpec((1, s, DIFF_VDIM), lambda bi, h, i: (bi, 0, vcol + h)),
        pl.BlockSpec((1, DIFF_VDIM), lambda bi, h, i: (0, 0)),
        pl.BlockSpec(kext.shape, lambda bi, h, i: (0, 0)),
        pl.BlockSpec((1, 1, LANES), lambda bi, h, i: (h, 0, 0)),
    ] + cast_specs
    return pl.pallas_call(
        functools.partial(_diff_kernel, n_cast=len(cast_arrays), lambda_init=lambda_init),
        grid=(b, DIFF_HEADS, n_q),
        in_specs=in_specs,
        out_specs=[pl.BlockSpec((1, DIFF_TQ, DIFF_VDIM), lambda bi, h, i: (bi, i, h))] + cast_specs,
        out_shape=[jax.ShapeDtypeStruct((b, s, DIFF_WIDTH), BF16)] + cast_shapes,
        scratch_shapes=[pltpu.VMEM((2, s, 2 * DIFF_DIM), BF16), pltpu.VMEM((2, DIFF_TQ, s), F32),
                        pltpu.VMEM((2, DIFF_TQ, DIFF_VDIM), F32)],
        compiler_params=_params(3),
        name="diff_attention",
    )(lq1.reshape(1, -1), lk1.reshape(1, -1), lq2.reshape(1, -1), lk2.reshape(1, -1),
      proj_b, proj_b, proj_b, gain.reshape(1, -1), kext, qext, *cast_arrays)


def _alibi_slopes(n_heads):
    return np.array([2.0 ** (-8.0 * (h + 1) / n_heads) for h in range(n_heads)], np.float32)


def _round_to_bf16(x):
    bits = np.array(x, np.float32).view(np.uint32)
    bits = (bits + np.uint32(0x7FFF) + ((bits >> np.uint32(16)) & np.uint32(1))) & np.uint32(0xFFFF0000)
    return np.float64(bits.view(np.float32))


def _bf16_pieces(x, n):
    out, rem = [], np.float64(x)
    for _ in range(n):
        piece = _round_to_bf16(rem)
        out.append(piece)
        rem = rem - piece
    return out


def _key_ext(seq):
    ext = np.zeros((seq, LANES), np.float32)
    pos = np.arange(seq)
    ext[pos, pos // SLC_BLOCK] = 1.0
    centred = pos - seq // 2
    hi = (centred // 256) * 256
    lo = centred - hi
    n_slc = seq // SLC_BLOCK
    ext[:, n_slc:n_slc + SLOPE_PIECES] = hi[:, None]
    ext[:, n_slc + SLOPE_PIECES:n_slc + 2 * SLOPE_PIECES] = lo[:, None]
    return ext


def _query_ext(slopes, n_slc):
    ext = np.zeros((len(slopes), LANES), np.float32)
    for h, slope in enumerate(slopes):
        pieces = _bf16_pieces(np.float64(slope) * LOG2E, SLOPE_PIECES)
        ext[h, n_slc:n_slc + SLOPE_PIECES] = pieces
        ext[h, n_slc + SLOPE_PIECES:n_slc + 2 * SLOPE_PIECES] = pieces
    return ext


def _gate_select():
    per = 3 * NSA_REP
    sel = np.zeros((NSA_GROUPS, LANES, per * LANES), np.float32)
    for g in range(NSA_GROUPS):
        for k in range(per):
            sel[g, g * per + k, k * LANES:(k + 1) * LANES] = 1.0
    return sel


def _layer(x, w_in, cmp_pe_k, cmp_w1_k, cmp_w2_k, cmp_pe_v, cmp_w1_v, cmp_w2_v, nsa_out_g,
           lambda_q1, lambda_k1, lambda_q2, lambda_k2, diff_subln_g, w_out, ln1_g, ln1_b,
           w_ff1, w_ff2, ln2_g, ln2_b, lambda_init):
    b, s, d = x.shape
    t = b * s
    x2 = x.reshape(t, d)

    w_t = jnp.swapaxes(w_in, 0, 1)
    x_bf, gates = _gate_cast(x2, w_t, row0=GATE_START)
    proj_a, chunks = _inproj(x_bf, w_t, BF16, row0=0, n=PROJ_A_COLS, scaled_cols=NSA_WIDTH,
                             chunked=(NSA_WIDTH, 2, s), name="in_proj_nsa")
    proj_b = _inproj(x_bf, w_t, BF16, row0=GATE_START, n=PROJ_B_COLS, scaled_cols=DIFF_WIDTH,
                     shift=GATE_COLS, name="in_proj_diff")
    proj_a = proj_a.reshape(b, s, PROJ_A_COLS)
    proj_b = proj_b.reshape(b, s, PROJ_B_COLS)

    pe = jnp.stack([cmp_pe_k, cmp_pe_v]).reshape(2, 1, CMP_BLOCK * HEAD_DIM).astype(BF16)
    w1 = jnp.stack([cmp_w1_k, cmp_w1_v]).astype(BF16)
    w2 = jnp.stack([cmp_w2_k, cmp_w2_v]).astype(BF16)
    cmp_kv = _compress(chunks, pe, w1, w2)

    n_slc = s // SLC_BLOCK
    kext = jnp.asarray(_key_ext(s)).astype(BF16)
    o_nsa, w_ff1_bf, w_out_bf = _nsa_attention(
        proj_a, cmp_kv, gates.reshape(b, s, LANES),
        nsa_out_g.reshape(NSA_GROUPS, NSA_REP, HEAD_DIM),
        jnp.asarray(_alibi_slopes(NSA_HEADS)), kext,
        jnp.asarray(_query_ext(_alibi_slopes(NSA_HEADS), n_slc)).reshape(NSA_GROUPS, NSA_REP, LANES),
        jnp.asarray(_gate_select()).astype(BF16), [w_ff1, w_out])
    o_diff, w_ff2_bf = _diff_attention(
        proj_b, lambda_q1, lambda_k1, lambda_q2, lambda_k2, diff_subln_g, kext,
        jnp.asarray(_query_ext(_alibi_slopes(DIFF_HEADS), n_slc)).reshape(DIFF_HEADS, 1, LANES),
        lambda_init, [w_ff2])

    h1 = _outproj(o_nsa.reshape(t, NSA_WIDTH), o_diff.reshape(t, DIFF_WIDTH), w_out_bf, x2)
    x1, x1_bf = _layer_norm(h1, ln1_g, ln1_b, (F32, BF16))
    hid = _ff1(x1_bf, w_ff1_bf)
    h2 = _ff2(hid, w_ff2_bf, x1)
    (out,) = _layer_norm(h2, ln2_g, ln2_b, (F32,))
    return out.reshape(b, s, d)


def kernel(x, w_in, cmp_pe_k, cmp_w1_k, cmp_w2_k, cmp_pe_v, cmp_w1_v, cmp_w2_v, nsa_out_g,
           lambda_q1, lambda_k1, lambda_q2, lambda_k2, diff_subln_g, w_out, ln1_g, ln1_b,
           w_ff1, w_ff2, ln2_g, ln2_b):
    for l in range(DEPTH):
        lambda_init = 0.8 - 0.6 * math.exp(-0.3 * l)
        x = _layer(x, w_in[l], cmp_pe_k[l], cmp_w1_k[l], cmp_w2_k[l], cmp_pe_v[l], cmp_w1_v[l],
                   cmp_w2_v[l], nsa_out_g[l], lambda_q1[l], lambda_k1[l], lambda_q2[l],
                   lambda_k2[l], diff_subln_g[l], w_out[l], ln1_g[l], ln1_b[l], w_ff1[l],
                   w_ff2[l], ln2_g[l], ln2_b[l], lambda_init)
    return x
```

```python
import functools
import math

import numpy as np
import jax
import jax.numpy as jnp
from jax import lax
from jax.experimental import pallas as pl
from jax.experimental.pallas import tpu as pltpu

F32 = jnp.float32
BF16 = jnp.bfloat16

D_MODEL = 4096
HEAD_DIM = 128
NSA_HEADS = 16
NSA_GROUPS = 4
NSA_REP = 4
CMP_BLOCK = 32
CMP_STRIDE = 16
CMP_HIDDEN = 256
SLC_BLOCK = 64
SLC_TOPK = 16
WINDOW = 512
DIFF_HEADS = 8
DIFF_DIM = 128
DIFF_VDIM = 256
NSA_WIDTH = NSA_HEADS * HEAD_DIM
KV_WIDTH = 6 * NSA_GROUPS * HEAD_DIM
GATE_COLS = 3 * NSA_HEADS
GATE_START = NSA_WIDTH + KV_WIDTH
DIFF_WIDTH = DIFF_HEADS * DIFF_VDIM
PROJ_A_COLS = GATE_START
PROJ_B_COLS = 3 * DIFF_WIDTH
D_FF = 4 * D_MODEL
LN_EPS = 1e-5
RMS_EPS = 1e-6
NEG = -1e30
DEPTH = 1
DEEPNORM_ALPHA = (2.0 * DEPTH) ** 0.25
ATT_SCALE = HEAD_DIM ** -0.5
LOG2E = 1.4426950408889634
ATT_SCALE2 = ATT_SCALE * LOG2E

LANES = 128
BF16_SUBLANES = 16
VMEM_LIMIT_BYTES = 56 * 1024 * 1024

MM_BM = 1024
MM_BN = 1024
INPROJ_BN = 512
OUTPROJ_BN = 512
GATE_BM = 512
FF2_BK = 2048
CAST_COLS = 512
LN_ROWS = 256
NSA_TQ = 256
SLC_CK = 512
WIN_SPAN = WINDOW + NSA_TQ
DIFF_TQ = 512
DIFF_CK = 512
SLOPE_PIECES = 4


def _params(n_axes):
    return pltpu.CompilerParams(dimension_semantics=("arbitrary",) * n_axes,
                                vmem_limit_bytes=VMEM_LIMIT_BYTES)


def _load_weight_rows(b_ref, tail_ref, shift, cols):
    if shift:
        return jnp.concatenate([b_ref[shift:, cols], tail_ref[0:shift, cols]], axis=0)
    return b_ref[:, cols]


def _inproj_kernel(*refs, shift, n_inner, scaled_tiles, chunked_tiles):
    refs = list(refs)
    a_ref, b_ref = refs[:2]
    tail_ref = refs[2] if shift else None
    o_ref = refs[3 if shift else 2]
    wb_ref = refs[-1]
    j = pl.program_id(0)
    i = pl.program_id(1)

    def prepare_slice():
        width = wb_ref.shape[1] // n_inner
        cols = pl.ds(pl.multiple_of(i * width, width), width)
        blk = _load_weight_rows(b_ref, tail_ref, shift, cols)
        blk = blk * jnp.where(j < scaled_tiles, ATT_SCALE2, 1.0)
        wb_ref[j % 2, cols, :] = blk.T.astype(BF16)

    pl.when(j == 0)(prepare_slice)

    @pl.when(j > 0)
    def _():
        acc = jnp.dot(a_ref[...], wb_ref[(j + 1) % 2], preferred_element_type=F32)
        o_ref[...] = acc.astype(o_ref.dtype)
        if chunked_tiles:
            for grp in range(refs[-2].shape[0]):
                refs[-2][grp] = acc[:, grp * HEAD_DIM:(grp + 1) * HEAD_DIM]
        prepare_slice()

    if chunked_tiles:
        first, n_ch = chunked_tiles
        c_ref, stage_ref = refs[-3], refs[-2]

        @pl.when((j - 1 >= first) & (j - 1 < first + n_ch))
        def _():
            n_groups, n_rows = c_ref.shape[2], c_ref.shape[3]
            for grp in range(n_groups):
                for l in range(CMP_STRIDE):
                    c_ref[0, 0, grp, :, l * HEAD_DIM:(l + 1) * HEAD_DIM] = stage_ref[
                        grp, pl.ds(l, n_rows, stride=CMP_STRIDE), :].astype(c_ref.dtype)


def _inproj(a, wt, out_dtype, *, row0, n, scaled_cols, shift=0, chunked=None, bn=INPROJ_BN,
            bm=MM_BM, name):
    m, k = a.shape
    n_tiles = n // bn
    n_inner = m // bm
    assert k % ((m // bm) * LANES) == 0
    jb = row0 // bn

    def tile(j):
        return jnp.minimum(j, n_tiles - 1)

    def row_tile(j, i):
        return jnp.where(j == 0, 0, i)

    in_specs = [pl.BlockSpec((bm, k), lambda j, i: (row_tile(j, i), 0)),
                pl.BlockSpec((bn, k), lambda j, i: (jb + tile(j), 0))]
    args = [a, wt]
    if shift:
        per = bn // LANES
        tb = row0 // LANES
        in_specs.append(pl.BlockSpec((LANES, k), lambda j, i: (tb + per * (tile(j) + 1), 0)))
        args.append(wt)
    out_specs = [pl.BlockSpec((bm, bn), lambda j, i: (row_tile(j, i), jnp.maximum(j - 1, 0)))]
    out_shape = [jax.ShapeDtypeStruct((m, n), out_dtype)]
    scratch = [pltpu.VMEM((2, k, bn), BF16)]
    chunked_tiles = None
    if chunked:
        col0, n_kinds, seq = chunked
        first = col0 // bn
        per_seq = seq // bm
        n_rows = bm // CMP_STRIDE
        chunked_tiles = (first, n_kinds)

        def chunk_block(j, i):
            t = j - 1
            ii = jnp.where(t < first, 0, jnp.where(t >= first + n_kinds, n_inner - 1, i))
            return (jnp.clip(t - first, 0, n_kinds - 1), ii // per_seq, 0, ii % per_seq, 0)

        out_specs.append(pl.BlockSpec((1, 1, bn // HEAD_DIM, n_rows, CMP_STRIDE * HEAD_DIM), chunk_block))
        out_shape.append(jax.ShapeDtypeStruct(
            (n_kinds, m // seq, bn // HEAD_DIM, seq // CMP_STRIDE, CMP_STRIDE * HEAD_DIM), out_dtype))
        scratch.insert(0, pltpu.VMEM((bn // HEAD_DIM, bm, HEAD_DIM), F32))
    res = pl.pallas_call(
        functools.partial(_inproj_kernel, shift=shift, n_inner=n_inner,
                          scaled_tiles=scaled_cols // bn, chunked_tiles=chunked_tiles),
        grid=(n_tiles + 1, n_inner),
        in_specs=in_specs,
        out_specs=out_specs,
        out_shape=out_shape,
        scratch_shapes=scratch,
        compiler_params=_params(2),
        name=name,
    )(*args)
    return res if chunked else res[0]


def _gate_cast_kernel(x_ref, wt_ref, xb_ref, g_ref, wb_ref):
    @pl.when(pl.program_id(0) == 0)
    def _():
        k = wb_ref.shape[0]
        for c0 in range(0, k, CAST_COLS):
            wb_ref[c0:c0 + CAST_COLS, :] = wt_ref[:, c0:c0 + CAST_COLS].T.astype(BF16)

    xb = x_ref[...].astype(BF16)
    xb_ref[...] = xb
    g_ref[...] = jnp.dot(xb, wb_ref[...], preferred_element_type=F32)


def _gate_cast(x, wt, *, row0, bm=GATE_BM):
    m, k = x.shape
    return pl.pallas_call(
        _gate_cast_kernel,
        grid=(m // bm,),
        in_specs=[pl.BlockSpec((bm, k), lambda i: (i, 0)),
                  pl.BlockSpec((LANES, k), lambda i: (row0 // LANES, 0))],
        out_specs=[pl.BlockSpec((bm, k), lambda i: (i, 0)),
                   pl.BlockSpec((bm, LANES), lambda i: (i, 0))],
        out_shape=[jax.ShapeDtypeStruct((m, k), BF16), jax.ShapeDtypeStruct((m, LANES), F32)],
        scratch_shapes=[pltpu.VMEM((k, LANES), BF16)],
        compiler_params=_params(1),
        name="gate_proj_cast",
    )(x, wt)


def _mm_kernel(a_ref, b_ref, o_ref):
    acc = jnp.dot(a_ref[...], b_ref[...], preferred_element_type=F32)
    o_ref[...] = jnp.square(jnp.maximum(acc, 0.0)).astype(o_ref.dtype)


def _ff1(a, b, *, bm=MM_BM, bn=MM_BN):
    m, k = a.shape
    _, n = b.shape
    return pl.pallas_call(
        _mm_kernel,
        grid=(m // bm, n // bn),
        in_specs=[pl.BlockSpec((bm, k), lambda i, j: (i, 0)),
                  pl.BlockSpec((k, bn), lambda i, j: (0, j))],
        out_specs=pl.BlockSpec((bm, bn), lambda i, j: (i, j)),
        out_shape=jax.ShapeDtypeStruct((m, n), BF16),
        compiler_params=_params(2),
        name="ff1",
    )(a, b)


def _outproj_kernel(a1_ref, a2_ref, b_ref, res_ref, o_ref):
    half = a1_ref.shape[1]
    acc = jnp.dot(a1_ref[...], b_ref[0:half, :], preferred_element_type=F32)
    acc = acc + jnp.dot(a2_ref[...], b_ref[half:2 * half, :], preferred_element_type=F32)
    o_ref[...] = DEEPNORM_ALPHA * res_ref[...] + acc


def _outproj(a1, a2, b, res, *, bm=MM_BM, bn=OUTPROJ_BN):
    m, k1 = a1.shape
    _, n = b.shape
    return pl.pallas_call(
        _outproj_kernel,
        grid=(m // bm, n // bn),
        in_specs=[pl.BlockSpec((bm, k1), lambda i, j: (i, 0)),
                  pl.BlockSpec((bm, k1), lambda i, j: (i, 0)),
                  pl.BlockSpec((2 * k1, bn), lambda i, j: (0, j)),
                  pl.BlockSpec((bm, bn), lambda i, j: (i, j))],
        out_specs=pl.BlockSpec((bm, bn), lambda i, j: (i, j)),
        out_shape=jax.ShapeDtypeStruct((m, n), F32),
        compiler_params=_params(2),
        name="outproj",
    )(a1, a2, b, res)


def _ff2_kernel(a_ref, b_ref, res_ref, o_ref, acc_ref):
    kk = pl.program_id(2)

    @pl.when(kk == 0)
    def _():
        acc_ref[...] = jnp.zeros_like(acc_ref)

    acc_ref[...] += jnp.dot(a_ref[...], b_ref[...], preferred_element_type=F32)

    @pl.when(kk == pl.num_programs(2) - 1)
    def _():
        o_ref[...] = DEEPNORM_ALPHA * res_ref[...] + acc_ref[...]


def _ff2(a, b, res, *, bm=MM_BM, bn=MM_BN, bk=FF2_BK):
    m, k = a.shape
    _, n = b.shape
    return pl.pallas_call(
        _ff2_kernel,
        grid=(m // bm, n // bn, k // bk),
        in_specs=[pl.BlockSpec((bm, bk), lambda i, j, kk: (i, kk)),
                  pl.BlockSpec((bk, bn), lambda i, j, kk: (kk, j)),
                  pl.BlockSpec((bm, bn), lambda i, j, kk: (i, j))],
        out_specs=pl.BlockSpec((bm, bn), lambda i, j, kk: (i, j)),
        out_shape=jax.ShapeDtypeStruct((m, n), F32),
        scratch_shapes=[pltpu.VMEM((bm, bn), F32)],
        compiler_params=_params(3),
        name="ff2",
    )(a, b, res)


def _cast_blocks(src_refs, dst_refs):
    for src, dst in zip(src_refs, dst_refs):
        dst[...] = src[...].astype(dst.dtype)


def _cast_specs(arrays, n_steps, step_of):
    specs, shapes = [], []
    for arr in arrays:
        rows, cols = arr.shape
        assert rows % (n_steps * BF16_SUBLANES) == 0
        blk = (rows // n_steps, cols)
        specs.append(pl.BlockSpec(blk, lambda *ids: (step_of(*ids), 0)))
        shapes.append(jax.ShapeDtypeStruct((rows, cols), BF16))
    return specs, shapes


def _ln_kernel(x_ref, g_ref, b_ref, *o_refs):
    x = x_ref[...]
    mu = jnp.mean(x, axis=-1, keepdims=True)
    xc = x - mu
    var = jnp.mean(xc * xc, axis=-1, keepdims=True)
    y = xc * lax.rsqrt(var + LN_EPS) * g_ref[...] + b_ref[...]
    for o_ref in o_refs:
        o_ref[...] = y.astype(o_ref.dtype)


def _layer_norm(x, g, b, out_dtypes):
    m, d = x.shape
    row_spec = pl.BlockSpec((LN_ROWS, d), lambda i: (i, 0))
    vec_spec = pl.BlockSpec((1, d), lambda i: (0, 0))
    return pl.pallas_call(
        _ln_kernel,
        grid=(m // LN_ROWS,),
        in_specs=[row_spec, vec_spec, vec_spec],
        out_specs=[row_spec for _ in out_dtypes],
        out_shape=[jax.ShapeDtypeStruct((m, d), dt) for dt in out_dtypes],
        compiler_params=_params(1),
        name="layer_norm",
    )(x, g.reshape(1, d), b.reshape(1, d))


def _gelu_tanh(x):
    c = math.sqrt(2.0 / math.pi)
    return 0.5 * x * (1.0 + jnp.tanh(c * (x + 0.044715 * (x * x * x))))


def _compress_kernel(c_ref, pe_ref, w1_ref, w2_ref, o_ref):
    half = CMP_STRIDE * HEAD_DIM
    c = c_ref[0, 0, 0]
    n_chunks = c.shape[0]
    y1 = jnp.dot(c, w1_ref[0, 0:half, :], preferred_element_type=F32)
    y2 = jnp.dot(c, w1_ref[0, half:2 * half, :], preferred_element_type=F32)
    pe8 = jnp.broadcast_to(pe_ref[0], (8, 2 * half))
    pb = jnp.dot(pe8, w1_ref[0], preferred_element_type=F32)[0:1, :]
    h = y1 + pltpu.roll(y2, n_chunks - 1, 0) + pb
    a = _gelu_tanh(h).astype(BF16)
    out = jnp.dot(a, w2_ref[0], preferred_element_type=F32)
    row = lax.broadcasted_iota(jnp.int32, out.shape, 0)
    out = jnp.where(row < n_chunks - 1, out, 0.0)
    o_ref[0, 0, 0] = out.astype(o_ref.dtype)


def _compress(chunks, pe, w1, w2):
    _, b, g, n_chunks, width = chunks.shape
    return pl.pallas_call(
        _compress_kernel,
        grid=(2, b, g),
        in_specs=[pl.BlockSpec((1, 1, 1, n_chunks, width), lambda t, i, j: (t, i, j, 0, 0)),
                  pl.BlockSpec((1, 1, 2 * width), lambda t, i, j: (t, 0, 0)),
                  pl.BlockSpec((1, 2 * width, CMP_HIDDEN), lambda t, i, j: (t, 0, 0)),
                  pl.BlockSpec((1, CMP_HIDDEN, HEAD_DIM), lambda t, i, j: (t, 0, 0))],
        out_specs=pl.BlockSpec((1, 1, 1, n_chunks, HEAD_DIM), lambda t, i, j: (t, i, j, 0, 0)),
        out_shape=jax.ShapeDtypeStruct((2, b, g, n_chunks, HEAD_DIM), BF16),
        compiler_params=_params(3),
        name="compress",
    )(chunks, pe, w1, w2)


def _nt_dot(a, b):
    return lax.dot_general(a, b, (((1,), (1,)), ((), ())), preferred_element_type=F32)


def _dot_row_halves(a, b):
    half = a.shape[0] // 2
    return jnp.concatenate([jnp.dot(a[:half], b, preferred_element_type=F32),
                            jnp.dot(a[half:], b, preferred_element_type=F32)], axis=0)


def _lane_fold(x, op):
    out = x[:, 0:LANES]
    for c0 in range(LANES, x.shape[1], LANES):
        out = op(out, x[:, c0:c0 + LANES])
    return out


def _nsa_kernel(slopes_ref, q_ref, kc_ref, vc_ref, ks_ref, vs_ref, kw_ref, vw_ref,
                gate_ref, gain_ref, kext_ref, qext_ref, gsel_ref, *rest, n_cast):
    cast_src, o_ref, cast_dst = rest[:n_cast], rest[n_cast], rest[n_cast + 1:2 * n_cast + 1]
    ks_aug, vs_aug, vw_aug, s_ref, oslc_ref, psum_ref = rest[2 * n_cast + 1:]
    _cast_blocks(cast_src, cast_dst)
    g = pl.program_id(1)
    i = pl.program_id(2)
    tq = NSA_TQ
    rows = NSA_REP * tq
    t0 = i * tq

    @pl.when(i == 0)
    def _():
        ks_aug[:, 0:HEAD_DIM] = ks_ref[0]
        ks_aug[:, HEAD_DIM:] = kext_ref[...]
        ones = jnp.ones((ks_aug.shape[0], HEAD_DIM), BF16)
        vs_aug[:, 0:HEAD_DIM] = vs_ref[0]
        vs_aug[:, HEAD_DIM:] = ones
        vw_aug[:, 0:HEAD_DIM] = vw_ref[0]
        vw_aug[:, HEAD_DIM:] = ones

    q = q_ref[0]
    q_heads = [q[:, r * HEAD_DIM:(r + 1) * HEAD_DIM] for r in range(NSA_REP)]
    qs = jnp.concatenate(q_heads, axis=0)
    t_col = t0 + lax.broadcasted_iota(jnp.int32, (tq, 1), 0)
    head = [slice(r * tq, (r + 1) * tq) for r in range(NSA_REP)]

    n_cmp = kc_ref.shape[2]
    s_t = _nt_dot(kc_ref[0, 0], qs)
    n_col = lax.broadcasted_iota(jnp.int32, (n_cmp, 1), 0)
    t_row = t0 + (lax.broadcasted_iota(jnp.int32, (1, rows), 1) & (tq - 1))
    cmask = (n_col * CMP_STRIDE + (CMP_BLOCK - 1)) <= t_row
    s_t = jnp.where(cmask, s_t, NEG)
    m = jnp.max(s_t, axis=0, keepdims=True)
    e = jnp.where(cmask, jnp.exp2(s_t - m), 0.0)
    l = jnp.sum(e, axis=0, keepdims=True)
    p_t = e * (1.0 / jnp.maximum(l, 1e-30))
    p_sum_t = p_t[:, head[0]]
    for r in range(1, NSA_REP):
        p_sum_t = p_sum_t + p_t[:, head[r]]
    o_cmp = lax.dot_general(p_t.astype(BF16), vc_ref[0, 0], (((0,), (0,)), ((), ())),
                            preferred_element_type=F32)

    n_slc = n_cmp * CMP_STRIDE // SLC_BLOCK
    per = SLC_BLOCK // CMP_STRIDE
    for hl in range(tq // LANES):
        psum_ref[hl] = p_sum_t[:, hl * LANES:(hl + 1) * LANES]
    rows_k = [jnp.concatenate([psum_ref[hl, pl.ds(k, n_slc, stride=per), :] for hl in range(tq // LANES)],
                              axis=1) for k in range(per)]
    j_t = lax.broadcasted_iota(jnp.int32, (n_slc, tq), 0)
    prev_last = jnp.where(j_t == 0, 0.0, pltpu.roll(rows_k[per - 1], 1, 0))
    imp_t = rows_k[0] + rows_k[1] + rows_k[2] + 0.5 * (rows_k[3] + prev_last)
    t_t = t0 + lax.broadcasted_iota(jnp.int32, (n_slc, tq), 1)
    causal = j_t * SLC_BLOCK <= t_t
    tb = lax.shift_right_logical(t_t, int(math.log2(SLC_BLOCK)))
    forced = jnp.where(j_t == 0, 1e6, jnp.where(j_t == tb, 1e6, jnp.where(j_t == tb - 1, 1e6, imp_t)))
    score = jnp.where(causal, forced, NEG)
    rank = jnp.zeros((n_slc, tq), F32)
    for jp in range(n_slc):
        row = score[jp:jp + 1, :]
        ge = jnp.where(row >= score, 1.0, 0.0)
        gt = jnp.where(row > score, 1.0, 0.0)
        rank = rank + jnp.where(j_t > jp, ge, gt)
    sel_bias = jnp.where(causal, jnp.where(rank < SLC_TOPK, 0.0, NEG), NEG)

    sel_rows = jnp.concatenate([sel_bias, jnp.zeros((LANES - n_slc, tq), F32)], axis=0).T
    q_aug = jnp.concatenate(
        [jnp.concatenate([q_heads[r], (sel_rows + qext_ref[0, r:r + 1, :]).astype(BF16)], axis=1)
         for r in range(NSA_REP)], axis=0)
    ck = SLC_CK
    n_chunks = (t0 + tq + ck - 1) // ck

    def slc_branch(n):
        mx = jnp.full((rows, LANES), NEG, F32)
        for c in range(n):
            cols = slice(c * ck, (c + 1) * ck)
            t = _nt_dot(q_aug, ks_aug[cols, :])
            if c == n - 1:
                pos = c * ck + lax.broadcasted_iota(jnp.int32, (1, ck), 1)
                cb = jnp.where(pos <= t_col, 0.0, NEG)
                t = t + jnp.concatenate([cb] * NSA_REP, axis=0)
            s_ref[:, cols] = t
            mx = jnp.maximum(mx, _lane_fold(t, jnp.maximum))
        m_slc = jnp.max(mx, axis=-1, keepdims=True)
        acc = jnp.zeros((rows, 2 * HEAD_DIM), F32)
        for c in range(n):
            cols = slice(c * ck, (c + 1) * ck)
            p = jnp.exp2(s_ref[:, cols] - m_slc).astype(BF16)
            acc = acc + _dot_row_halves(p, vs_aug[cols, :])
        oslc_ref[...] = acc[:, 0:HEAD_DIM] * (1.0 / acc[:, HEAD_DIM:])

    for n in range(1, ks_aug.shape[0] // ck + 1):
        pl.when(n_chunks == n)(functools.partial(slc_branch, n))
    o_slc = oslc_ref[...]

    slope2 = [slopes_ref[g * NSA_REP + r] * LOG2E for r in range(NSA_REP)]
    k0 = pl.multiple_of(jnp.maximum(t0 - WINDOW, 0), tq)
    pos = k0 + lax.broadcasted_iota(jnp.int32, (1, WIN_SPAN), 1)
    dist = t_col - pos
    wbias = jnp.where(dist >= 0, jnp.where(dist < WINDOW, 0.0, NEG), NEG)
    posrel = (pos - t0).astype(F32)
    kband = kw_ref[0, pl.ds(k0, WIN_SPAN), :]
    vband = vw_aug[pl.ds(k0, WIN_SPAN), :]
    halves = []
    for pair in range(NSA_REP // 2):
        ps = []
        for r in (2 * pair, 2 * pair + 1):
            t = _nt_dot(q_heads[r], kband) + slope2[r] * posrel + wbias
            m = jnp.max(t, axis=-1, keepdims=True)
            ps.append(jnp.exp2(t - m).astype(BF16))
        halves.append(jnp.dot(jnp.concatenate(ps, axis=0), vband, preferred_element_type=F32))
    acc_w = jnp.concatenate(halves, axis=0)
    o_win = acc_w[:, 0:HEAD_DIM] * (1.0 / acc_w[:, HEAD_DIM:])

    sig = jax.nn.sigmoid(gate_ref[0]).astype(BF16)
    gates = jnp.dot(sig, gsel_ref[0], preferred_element_type=F32)
    for r in range(NSA_REP):
        gt = [gates[:, (3 * r + br) * LANES:(3 * r + br + 1) * LANES] for br in range(3)]
        o = gt[0] * o_cmp[head[r]] + gt[1] * o_slc[head[r]] + gt[2] * o_win[head[r]]
        o = o * lax.rsqrt(jnp.mean(o * o, axis=-1, keepdims=True) + RMS_EPS) * gain_ref[0, r:r + 1, :]
        o_ref[0, :, r * HEAD_DIM:(r + 1) * HEAD_DIM] = o.astype(o_ref.dtype)


def _nsa_attention(proj_a, cmp_kv, gates3, gain, slopes, kext, qext, gsel, cast_arrays):
    b, s, _ = proj_a.shape
    n_chunks = cmp_kv.shape[3]
    n_q = s // NSA_TQ
    cast_specs, cast_shapes = _cast_specs(cast_arrays, b * NSA_GROUPS * n_q,
                                          lambda bi, g, i: (bi * NSA_GROUPS + g) * n_q + i)
    col0 = NSA_WIDTH // HEAD_DIM

    def kv_spec(idx):
        return pl.BlockSpec((1, s, HEAD_DIM), lambda bi, g, i, idx=idx: (bi, 0, col0 + idx * NSA_GROUPS + g))

    cmp_spec = pl.BlockSpec((1, 1, n_chunks, HEAD_DIM), lambda bi, g, i: (bi, g, 0, 0))
    in_specs = [
        pl.BlockSpec(memory_space=pltpu.SMEM),
        pl.BlockSpec((1, NSA_TQ, NSA_REP * HEAD_DIM), lambda bi, g, i: (bi, i, g)),
        cmp_spec, cmp_spec,
        kv_spec(2), kv_spec(3), kv_spec(4), kv_spec(5),
        pl.BlockSpec((1, NSA_TQ, LANES), lambda bi, g, i: (bi, i, 0)),
        pl.BlockSpec((1, NSA_REP, HEAD_DIM), lambda bi, g, i: (g, 0, 0)),
        pl.BlockSpec(kext.shape, lambda bi, g, i: (0, 0)),
        pl.BlockSpec((1, NSA_REP, LANES), lambda bi, g, i: (g, 0, 0)),
        pl.BlockSpec((1,) + gsel.shape[1:], lambda bi, g, i: (g, 0, 0)),
    ] + cast_specs
    aug = pltpu.VMEM((s, 2 * HEAD_DIM), BF16)
    return pl.pallas_call(
        functools.partial(_nsa_kernel, n_cast=len(cast_arrays)),
        grid=(b, NSA_GROUPS, n_q),
        in_specs=in_specs,
        out_specs=[pl.BlockSpec((1, NSA_TQ, NSA_REP * HEAD_DIM), lambda bi, g, i: (bi, i, g))] + cast_specs,
        out_shape=[jax.ShapeDtypeStruct((b, s, NSA_WIDTH), BF16)] + cast_shapes,
        scratch_shapes=[aug, aug, aug, pltpu.VMEM((NSA_REP * NSA_TQ, s), F32),
                        pltpu.VMEM((NSA_REP * NSA_TQ, HEAD_DIM), F32),
                        pltpu.VMEM((NSA_TQ // LANES, n_chunks, LANES), F32)],
        compiler_params=_params(3),
        name="nsa_attention",
    )(slopes, proj_a, cmp_kv[0], cmp_kv[1], proj_a, proj_a, proj_a, proj_a, gates3, gain,
      kext, qext, gsel, *cast_arrays)


def _diff_kernel(lq1_ref, lk1_ref, lq2_ref, lk2_ref, q_ref, k_ref, v_ref, g_ref, kext_ref, qext_ref,
                 *rest, n_cast, lambda_init):
    cast_src, o_ref, cast_dst = rest[:n_cast], rest[n_cast], rest[n_cast + 1:2 * n_cast + 1]
    k_aug, s_ref, acc_ref = rest[2 * n_cast + 1:]
    _cast_blocks(cast_src, cast_dst)
    i = pl.program_id(2)
    tq, ck = DIFF_TQ, DIFF_CK
    t0 = i * tq

    @pl.when(i == 0)
    def _():
        for mi in range(2):
            k_aug[mi, :, 0:DIFF_DIM] = k_ref[0, :, mi * DIFF_DIM:(mi + 1) * DIFF_DIM]
            k_aug[mi, :, DIFF_DIM:] = kext_ref[...]

    q = q_ref[0]
    qext = jnp.broadcast_to(qext_ref[0], (tq, LANES)).astype(BF16)
    q_aug = [jnp.concatenate([q[:, mi * DIFF_DIM:(mi + 1) * DIFF_DIM], qext], axis=1) for mi in range(2)]
    t_col = t0 + lax.broadcasted_iota(jnp.int32, (tq, 1), 0)

    def branch(n):
        m_row = []
        for mi in range(2):
            mx = jnp.full((tq, LANES), NEG, F32)
            for c in range(n):
                cols = slice(c * ck, (c + 1) * ck)
                t = _nt_dot(q_aug[mi], k_aug[mi, cols, :])
                if c == n - 1:
                    pos = c * ck + lax.broadcasted_iota(jnp.int32, (1, ck), 1)
                    t = jnp.where(pos <= t_col, t, NEG)
                s_ref[mi, :, cols] = t
                mx = jnp.maximum(mx, _lane_fold(t, jnp.maximum))
            m_row.append(jnp.max(mx, axis=-1, keepdims=True))
        for mi in range(2):
            l_fold = jnp.zeros((tq, LANES), F32)
            acc = jnp.zeros((tq, DIFF_VDIM), F32)
            for c in range(n):
                cols = slice(c * ck, (c + 1) * ck)
                p = jnp.exp2(s_ref[mi, :, cols] - m_row[mi])
                l_fold = l_fold + _lane_fold(p, jnp.add)
                acc = acc + _dot_row_halves(p.astype(BF16), v_ref[0, cols, :])
            acc_ref[mi] = acc * (1.0 / jnp.sum(l_fold, axis=-1, keepdims=True))

    for n in range(1, k_aug.shape[1] // ck + 1):
        pl.when(i + 1 == n)(functools.partial(branch, n))

    lam = (jnp.exp(jnp.sum(lq1_ref[...] * lk1_ref[...], axis=-1, keepdims=True))
           - jnp.exp(jnp.sum(lq2_ref[...] * lk2_ref[...], axis=-1, keepdims=True))
           + lambda_init)
    o = acc_ref[0] - lam * acc_ref[1]
    o = o * lax.rsqrt(jnp.mean(o * o, axis=-1, keepdims=True) + RMS_EPS) * g_ref[...]
    o_ref[0] = (o * (1.0 - lambda_init)).astype(o_ref.dtype)


def _diff_attention(proj_b, lq1, lk1, lq2, lk2, gain, kext, qext, lambda_init, cast_arrays):
    b, s, _ = proj_b.shape
    n_q = s // DIFF_TQ
    cast_specs, cast_shapes = _cast_specs(cast_arrays, b * DIFF_HEADS * n_q,
                                          lambda bi, h, i: (bi * DIFF_HEADS + h) * n_q + i)
    qcol, kcol, vcol = 0, DIFF_HEADS, 2 * DIFF_HEADS
    vec = pl.BlockSpec((1, DIFF_DIM), lambda bi, h, i: (0, 0))
    in_specs = [
        vec, vec, vec, vec,
        pl.BlockSpec((1, DIFF_TQ, DIFF_VDIM), lambda bi, h, i: (bi, i, qcol + h)),
        pl.BlockSpec((1, s, DIFF_VDIM), lambda bi, h, i: (bi, 0, kcol + h)),
        pl.BlockSpec((1, s, DIFF_VDIM), lambda bi, h, i: (bi, 0, vcol + h)),
        pl.BlockSpec((1, DIFF_VDIM), lambda bi, h, i: (0, 0)),
        pl.BlockSpec(kext.shape, lambda bi, h, i: (0, 0)),
        pl.BlockSpec((1, 1, LANES), lambda bi, h, i: (h, 0, 0)),
    ] + cast_specs
    return pl.pallas_call(
        functools.partial(_diff_kernel, n_cast=len(cast_arrays), lambda_init=lambda_init),
        grid=(b, DIFF_HEADS, n_q),
        in_specs=in_specs,
        out_specs=[pl.BlockSpec((1, DIFF_TQ, DIFF_VDIM), lambda bi, h, i: (bi, i, h))] + cast_specs,
        out_shape=[jax.ShapeDtypeStruct((b, s, DIFF_WIDTH), BF16)] + cast_shapes,
        scratch_shapes=[pltpu.VMEM((2, s, 2 * DIFF_DIM), BF16), pltpu.VMEM((2, DIFF_TQ, s), F32),
                        pltpu.VMEM((2, DIFF_TQ, DIFF_VDIM), F32)],
        compiler_params=_params(3),
        name="diff_attention",
    )(lq1.reshape(1, -1), lk1.reshape(1, -1), lq2.reshape(1, -1), lk2.reshape(1, -1),
      proj_b, proj_b, proj_b, gain.reshape(1, -1), kext, qext, *cast_arrays)


def _alibi_slopes(n_heads):
    return np.array([2.0 ** (-8.0 * (h + 1) / n_heads) for h in range(n_heads)], np.float32)


def _round_to_bf16(x):
    bits = np.array(x, np.float32).view(np.uint32)
    bits = (bits + np.uint32(0x7FFF) + ((bits >> np.uint32(16)) & np.uint32(1))) & np.uint32(0xFFFF0000)
    return np.float64(bits.view(np.float32))


def _bf16_pieces(x, n):
    out, rem = [], np.float64(x)
    for _ in range(n):
        piece = _round_to_bf16(rem)
        out.append(piece)
        rem = rem - piece
    return out


def _key_ext(seq):
    ext = np.zeros((seq, LANES), np.float32)
    pos = np.arange(seq)
    ext[pos, pos // SLC_BLOCK] = 1.0
    centred = pos - seq // 2
    hi = (centred // 256) * 256
    lo = centred - hi
    n_slc = seq // SLC_BLOCK
    ext[:, n_slc:n_slc + SLOPE_PIECES] = hi[:, None]
    ext[:, n_slc + SLOPE_PIECES:n_slc + 2 * SLOPE_PIECES] = lo[:, None]
    return ext


def _query_ext(slopes, n_slc):
    ext = np.zeros((len(slopes), LANES), np.float32)
    for h, slope in enumerate(slopes):
        pieces = _bf16_pieces(np.float64(slope) * LOG2E, SLOPE_PIECES)
        ext[h, n_slc:n_slc + SLOPE_PIECES] = pieces
        ext[h, n_slc + SLOPE_PIECES:n_slc + 2 * SLOPE_PIECES] = pieces
    return ext


def _gate_select():
    per = 3 * NSA_REP
    sel = np.zeros((NSA_GROUPS, LANES, per * LANES), np.float32)
    for g in range(NSA_GROUPS):
        for k in range(per):
            sel[g, g * per + k, k * LANES:(k + 1) * LANES] = 1.0
    return sel


def _layer(x, w_in, cmp_pe_k, cmp_w1_k, cmp_w2_k, cmp_pe_v, cmp_w1_v, cmp_w2_v, nsa_out_g,
           lambda_q1, lambda_k1, lambda_q2, lambda_k2, diff_subln_g, w_out, ln1_g, ln1_b,
           w_ff1, w_ff2, ln2_g, ln2_b, lambda_init):
    b, s, d = x.shape
    t = b * s
    x2 = x.reshape(t, d)

    w_t = jnp.swapaxes(w_in, 0, 1)
    x_bf, gates = _gate_cast(x2, w_t, row0=GATE_START)
    proj_a, chunks = _inproj(x_bf, w_t, BF16, row0=0, n=PROJ_A_COLS, scaled_cols=NSA_WIDTH,
                             chunked=(NSA_WIDTH, 2, s), name="in_proj_nsa")
    proj_b = _inproj(x_bf, w_t, BF16, row0=GATE_START, n=PROJ_B_COLS, scaled_cols=DIFF_WIDTH,
                     shift=GATE_COLS, name="in_proj_diff")
    proj_a = proj_a.reshape(b, s, PROJ_A_COLS)
    proj_b = proj_b.reshape(b, s, PROJ_B_COLS)

    pe = jnp.stack([cmp_pe_k, cmp_pe_v]).reshape(2, 1, CMP_BLOCK * HEAD_DIM).astype(BF16)
    w1 = jnp.stack([cmp_w1_k, cmp_w1_v]).astype(BF16)
    w2 = jnp.stack([cmp_w2_k, cmp_w2_v]).astype(BF16)
    cmp_kv = _compress(chunks, pe, w1, w2)

    n_slc = s // SLC_BLOCK
    kext = jnp.asarray(_key_ext(s)).astype(BF16)
    o_nsa, w_ff1_bf, w_out_bf = _nsa_attention(
        proj_a, cmp_kv, gates.reshape(b, s, LANES),
        nsa_out_g.reshape(NSA_GROUPS, NSA_REP, HEAD_DIM),
        jnp.asarray(_alibi_slopes(NSA_HEADS)), kext,
        jnp.asarray(_query_ext(_alibi_slopes(NSA_HEADS), n_slc)).reshape(NSA_GROUPS, NSA_REP, LANES),
        jnp.asarray(_gate_select()).astype(BF16), [w_ff1, w_out])
    o_diff, w_ff2_bf = _diff_attention(
        proj_b, lambda_q1, lambda_k1, lambda_q2, lambda_k2, diff_subln_g, kext,
        jnp.asarray(_query_ext(_alibi_slopes(DIFF_HEADS), n_slc)).reshape(DIFF_HEADS, 1, LANES),
        lambda_init, [w_ff2])

    h1 = _outproj(o_nsa.reshape(t, NSA_WIDTH), o_diff.reshape(t, DIFF_WIDTH), w_out_bf, x2)
    x1, x1_bf = _layer_norm(h1, ln1_g, ln1_b, (F32, BF16))
    hid = _ff1(x1_bf, w_ff1_bf)
    h2 = _ff2(hid, w_ff2_bf, x1)
    (out,) = _layer_norm(h2, ln2_g, ln2_b, (F32,))
    return out.reshape(b, s, d)


def kernel(x, w_in, cmp_pe_k, cmp_w1_k, cmp_w2_k, cmp_pe_v, cmp_w1_v, cmp_w2_v, nsa_out_g,
           lambda_q1, lambda_k1, lambda_q2, lambda_k2, diff_subln_g, w_out, ln1_g, ln1_b,
           w_ff1, w_ff2, ln2_g, ln2_b):
    for l in range(DEPTH):
        lambda_init = 0.8 - 0.6 * math.exp(-0.3 * l)
        x = _layer(x, w_in[l], cmp_pe_k[l], cmp_w1_k[l], cmp_w2_k[l], cmp_pe_v[l], cmp_w1_v[l],
                   cmp_w2_v[l], nsa_out_g[l], lambda_q1[l], lambda_k1[l], lambda_q2[l],
                   lambda_k2[l], diff_subln_g[l], w_out[l], ln1_g[l], ln1_b[l], w_ff1[l],
                   w_ff2[l], ln2_g[l], ln2_b[l], lambda_init)
    return x
```

```python
import functools
import math

import numpy as np
import jax
import jax.numpy as jnp
from jax import lax
from jax.experimental import pallas as pl
from jax.experimental.pallas import tpu as pltpu

F32 = jnp.float32
BF16 = jnp.bfloat16

D_MODEL = 4096
HEAD_DIM = 128
NSA_HEADS = 16
NSA_GROUPS = 4
NSA_REP = 4
CMP_BLOCK = 32
CMP_STRIDE = 16
CMP_HIDDEN = 256
SLC_BLOCK = 64
SLC_TOPK = 16
WINDOW = 512
DIFF_HEADS = 8
DIFF_DIM = 128
DIFF_VDIM = 256
NSA_WIDTH = NSA_HEADS * HEAD_DIM
KV_WIDTH = 6 * NSA_GROUPS * HEAD_DIM
GATE_COLS = 3 * NSA_HEADS
GATE_START = NSA_WIDTH + KV_WIDTH
DIFF_WIDTH = DIFF_HEADS * DIFF_VDIM
PROJ_A_COLS = GATE_START
PROJ_B_COLS = 3 * DIFF_WIDTH
LN_EPS = 1e-5
RMS_EPS = 1e-6
NEG = -1e30
DEPTH = 1
DEEPNORM_ALPHA = (2.0 * DEPTH) ** 0.25
ATT_SCALE = HEAD_DIM ** -0.5
LOG2E = 1.4426950408889634
ATT_SCALE2 = ATT_SCALE * LOG2E

LANES = 128
BF16_SUBLANES = 16
VMEM_LIMIT_BYTES = 56 * 1024 * 1024

MM_BM = 1024
MM_BN = 1024
INPROJ_BN = 512
OUTPROJ_BN = 512
GATE_BM = 512
FF2_BK = 4096
CAST_COLS = 512
LN_ROWS = 256
NSA_TQ = 256
SLC_CK = 512
WIN_SPAN = WINDOW + NSA_TQ
DIFF_TQ = 512
DIFF_CK = 512
SLOPE_PIECES = 4


def _params(n_axes):
    return pltpu.CompilerParams(dimension_semantics=("arbitrary",) * n_axes,
                                vmem_limit_bytes=VMEM_LIMIT_BYTES)


def _load_weight_rows(b_ref, tail_ref, shift, cols):
    if shift:
        return jnp.concatenate([b_ref[shift:, cols], tail_ref[0:shift, cols]], axis=0)
    return b_ref[:, cols]


def _inproj_kernel(*refs, shift, n_inner, scaled_tiles, chunked_tiles):
    refs = list(refs)
    a_ref, b_ref = refs[:2]
    tail_ref = refs[2] if shift else None
    o_ref = refs[3 if shift else 2]
    wb_ref = refs[-1]
    j = pl.program_id(0)
    i = pl.program_id(1)

    def prepare_slice():
        width = wb_ref.shape[1] // n_inner
        cols = pl.ds(pl.multiple_of(i * width, width), width)
        blk = _load_weight_rows(b_ref, tail_ref, shift, cols)
        blk = blk * jnp.where(j < scaled_tiles, ATT_SCALE2, 1.0)
        wb_ref[j % 2, cols, :] = blk.T.astype(BF16)

    pl.when(j == 0)(prepare_slice)

    @pl.when(j > 0)
    def _():
        acc = jnp.dot(a_ref[...], wb_ref[(j + 1) % 2], preferred_element_type=F32)
        o_ref[...] = acc.astype(o_ref.dtype)
        if chunked_tiles:
            for grp in range(refs[-2].shape[0]):
                refs[-2][grp] = acc[:, grp * HEAD_DIM:(grp + 1) * HEAD_DIM]
        prepare_slice()

    if chunked_tiles:
        first, n_ch = chunked_tiles
        c_ref, stage_ref = refs[-3], refs[-2]

        @pl.when((j - 1 >= first) & (j - 1 < first + n_ch))
        def _():
            n_groups, n_rows = c_ref.shape[2], c_ref.shape[3]
            for grp in range(n_groups):
                for l in range(CMP_STRIDE):
                    c_ref[0, 0, grp, :, l * HEAD_DIM:(l + 1) * HEAD_DIM] = stage_ref[
                        grp, pl.ds(l, n_rows, stride=CMP_STRIDE), :].astype(c_ref.dtype)


def _inproj(a, wt, out_dtype, *, row0, n, scaled_cols, shift=0, chunked=None, bn=INPROJ_BN,
            bm=MM_BM, name):
    m, k = a.shape
    n_tiles = n // bn
    n_inner = m // bm
    assert k % ((m // bm) * LANES) == 0
    jb = row0 // bn

    def tile(j):
        return jnp.minimum(j, n_tiles - 1)

    def row_tile(j, i):
        return jnp.where(j == 0, 0, i)

    in_specs = [pl.BlockSpec((bm, k), lambda j, i: (row_tile(j, i), 0)),
                pl.BlockSpec((bn, k), lambda j, i: (jb + tile(j), 0))]
    args = [a, wt]
    if shift:
        per = bn // LANES
        tb = row0 // LANES
        in_specs.append(pl.BlockSpec((LANES, k), lambda j, i: (tb + per * (tile(j) + 1), 0)))
        args.append(wt)
    out_specs = [pl.BlockSpec((bm, bn), lambda j, i: (row_tile(j, i), jnp.maximum(j - 1, 0)))]
    out_shape = [jax.ShapeDtypeStruct((m, n), out_dtype)]
    scratch = [pltpu.VMEM((2, k, bn), BF16)]
    chunked_tiles = None
    if chunked:
        col0, n_kinds, seq = chunked
        first = col0 // bn
        per_seq = seq // bm
        n_rows = bm // CMP_STRIDE
        chunked_tiles = (first, n_kinds)

        def chunk_block(j, i):
            t = j - 1
            ii = jnp.where(t < first, 0, jnp.where(t >= first + n_kinds, n_inner - 1, i))
            return (jnp.clip(t - first, 0, n_kinds - 1), ii // per_seq, 0, ii % per_seq, 0)

        out_specs.append(pl.BlockSpec((1, 1, bn // HEAD_DIM, n_rows, CMP_STRIDE * HEAD_DIM), chunk_block))
        out_shape.append(jax.ShapeDtypeStruct(
            (n_kinds, m // seq, bn // HEAD_DIM, seq // CMP_STRIDE, CMP_STRIDE * HEAD_DIM), out_dtype))
        scratch.insert(0, pltpu.VMEM((bn // HEAD_DIM, bm, HEAD_DIM), F32))
    res = pl.pallas_call(
        functools.partial(_inproj_kernel, shift=shift, n_inner=n_inner,
                          scaled_tiles=scaled_cols // bn, chunked_tiles=chunked_tiles),
        grid=(n_tiles + 1, n_inner),
        in_specs=in_specs,
        out_specs=out_specs,
        out_shape=out_shape,
        scratch_shapes=scratch,
        compiler_params=_params(2),
        name=name,
    )(*args)
    return res if chunked else res[0]


def _gate_cast_kernel(x_ref, wt_ref, xb_ref, g_ref, wb_ref):
    @pl.when(pl.program_id(0) == 0)
    def _():
        k = wb_ref.shape[0]
        for c0 in range(0, k, CAST_COLS):
            wb_ref[c0:c0 + CAST_COLS, :] = wt_ref[:, c0:c0 + CAST_COLS].T.astype(BF16)

    xb = x_ref[...].astype(BF16)
    xb_ref[...] = xb
    g_ref[...] = jnp.dot(xb, wb_ref[...], preferred_element_type=F32)


def _gate_cast(x, wt, *, row0, bm=GATE_BM):
    m, k = x.shape
    return pl.pallas_call(
        _gate_cast_kernel,
        grid=(m // bm,),
        in_specs=[pl.BlockSpec((bm, k), lambda i: (i, 0)),
                  pl.BlockSpec((LANES, k), lambda i: (row0 // LANES, 0))],
        out_specs=[pl.BlockSpec((bm, k), lambda i: (i, 0)),
                   pl.BlockSpec((bm, LANES), lambda i: (i, 0))],
        out_shape=[jax.ShapeDtypeStruct((m, k), BF16), jax.ShapeDtypeStruct((m, LANES), F32)],
        scratch_shapes=[pltpu.VMEM((k, LANES), BF16)],
        compiler_params=_params(1),
        name="gate_proj_cast",
    )(x, wt)


def _mm_kernel(a_ref, b_ref, o_ref):
    acc = jnp.dot(a_ref[...], b_ref[...], preferred_element_type=F32)
    o_ref[...] = jnp.square(jnp.maximum(acc, 0.0)).astype(o_ref.dtype)


def _ff1(a, b, *, bm=MM_BM, bn=MM_BN):
    m, k = a.shape
    _, n = b.shape
    return pl.pallas_call(
        _mm_kernel,
        grid=(m // bm, n // bn),
        in_specs=[pl.BlockSpec((bm, k), lambda i, j: (i, 0)),
                  pl.BlockSpec((k, bn), lambda i, j: (0, j))],
        out_specs=pl.BlockSpec((bm, bn), lambda i, j: (i, j)),
        out_shape=jax.ShapeDtypeStruct((m, n), BF16),
        compiler_params=_params(2),
        name="ff1",
    )(a, b)


def _outproj_kernel(a1_ref, a2_ref, b_ref, res_ref, o_ref):
    half = a1_ref.shape[1]
    acc = jnp.dot(a1_ref[...], b_ref[0:half, :], preferred_element_type=F32)
    acc = acc + jnp.dot(a2_ref[...], b_ref[half:2 * half, :], preferred_element_type=F32)
    o_ref[...] = DEEPNORM_ALPHA * res_ref[...] + acc


def _outproj(a1, a2, b, res, *, bm=MM_BM, bn=OUTPROJ_BN):
    m, k1 = a1.shape
    _, n = b.shape
    return pl.pallas_call(
        _outproj_kernel,
        grid=(m // bm, n // bn),
        in_specs=[pl.BlockSpec((bm, k1), lambda i, j: (i, 0)),
                  pl.BlockSpec((bm, k1), lambda i, j: (i, 0)),
                  pl.BlockSpec((2 * k1, bn), lambda i, j: (0, j)),
                  pl.BlockSpec((bm, bn), lambda i, j: (i, j))],
        out_specs=pl.BlockSpec((bm, bn), lambda i, j: (i, j)),
        out_shape=jax.ShapeDtypeStruct((m, n), F32),
        compiler_params=_params(2),
        name="outproj",
    )(a1, a2, b, res)


def _ff2_kernel(a_ref, b_ref, res_ref, o_ref):
    @pl.when(pl.program_id(2) == 0)
    def _():
        o_ref[...] = DEEPNORM_ALPHA * res_ref[...]

    o_ref[...] += jnp.dot(a_ref[...], b_ref[...], preferred_element_type=F32)


def _ff2(a, b, res, *, bm=MM_BM, bn=MM_BN, bk=FF2_BK):
    m, k = a.shape
    _, n = b.shape
    return pl.pallas_call(
        _ff2_kernel,
        grid=(m // bm, n // bn, k // bk),
        in_specs=[pl.BlockSpec((bm, bk), lambda i, j, kk: (i, kk)),
                  pl.BlockSpec((bk, bn), lambda i, j, kk: (kk, j)),
                  pl.BlockSpec((bm, bn), lambda i, j, kk: (i, j))],
        out_specs=pl.BlockSpec((bm, bn), lambda i, j, kk: (i, j)),
        out_shape=jax.ShapeDtypeStruct((m, n), F32),
        compiler_params=_params(3),
        name="ff2",
    )(a, b, res)


def _cast_blocks(src_refs, dst_refs):
    for src, dst in zip(src_refs, dst_refs):
        dst[...] = src[...].astype(dst.dtype)


def _cast_specs(arrays, n_steps, step_of):
    specs, shapes = [], []
    for arr in arrays:
        rows, cols = arr.shape
        assert rows % (n_steps * BF16_SUBLANES) == 0
        blk = (rows // n_steps, cols)
        specs.append(pl.BlockSpec(blk, lambda *ids: (step_of(*ids), 0)))
        shapes.append(jax.ShapeDtypeStruct((rows, cols), BF16))
    return specs, shapes


def _ln_kernel(x_ref, g_ref, b_ref, *o_refs):
    x = x_ref[...]
    mu = jnp.mean(x, axis=-1, keepdims=True)
    xc = x - mu
    var = jnp.mean(xc * xc, axis=-1, keepdims=True)
    y = xc * lax.rsqrt(var + LN_EPS) * g_ref[...] + b_ref[...]
    for o_ref in o_refs:
        o_ref[...] = y.astype(o_ref.dtype)


def _layer_norm(x, g, b, out_dtypes):
    m, d = x.shape
    row_spec = pl.BlockSpec((LN_ROWS, d), lambda i: (i, 0))
    vec_spec = pl.BlockSpec((1, d), lambda i: (0, 0))
    return pl.pallas_call(
        _ln_kernel,
        grid=(m // LN_ROWS,),
        in_specs=[row_spec, vec_spec, vec_spec],
        out_specs=[row_spec for _ in out_dtypes],
        out_shape=[jax.ShapeDtypeStruct((m, d), dt) for dt in out_dtypes],
        compiler_params=_params(1),
        name="layer_norm",
    )(x, g.reshape(1, d), b.reshape(1, d))


def _gelu_tanh(x):
    c = math.sqrt(2.0 / math.pi)
    return 0.5 * x * (1.0 + jnp.tanh(c * (x + 0.044715 * (x * x * x))))


def _compress_kernel(c_ref, pe_ref, w1_ref, w2_ref, o_ref):
    half = CMP_STRIDE * HEAD_DIM
    c = c_ref[0, 0, 0]
    n_chunks = c.shape[0]
    y1 = jnp.dot(c, w1_ref[0, 0:half, :], preferred_element_type=F32)
    y2 = jnp.dot(c, w1_ref[0, half:2 * half, :], preferred_element_type=F32)
    pe8 = jnp.broadcast_to(pe_ref[0], (8, 2 * half))
    pb = jnp.dot(pe8, w1_ref[0], preferred_element_type=F32)[0:1, :]
    h = y1 + pltpu.roll(y2, n_chunks - 1, 0) + pb
    a = _gelu_tanh(h).astype(BF16)
    out = jnp.dot(a, w2_ref[0], preferred_element_type=F32)
    row = lax.broadcasted_iota(jnp.int32, out.shape, 0)
    out = jnp.where(row < n_chunks - 1, out, 0.0)
    o_ref[0, 0, 0] = out.astype(o_ref.dtype)


def _compress(chunks, pe, w1, w2):
    _, b, g, n_chunks, width = chunks.shape
    return pl.pallas_call(
        _compress_kernel,
        grid=(2, b, g),
        in_specs=[pl.BlockSpec((1, 1, 1, n_chunks, width), lambda t, i, j: (t, i, j, 0, 0)),
                  pl.BlockSpec((1, 1, 2 * width), lambda t, i, j: (t, 0, 0)),
                  pl.BlockSpec((1, 2 * width, CMP_HIDDEN), lambda t, i, j: (t, 0, 0)),
                  pl.BlockSpec((1, CMP_HIDDEN, HEAD_DIM), lambda t, i, j: (t, 0, 0))],
        out_specs=pl.BlockSpec((1, 1, 1, n_chunks, HEAD_DIM), lambda t, i, j: (t, i, j, 0, 0)),
        out_shape=jax.ShapeDtypeStruct((2, b, g, n_chunks, HEAD_DIM), BF16),
        compiler_params=_params(3),
        name="compress",
    )(chunks, pe, w1, w2)


def _nt_dot(a, b):
    return lax.dot_general(a, b, (((1,), (1,)), ((), ())), preferred_element_type=F32)


def _dot_row_halves(a, b):
    half = a.shape[0] // 2
    return jnp.concatenate([jnp.dot(a[:half], b, preferred_element_type=F32),
                            jnp.dot(a[half:], b, preferred_element_type=F32)], axis=0)


def _lane_fold(x, op):
    out = x[:, 0:LANES]
    for c0 in range(LANES, x.shape[1], LANES):
        out = op(out, x[:, c0:c0 + LANES])
    return out


def _nsa_kernel(slopes_ref, q_ref, kc_ref, vc_ref, ks_ref, vs_ref, kw_ref, vw_ref,
                gate_ref, gain_ref, kext_ref, qext_ref, gsel_ref, *rest, n_cast):
    cast_src, o_ref, cast_dst = rest[:n_cast], rest[n_cast], rest[n_cast + 1:2 * n_cast + 1]
    ks_aug, vs_aug, vw_aug, s_ref, oslc_ref, psum_ref = rest[2 * n_cast + 1:]
    _cast_blocks(cast_src, cast_dst)
    g = pl.program_id(1)
    i = pl.program_id(2)
    tq = NSA_TQ
    rows = NSA_REP * tq
    t0 = i * tq

    @pl.when(i == 0)
    def _():
        ks_aug[:, 0:HEAD_DIM] = ks_ref[0]
        ks_aug[:, HEAD_DIM:] = kext_ref[...]
        ones = jnp.ones((ks_aug.shape[0], HEAD_DIM), BF16)
        vs_aug[:, 0:HEAD_DIM] = vs_ref[0]
        vs_aug[:, HEAD_DIM:] = ones
        vw_aug[:, 0:HEAD_DIM] = vw_ref[0]
        vw_aug[:, HEAD_DIM:] = ones

    q = q_ref[0]
    q_heads = [q[:, r * HEAD_DIM:(r + 1) * HEAD_DIM] for r in range(NSA_REP)]
    qs = jnp.concatenate(q_heads, axis=0)
    t_col = t0 + lax.broadcasted_iota(jnp.int32, (tq, 1), 0)
    head = [slice(r * tq, (r + 1) * tq) for r in range(NSA_REP)]

    n_cmp = kc_ref.shape[2]
    s_t = _nt_dot(kc_ref[0, 0], qs)
    n_col = lax.broadcasted_iota(jnp.int32, (n_cmp, 1), 0)
    t_row = t0 + (lax.broadcasted_iota(jnp.int32, (1, rows), 1) & (tq - 1))
    cmask = (n_col * CMP_STRIDE + (CMP_BLOCK - 1)) <= t_row
    s_t = jnp.where(cmask, s_t, NEG)
    m = jnp.max(s_t, axis=0, keepdims=True)
    e = jnp.where(cmask, jnp.exp2(s_t - m), 0.0)
    l = jnp.sum(e, axis=0, keepdims=True)
    p_t = e * (1.0 / jnp.maximum(l, 1e-30))
    p_sum_t = p_t[:, head[0]]
    for r in range(1, NSA_REP):
        p_sum_t = p_sum_t + p_t[:, head[r]]
    o_cmp = lax.dot_general(p_t.astype(BF16), vc_ref[0, 0], (((0,), (0,)), ((), ())),
                            preferred_element_type=F32)

    n_slc = n_cmp * CMP_STRIDE // SLC_BLOCK
    per = SLC_BLOCK // CMP_STRIDE
    for hl in range(tq // LANES):
        psum_ref[hl] = p_sum_t[:, hl * LANES:(hl + 1) * LANES]
    rows_k = [jnp.concatenate([psum_ref[hl, pl.ds(k, n_slc, stride=per), :] for hl in range(tq // LANES)],
                              axis=1) for k in range(per)]
    j_t = lax.broadcasted_iota(jnp.int32, (n_slc, tq), 0)
    prev_last = jnp.where(j_t == 0, 0.0, pltpu.roll(rows_k[per - 1], 1, 0))
    imp_t = rows_k[0] + rows_k[1] + rows_k[2] + 0.5 * (rows_k[3] + prev_last)
    t_t = t0 + lax.broadcasted_iota(jnp.int32, (n_slc, tq), 1)
    causal = j_t * SLC_BLOCK <= t_t
    tb = lax.shift_right_logical(t_t, int(math.log2(SLC_BLOCK)))
    forced = jnp.where(j_t == 0, 1e6, jnp.where(j_t == tb, 1e6, jnp.where(j_t == tb - 1, 1e6, imp_t)))
    score = jnp.where(causal, forced, NEG)
    rank = jnp.zeros((n_slc, tq), F32)
    for jp in range(n_slc):
        row = score[jp:jp + 1, :]
        ge = jnp.where(row >= score, 1.0, 0.0)
        gt = jnp.where(row > score, 1.0, 0.0)
        rank = rank + jnp.where(j_t > jp, ge, gt)
    sel_bias = jnp.where(causal, jnp.where(rank < SLC_TOPK, 0.0, NEG), NEG)

    sel_rows = jnp.concatenate([sel_bias, jnp.zeros((LANES - n_slc, tq), F32)], axis=0).T
    q_aug = jnp.concatenate(
        [jnp.concatenate([q_heads[r], (sel_rows + qext_ref[0, r:r + 1, :]).astype(BF16)], axis=1)
         for r in range(NSA_REP)], axis=0)
    ck = SLC_CK
    n_chunks = (t0 + tq + ck - 1) // ck

    def slc_branch(n):
        mx = jnp.full((rows, LANES), NEG, F32)
        for c in range(n):
            cols = slice(c * ck, (c + 1) * ck)
            t = _nt_dot(q_aug, ks_aug[cols, :])
            if c == n - 1:
                pos = c * ck + lax.broadcasted_iota(jnp.int32, (1, ck), 1)
                cb = jnp.where(pos <= t_col, 0.0, NEG)
                t = t + jnp.concatenate([cb] * NSA_REP, axis=0)
            s_ref[:, cols] = t
            mx = jnp.maximum(mx, _lane_fold(t, jnp.maximum))
        m_slc = jnp.max(mx, axis=-1, keepdims=True)
        acc = jnp.zeros((rows, 2 * HEAD_DIM), F32)
        for c in range(n):
            cols = slice(c * ck, (c + 1) * ck)
            p = jnp.exp2(s_ref[:, cols] - m_slc).astype(BF16)
            acc = acc + _dot_row_halves(p, vs_aug[cols, :])
        oslc_ref[...] = acc[:, 0:HEAD_DIM] * (1.0 / acc[:, HEAD_DIM:])

    for n in range(1, ks_aug.shape[0] // ck + 1):
        pl.when(n_chunks == n)(functools.partial(slc_branch, n))
    o_slc = oslc_ref[...]

    slope2 = [slopes_ref[g * NSA_REP + r] * LOG2E for r in range(NSA_REP)]
    k0 = pl.multiple_of(jnp.maximum(t0 - WINDOW, 0), tq)
    pos = k0 + lax.broadcasted_iota(jnp.int32, (1, WIN_SPAN), 1)
    dist = t_col - pos
    wbias = jnp.where(dist >= 0, jnp.where(dist < WINDOW, 0.0, NEG), NEG)
    posrel = (pos - t0).astype(F32)
    kband = kw_ref[0, pl.ds(k0, WIN_SPAN), :]
    vband = vw_aug[pl.ds(k0, WIN_SPAN), :]
    halves = []
    for pair in range(NSA_REP // 2):
        ps = []
        for r in (2 * pair, 2 * pair + 1):
            t = _nt_dot(q_heads[r], kband) + slope2[r] * posrel + wbias
            m = jnp.max(t, axis=-1, keepdims=True)
            ps.append(jnp.exp2(t - m).astype(BF16))
        halves.append(jnp.dot(jnp.concatenate(ps, axis=0), vband, preferred_element_type=F32))
    acc_w = jnp.concatenate(halves, axis=0)
    o_win = acc_w[:, 0:HEAD_DIM] * (1.0 / acc_w[:, HEAD_DIM:])

    sig = jax.nn.sigmoid(gate_ref[0]).astype(BF16)
    gates = jnp.dot(sig, gsel_ref[0], preferred_element_type=F32)
    for r in range(NSA_REP):
        gt = [gates[:, (3 * r + br) * LANES:(3 * r + br + 1) * LANES] for br in range(3)]
        o = gt[0] * o_cmp[head[r]] + gt[1] * o_slc[head[r]] + gt[2] * o_win[head[r]]
        o = o * lax.rsqrt(jnp.mean(o * o, axis=-1, keepdims=True) + RMS_EPS) * gain_ref[0, r:r + 1, :]
        o_ref[0, :, r * HEAD_DIM:(r + 1) * HEAD_DIM] = o.astype(o_ref.dtype)


def _nsa_attention(proj_a, cmp_kv, gates3, gain, slopes, kext, qext, gsel, cast_arrays):
    b, s, _ = proj_a.shape
    n_chunks = cmp_kv.shape[3]
    n_q = s // NSA_TQ
    cast_specs, cast_shapes = _cast_specs(cast_arrays, b * NSA_GROUPS * n_q,
                                          lambda bi, g, i: (bi * NSA_GROUPS + g) * n_q + i)
    col0 = NSA_WIDTH // HEAD_DIM

    def kv_spec(idx):
        return pl.BlockSpec((1, s, HEAD_DIM), lambda bi, g, i, idx=idx: (bi, 0, col0 + idx * NSA_GROUPS + g))

    cmp_spec = pl.BlockSpec((1, 1, n_chunks, HEAD_DIM), lambda bi, g, i: (bi, g, 0, 0))
    in_specs = [
        pl.BlockSpec(memory_space=pltpu.SMEM),
        pl.BlockSpec((1, NSA_TQ, NSA_REP * HEAD_DIM), lambda bi, g, i: (bi, i, g)),
        cmp_spec, cmp_spec,
        kv_spec(2), kv_spec(3), kv_spec(4), kv_spec(5),
        pl.BlockSpec((1, NSA_TQ, LANES), lambda bi, g, i: (bi, i, 0)),
        pl.BlockSpec((1, NSA_REP, HEAD_DIM), lambda bi, g, i: (g, 0, 0)),
        pl.BlockSpec(kext.shape, lambda bi, g, i: (0, 0)),
        pl.BlockSpec((1, NSA_REP, LANES), lambda bi, g, i: (g, 0, 0)),
        pl.BlockSpec((1,) + gsel.shape[1:], lambda bi, g, i: (g, 0, 0)),
    ] + cast_specs
    aug = pltpu.VMEM((s, 2 * HEAD_DIM), BF16)
    return pl.pallas_call(
        functools.partial(_nsa_kernel, n_cast=len(cast_arrays)),
        grid=(b, NSA_GROUPS, n_q),
        in_specs=in_specs,
        out_specs=[pl.BlockSpec((1, NSA_TQ, NSA_REP * HEAD_DIM), lambda bi, g, i: (bi, i, g))] + cast_specs,
        out_shape=[jax.ShapeDtypeStruct((b, s, NSA_WIDTH), BF16)] + cast_shapes,
        scratch_shapes=[aug, aug, aug, pltpu.VMEM((NSA_REP * NSA_TQ, s), F32),
                        pltpu.VMEM((NSA_REP * NSA_TQ, HEAD_DIM), F32),
                        pltpu.VMEM((NSA_TQ // LANES, n_chunks, LANES), F32)],
        compiler_params=_params(3),
        name="nsa_attention",
    )(slopes, proj_a, cmp_kv[0], cmp_kv[1], proj_a, proj_a, proj_a, proj_a, gates3, gain,
      kext, qext, gsel, *cast_arrays)


def _diff_kernel(lq1_ref, lk1_ref, lq2_ref, lk2_ref, q_ref, k_ref, v_ref, g_ref, kext_ref, qext_ref,
                 *rest, n_cast, lambda_init):
    cast_src, o_ref, cast_dst = rest[:n_cast], rest[n_cast], rest[n_cast + 1:2 * n_cast + 1]
    k_aug, s_ref, acc_ref = rest[2 * n_cast + 1:]
    _cast_blocks(cast_src, cast_dst)
    i = pl.program_id(2)
    tq, ck = DIFF_TQ, DIFF_CK
    t0 = i * tq

    @pl.when(i == 0)
    def _():
        for mi in range(2):
            k_aug[mi, :, 0:DIFF_DIM] = k_ref[0, :, mi * DIFF_DIM:(mi + 1) * DIFF_DIM]
            k_aug[mi, :, DIFF_DIM:] = kext_ref[...]

    q = q_ref[0]
    qext = jnp.broadcast_to(qext_ref[0], (tq, LANES)).astype(BF16)
    q_aug = [jnp.concatenate([q[:, mi * DIFF_DIM:(mi + 1) * DIFF_DIM], qext], axis=1) for mi in range(2)]
    t_col = t0 + lax.broadcasted_iota(jnp.int32, (tq, 1), 0)

    def branch(n):
        m_row = []
        for mi in range(2):
            mx = jnp.full((tq, LANES), NEG, F32)
            for c in range(n):
                cols = slice(c * ck, (c + 1) * ck)
                t = _nt_dot(q_aug[mi], k_aug[mi, cols, :])
                if c == n - 1:
                    pos = c * ck + lax.broadcasted_iota(jnp.int32, (1, ck), 1)
                    t = jnp.where(pos <= t_col, t, NEG)
                s_ref[mi, :, cols] = t
                mx = jnp.maximum(mx, _lane_fold(t, jnp.maximum))
            m_row.append(jnp.max(mx, axis=-1, keepdims=True))
        for mi in range(2):
            l_fold = jnp.zeros((tq, LANES), F32)
            acc = jnp.zeros((tq, DIFF_VDIM), F32)
            for c in range(n):
                cols = slice(c * ck, (c + 1) * ck)
                p = jnp.exp2(s_ref[mi, :, cols] - m_row[mi])
                l_fold = l_fold + _lane_fold(p, jnp.add)
                acc = acc + _dot_row_halves(p.astype(BF16), v_ref[0, cols, :])
            acc_ref[mi] = acc * (1.0 / jnp.sum(l_fold, axis=-1, keepdims=True))

    for n in range(1, k_aug.shape[1] // ck + 1):
        pl.when(i + 1 == n)(functools.partial(branch, n))

    lam = (jnp.exp(jnp.sum(lq1_ref[...] * lk1_ref[...], axis=-1, keepdims=True))
           - jnp.exp(jnp.sum(lq2_ref[...] * lk2_ref[...], axis=-1, keepdims=True))
           + lambda_init)
    o = acc_ref[0] - lam * acc_ref[1]
    o = o * lax.rsqrt(jnp.mean(o * o, axis=-1, keepdims=True) + RMS_EPS) * g_ref[...]
    o_ref[0] = (o * (1.0 - lambda_init)).astype(o_ref.dtype)


def _diff_attention(proj_b, lq1, lk1, lq2, lk2, gain, kext, qext, lambda_init, cast_arrays):
    b, s, _ = proj_b.shape
    n_q = s // DIFF_TQ
    cast_specs, cast_shapes = _cast_specs(cast_arrays, b * DIFF_HEADS * n_q,
                                          lambda bi, h, i: (bi * DIFF_HEADS + h) * n_q + i)
    qcol, kcol, vcol = 0, DIFF_HEADS, 2 * DIFF_HEADS
    vec = pl.BlockSpec((1, DIFF_DIM), lambda bi, h, i: (0, 0))
    in_specs = [
        vec, vec, vec, vec,
        pl.BlockSpec((1, DIFF_TQ, DIFF_VDIM), lambda bi, h, i: (bi, i, qcol + h)),
        pl.BlockSpec((1, s, DIFF_VDIM), lambda bi, h, i: (bi, 0, kcol + h)),
        pl.BlockSpec((1, s, DIFF_VDIM), lambda bi, h, i: (bi, 0, vcol + h)),
        pl.BlockSpec((1, DIFF_VDIM), lambda bi, h, i: (0, 0)),
        pl.BlockSpec(kext.shape, lambda bi, h, i: (0, 0)),
        pl.BlockSpec((1, 1, LANES), lambda bi, h, i: (h, 0, 0)),
    ] + cast_specs
    return pl.pallas_call(
        functools.partial(_diff_kernel, n_cast=len(cast_arrays), lambda_init=lambda_init),
        grid=(b, DIFF_HEADS, n_q),
        in_specs=in_specs,
        out_specs=[pl.BlockSpec((1, DIFF_TQ, DIFF_VDIM), lambda bi, h, i: (bi, i, h))] + cast_specs,
        out_shape=[jax.ShapeDtypeStruct((b, s, DIFF_WIDTH), BF16)] + cast_shapes,
        scratch_shapes=[pltpu.VMEM((2, s, 2 * DIFF_DIM), BF16), pltpu.VMEM((2, DIFF_TQ, s), F32),
                        pltpu.VMEM((2, DIFF_TQ, DIFF_VDIM), F32)],
        compiler_params=_params(3),
        name="diff_attention",
    )(lq1.reshape(1, -1), lk1.reshape(1, -1), lq2.reshape(1, -1), lk2.reshape(1, -1),
      proj_b, proj_b, proj_b, gain.reshape(1, -1), kext, qext, *cast_arrays)


def _alibi_slopes(n_heads):
    return np.array([2.0 ** (-8.0 * (h + 1) / n_heads) for h in range(n_heads)], np.float32)


def _round_to_bf16(x):
    bits = np.array(x, np.float32).view(np.uint32)
    bits = (bits + np.uint32(0x7FFF) + ((bits >> np.uint32(16)) & np.uint32(1))) & np.uint32(0xFFFF0000)
    return np.float64(bits.view(np.float32))


def _bf16_pieces(x, n):
    out, rem = [], np.float64(x)
    for _ in range(n):
        piece = _round_to_bf16(rem)
        out.append(piece)
        rem = rem - piece
    return out


def _key_ext(seq):
    ext = np.zeros((seq, LANES), np.float32)
    pos = np.arange(seq)
    ext[pos, pos // SLC_BLOCK] = 1.0
    centred = pos - seq // 2
    hi = (centred // 256) * 256
    lo = centred - hi
    n_slc = seq // SLC_BLOCK
    ext[:, n_slc:n_slc + SLOPE_PIECES] = hi[:, None]
    ext[:, n_slc + SLOPE_PIECES:n_slc + 2 * SLOPE_PIECES] = lo[:, None]
    return ext


def _query_ext(slopes, n_slc):
    ext = np.zeros((len(slopes), LANES), np.float32)
    for h, slope in enumerate(slopes):
        pieces = _bf16_pieces(np.float64(slope) * LOG2E, SLOPE_PIECES)
        ext[h, n_slc:n_slc + SLOPE_PIECES] = pieces
        ext[h, n_slc + SLOPE_PIECES:n_slc + 2 * SLOPE_PIECES] = pieces
    return ext


def _gate_select():
    per = 3 * NSA_REP
    sel = np.zeros((NSA_GROUPS, LANES, per * LANES), np.float32)
    for g in range(NSA_GROUPS):
        for k in range(per):
            sel[g, g * per + k, k * LANES:(k + 1) * LANES] = 1.0
    return sel


def _layer(x, w_in, cmp_pe_k, cmp_w1_k, cmp_w2_k, cmp_pe_v, cmp_w1_v, cmp_w2_v, nsa_out_g,
           lambda_q1, lambda_k1, lambda_q2, lambda_k2, diff_subln_g, w_out, ln1_g, ln1_b,
           w_ff1, w_ff2, ln2_g, ln2_b, lambda_init):
    b, s, d = x.shape
    t = b * s
    x2 = x.reshape(t, d)

    w_t = jnp.swapaxes(w_in, 0, 1)
    x_bf, gates = _gate_cast(x2, w_t, row0=GATE_START)
    proj_a, chunks = _inproj(x_bf, w_t, BF16, row0=0, n=PROJ_A_COLS, scaled_cols=NSA_WIDTH,
                             chunked=(NSA_WIDTH, 2, s), name="in_proj_nsa")
    proj_b = _inproj(x_bf, w_t, BF16, row0=GATE_START, n=PROJ_B_COLS, scaled_cols=DIFF_WIDTH,
                     shift=GATE_COLS, name="in_proj_diff")
    proj_a = proj_a.reshape(b, s, PROJ_A_COLS)
    proj_b = proj_b.reshape(b, s, PROJ_B_COLS)

    pe = jnp.stack([cmp_pe_k, cmp_pe_v]).reshape(2, 1, CMP_BLOCK * HEAD_DIM).astype(BF16)
    w1 = jnp.stack([cmp_w1_k, cmp_w1_v]).astype(BF16)
    w2 = jnp.stack([cmp_w2_k, cmp_w2_v]).astype(BF16)
    cmp_kv = _compress(chunks, pe, w1, w2)

    n_slc = s // SLC_BLOCK
    kext = jnp.asarray(_key_ext(s)).astype(BF16)
    o_nsa, w_ff1_bf, w_out_bf = _nsa_attention(
        proj_a, cmp_kv, gates.reshape(b, s, LANES),
        nsa_out_g.reshape(NSA_GROUPS, NSA_REP, HEAD_DIM),
        jnp.asarray(_alibi_slopes(NSA_HEADS)), kext,
        jnp.asarray(_query_ext(_alibi_slopes(NSA_HEADS), n_slc)).reshape(NSA_GROUPS, NSA_REP, LANES),
        jnp.asarray(_gate_select()).astype(BF16), [w_ff1, w_out])
    o_diff, w_ff2_bf = _diff_attention(
        proj_b, lambda_q1, lambda_k1, lambda_q2, lambda_k2, diff_subln_g, kext,
        jnp.asarray(_query_ext(_alibi_slopes(DIFF_HEADS), n_slc)).reshape(DIFF_HEADS, 1, LANES),
        lambda_init, [w_ff2])

    h1 = _outproj(o_nsa.reshape(t, NSA_WIDTH), o_diff.reshape(t, DIFF_WIDTH), w_out_bf, x2)
    x1, x1_bf = _layer_norm(h1, ln1_g, ln1_b, (F32, BF16))
    hid = _ff1(x1_bf, w_ff1_bf)
    h2 = _ff2(hid, w_ff2_bf, x1)
    (out,) = _layer_norm(h2, ln2_g, ln2_b, (F32,))
    return out.reshape(b, s, d)


def kernel(x, w_in, cmp_pe_k, cmp_w1_k, cmp_w2_k, cmp_pe_v, cmp_w1_v, cmp_w2_v, nsa_out_g,
           lambda_q1, lambda_k1, lambda_q2, lambda_k2, diff_subln_g, w_out, ln1_g, ln1_b,
           w_ff1, w_ff2, ln2_g, ln2_b):
    for l in range(DEPTH):
        lambda_init = 0.8 - 0.6 * math.exp(-0.3 * l)
        x = _layer(x, w_in[l], cmp_pe_k[l], cmp_w1_k[l], cmp_w2_k[l], cmp_pe_v[l], cmp_w1_v[l],
                   cmp_w2_v[l], nsa_out_g[l], lambda_q1[l], lambda_k1[l], lambda_q2[l],
                   lambda_k2[l], diff_subln_g[l], w_out[l], ln1_g[l], ln1_b[l], w_ff1[l],
                   w_ff2[l], ln2_g[l], ln2_b[l], lambda_init)
    return x
```

```python
import functools
import math

import numpy as np
import jax
import jax.numpy as jnp
from jax import lax
from jax.experimental import pallas as pl
from jax.experimental.pallas import tpu as pltpu

F32 = jnp.float32
BF16 = jnp.bfloat16

D_MODEL = 4096
HEAD_DIM = 128
NSA_HEADS = 16
NSA_GROUPS = 4
NSA_REP = 4
CMP_BLOCK = 32
CMP_STRIDE = 16
CMP_HIDDEN = 256
SLC_BLOCK = 64
SLC_TOPK = 16
WINDOW = 512
DIFF_HEADS = 8
DIFF_DIM = 128
DIFF_VDIM = 256
NSA_WIDTH = NSA_HEADS * HEAD_DIM
KV_WIDTH = 6 * NSA_GROUPS * HEAD_DIM
GATE_COLS = 3 * NSA_HEADS
GATE_START = NSA_WIDTH + KV_WIDTH
DIFF_WIDTH = DIFF_HEADS * DIFF_VDIM
PROJ_A_COLS = GATE_START
PROJ_B_COLS = 3 * DIFF_WIDTH
LN_EPS = 1e-5
RMS_EPS = 1e-6
NEG = -1e30
DEPTH = 1
DEEPNORM_ALPHA = (2.0 * DEPTH) ** 0.25
ATT_SCALE = HEAD_DIM ** -0.5
LOG2E = 1.4426950408889634
ATT_SCALE2 = ATT_SCALE * LOG2E

LANES = 128
BF16_SUBLANES = 16
VMEM_LIMIT_BYTES = 56 * 1024 * 1024

MM_BM = 1024
MM_BN = 1024
INPROJ_BN = 512
OUTPROJ_BN = 512
GATE_BM = 512
FF2_BK = 4096
CAST_COLS = 512
LN_ROWS = 256
NSA_TQ = 256
SLC_CK = 512
WIN_SPAN = WINDOW + NSA_TQ
DIFF_TQ = 512
DIFF_CK = 512
SLOPE_PIECES = 4


def _params(n_axes):
    return pltpu.CompilerParams(dimension_semantics=("arbitrary",) * n_axes,
                                vmem_limit_bytes=VMEM_LIMIT_BYTES)


def _load_weight_rows(b_ref, tail_ref, shift, cols):
    if shift:
        return jnp.concatenate([b_ref[shift:, cols], tail_ref[0:shift, cols]], axis=0)
    return b_ref[:, cols]


def _inproj_kernel(*refs, shift, n_inner, scaled_tiles, chunked_tiles):
    refs = list(refs)
    a_ref, b_ref = refs[:2]
    tail_ref = refs[2] if shift else None
    o_ref = refs[3 if shift else 2]
    wb_ref = refs[-1]
    j = pl.program_id(0)
    i = pl.program_id(1)

    def prepare_slice():
        width = wb_ref.shape[1] // n_inner
        cols = pl.ds(pl.multiple_of(i * width, width), width)
        blk = _load_weight_rows(b_ref, tail_ref, shift, cols)
        blk = blk * jnp.where(j < scaled_tiles, ATT_SCALE2, 1.0)
        wb_ref[j % 2, cols, :] = blk.T.astype(BF16)

    pl.when(j == 0)(prepare_slice)

    @pl.when(j > 0)
    def _():
        acc = jnp.dot(a_ref[...], wb_ref[(j + 1) % 2], preferred_element_type=F32)
        o_ref[...] = acc.astype(o_ref.dtype)
        if chunked_tiles:
            for grp in range(refs[-2].shape[0]):
                refs[-2][grp] = acc[:, grp * HEAD_DIM:(grp + 1) * HEAD_DIM]
        prepare_slice()

    if chunked_tiles:
        first, n_ch = chunked_tiles
        c_ref, stage_ref = refs[-3], refs[-2]

        @pl.when((j - 1 >= first) & (j - 1 < first + n_ch))
        def _():
            n_groups, n_rows = c_ref.shape[2], c_ref.shape[3]
            for grp in range(n_groups):
                for l in range(CMP_STRIDE):
                    c_ref[0, 0, grp, :, l * HEAD_DIM:(l + 1) * HEAD_DIM] = stage_ref[
                        grp, pl.ds(l, n_rows, stride=CMP_STRIDE), :].astype(c_ref.dtype)


def _inproj(a, wt, out_dtype, *, row0, n, scaled_cols, shift=0, chunked=None, bn=INPROJ_BN,
            bm=MM_BM, name):
    m, k = a.shape
    n_tiles = n // bn
    n_inner = m // bm
    assert k % ((m // bm) * LANES) == 0
    jb = row0 // bn

    def tile(j):
        return jnp.minimum(j, n_tiles - 1)

    def row_tile(j, i):
        return jnp.where(j == 0, 0, i)

    in_specs = [pl.BlockSpec((bm, k), lambda j, i: (row_tile(j, i), 0)),
                pl.BlockSpec((bn, k), lambda j, i: (jb + tile(j), 0))]
    args = [a, wt]
    if shift:
        per = bn // LANES
        tb = row0 // LANES
        in_specs.append(pl.BlockSpec((LANES, k), lambda j, i: (tb + per * (tile(j) + 1), 0)))
        args.append(wt)
    out_specs = [pl.BlockSpec((bm, bn), lambda j, i: (row_tile(j, i), jnp.maximum(j - 1, 0)))]
    out_shape = [jax.ShapeDtypeStruct((m, n), out_dtype)]
    scratch = [pltpu.VMEM((2, k, bn), BF16)]
    chunked_tiles = None
    if chunked:
        col0, n_kinds, seq = chunked
        first = col0 // bn
        per_seq = seq // bm
        n_rows = bm // CMP_STRIDE
        chunked_tiles = (first, n_kinds)

        def chunk_block(j, i):
            t = j - 1
            ii = jnp.where(t < first, 0, jnp.where(t >= first + n_kinds, n_inner - 1, i))
            return (jnp.clip(t - first, 0, n_kinds - 1), ii // per_seq, 0, ii % per_seq, 0)

        out_specs.append(pl.BlockSpec((1, 1, bn // HEAD_DIM, n_rows, CMP_STRIDE * HEAD_DIM), chunk_block))
        out_shape.append(jax.ShapeDtypeStruct(
            (n_kinds, m // seq, bn // HEAD_DIM, seq // CMP_STRIDE, CMP_STRIDE * HEAD_DIM), out_dtype))
        scratch.insert(0, pltpu.VMEM((bn // HEAD_DIM, bm, HEAD_DIM), F32))
    res = pl.pallas_call(
        functools.partial(_inproj_kernel, shift=shift, n_inner=n_inner,
                          scaled_tiles=scaled_cols // bn, chunked_tiles=chunked_tiles),
        grid=(n_tiles + 1, n_inner),
        in_specs=in_specs,
        out_specs=out_specs,
        out_shape=out_shape,
        scratch_shapes=scratch,
        compiler_params=_params(2),
        name=name,
    )(*args)
    return res if chunked else res[0]


def _gate_cast_kernel(x_ref, wt_ref, xb_ref, g_ref, wb_ref):
    @pl.when(pl.program_id(0) == 0)
    def _():
        k = wb_ref.shape[0]
        for c0 in range(0, k, CAST_COLS):
            wb_ref[c0:c0 + CAST_COLS, :] = wt_ref[:, c0:c0 + CAST_COLS].T.astype(BF16)

    xb = x_ref[...].astype(BF16)
    xb_ref[...] = xb
    g_ref[...] = jnp.dot(xb, wb_ref[...], preferred_element_type=F32)


def _gate_cast(x, wt, *, row0, bm=GATE_BM):
    m, k = x.shape
    return pl.pallas_call(
        _gate_cast_kernel,
        grid=(m // bm,),
        in_specs=[pl.BlockSpec((bm, k), lambda i: (i, 0)),
                  pl.BlockSpec((LANES, k), lambda i: (row0 // LANES, 0))],
        out_specs=[pl.BlockSpec((bm, k), lambda i: (i, 0)),
                   pl.BlockSpec((bm, LANES), lambda i: (i, 0))],
        out_shape=[jax.ShapeDtypeStruct((m, k), BF16), jax.ShapeDtypeStruct((m, LANES), F32)],
        scratch_shapes=[pltpu.VMEM((k, LANES), BF16)],
        compiler_params=_params(1),
        name="gate_proj_cast",
    )(x, wt)


def _mm_kernel(a_ref, b_ref, o_ref):
    acc = jnp.dot(a_ref[...], b_ref[...], preferred_element_type=F32)
    o_ref[...] = jnp.square(jnp.maximum(acc, 0.0)).astype(o_ref.dtype)


def _ff1(a, b, *, bm=MM_BM, bn=MM_BN):
    m, k = a.shape
    _, n = b.shape
    return pl.pallas_call(
        _mm_kernel,
        grid=(m // bm, n // bn),
        in_specs=[pl.BlockSpec((bm, k), lambda i, j: (i, 0)),
                  pl.BlockSpec((k, bn), lambda i, j: (0, j))],
        out_specs=pl.BlockSpec((bm, bn), lambda i, j: (i, j)),
        out_shape=jax.ShapeDtypeStruct((m, n), BF16),
        compiler_params=_params(2),
        name="ff1",
    )(a, b)


def _outproj_kernel(a1_ref, a2_ref, b_ref, res_ref, o_ref):
    half = a1_ref.shape[1]
    acc = jnp.dot(a1_ref[...], b_ref[0:half, :], preferred_element_type=F32)
    acc = acc + jnp.dot(a2_ref[...], b_ref[half:2 * half, :], preferred_element_type=F32)
    o_ref[...] = DEEPNORM_ALPHA * res_ref[...] + acc


def _outproj(a1, a2, b, res, *, bm=MM_BM, bn=OUTPROJ_BN):
    m, k1 = a1.shape
    _, n = b.shape
    return pl.pallas_call(
        _outproj_kernel,
        grid=(m // bm, n // bn),
        in_specs=[pl.BlockSpec((bm, k1), lambda i, j: (i, 0)),
                  pl.BlockSpec((bm, k1), lambda i, j: (i, 0)),
                  pl.BlockSpec((2 * k1, bn), lambda i, j: (0, j)),
                  pl.BlockSpec((bm, bn), lambda i, j: (i, j))],
        out_specs=pl.BlockSpec((bm, bn), lambda i, j: (i, j)),
        out_shape=jax.ShapeDtypeStruct((m, n), F32),
        compiler_params=_params(2),
        name="outproj",
    )(a1, a2, b, res)


def _ff2_kernel(a_ref, b_ref, res_ref, o_ref):
    @pl.when(pl.program_id(2) == 0)
    def _():
        o_ref[...] = DEEPNORM_ALPHA * res_ref[...]

    o_ref[...] += jnp.dot(a_ref[...], b_ref[...], preferred_element_type=F32)


def _ff2(a, b, res, *, bm=MM_BM, bn=MM_BN, bk=FF2_BK):
    m, k = a.shape
    _, n = b.shape
    return pl.pallas_call(
        _ff2_kernel,
        grid=(m // bm, n // bn, k // bk),
        in_specs=[pl.BlockSpec((bm, bk), lambda i, j, kk: (i, kk)),
                  pl.BlockSpec((bk, bn), lambda i, j, kk: (kk, j)),
                  pl.BlockSpec((bm, bn), lambda i, j, kk: (i, j))],
        out_specs=pl.BlockSpec((bm, bn), lambda i, j, kk: (i, j)),
        out_shape=jax.ShapeDtypeStruct((m, n), F32),
        compiler_params=_params(3),
        name="ff2",
    )(a, b, res)


def _cast_blocks(src_refs, dst_refs):
    for src, dst in zip(src_refs, dst_refs):
        dst[...] = src[...].astype(dst.dtype)


def _cast_specs(arrays, n_steps, step_of):
    specs, shapes = [], []
    for arr in arrays:
        rows, cols = arr.shape
        assert rows % (n_steps * BF16_SUBLANES) == 0
        blk = (rows // n_steps, cols)
        specs.append(pl.BlockSpec(blk, lambda *ids: (step_of(*ids), 0)))
        shapes.append(jax.ShapeDtypeStruct((rows, cols), BF16))
    return specs, shapes


def _ln_kernel(x_ref, g_ref, b_ref, *o_refs):
    x = x_ref[...]
    mu = jnp.mean(x, axis=-1, keepdims=True)
    xc = x - mu
    var = jnp.mean(xc * xc, axis=-1, keepdims=True)
    y = xc * lax.rsqrt(var + LN_EPS) * g_ref[...] + b_ref[...]
    for o_ref in o_refs:
        o_ref[...] = y.astype(o_ref.dtype)


def _layer_norm(x, g, b, out_dtypes):
    m, d = x.shape
    row_spec = pl.BlockSpec((LN_ROWS, d), lambda i: (i, 0))
    vec_spec = pl.BlockSpec((1, d), lambda i: (0, 0))
    return pl.pallas_call(
        _ln_kernel,
        grid=(m // LN_ROWS,),
        in_specs=[row_spec, vec_spec, vec_spec],
        out_specs=[row_spec for _ in out_dtypes],
        out_shape=[jax.ShapeDtypeStruct((m, d), dt) for dt in out_dtypes],
        compiler_params=_params(1),
        name="layer_norm",
    )(x, g.reshape(1, d), b.reshape(1, d))


def _gelu_tanh(x):
    c = math.sqrt(2.0 / math.pi)
    return 0.5 * x * (1.0 + jnp.tanh(c * (x + 0.044715 * (x * x * x))))


def _compress_kernel(c_ref, pe_ref, w1_ref, w2_ref, o_ref):
    half = CMP_STRIDE * HEAD_DIM
    nb, ng, n_chunks, _ = c_ref.shape[1:]
    rows = nb * ng * n_chunks
    c = c_ref[0].reshape(rows, half)
    y1 = jnp.dot(c, w1_ref[0, 0:half, :], preferred_element_type=F32)
    y2 = jnp.dot(c, w1_ref[0, half:2 * half, :], preferred_element_type=F32)
    pe8 = jnp.broadcast_to(pe_ref[0], (8, 2 * half))
    pb = jnp.dot(pe8, w1_ref[0], preferred_element_type=F32)[0:1, :]
    h = y1 + pltpu.roll(y2, rows - 1, 0) + pb
    a = _gelu_tanh(h).astype(BF16)
    out = jnp.dot(a, w2_ref[0], preferred_element_type=F32)
    row = lax.broadcasted_iota(jnp.int32, out.shape, 0)
    out = jnp.where((row & (n_chunks - 1)) < n_chunks - 1, out, 0.0)
    o_ref[0] = out.astype(o_ref.dtype).reshape(nb, ng, n_chunks, HEAD_DIM)


def _compress(chunks, pe, w1, w2):
    _, b, g, n_chunks, width = chunks.shape
    assert n_chunks & (n_chunks - 1) == 0
    return pl.pallas_call(
        _compress_kernel,
        grid=(2,),
        in_specs=[pl.BlockSpec((1, b, g, n_chunks, width), lambda t: (t, 0, 0, 0, 0)),
                  pl.BlockSpec((1, 1, 2 * width), lambda t: (t, 0, 0)),
                  pl.BlockSpec((1, 2 * width, CMP_HIDDEN), lambda t: (t, 0, 0)),
                  pl.BlockSpec((1, CMP_HIDDEN, HEAD_DIM), lambda t: (t, 0, 0))],
        out_specs=pl.BlockSpec((1, b, g, n_chunks, HEAD_DIM), lambda t: (t, 0, 0, 0, 0)),
        out_shape=jax.ShapeDtypeStruct((2, b, g, n_chunks, HEAD_DIM), BF16),
        compiler_params=_params(1),
        name="compress",
    )(chunks, pe, w1, w2)


def _nt_dot(a, b):
    return lax.dot_general(a, b, (((1,), (1,)), ((), ())), preferred_element_type=F32)


def _dot_row_halves(a, b):
    half = a.shape[0] // 2
    return jnp.concatenate([jnp.dot(a[:half], b, preferred_element_type=F32),
                            jnp.dot(a[half:], b, preferred_element_type=F32)], axis=0)


def _lane_fold(x, op):
    out = x[:, 0:LANES]
    for c0 in range(LANES, x.shape[1], LANES):
        out = op(out, x[:, c0:c0 + LANES])
    return out


def _nsa_kernel(slopes_ref, q_ref, kc_ref, vc_ref, ks_ref, vs_ref, kw_ref, vw_ref,
                gate_ref, gain_ref, kext_ref, qext_ref, gsel_ref, *rest, n_cast):
    cast_src, o_ref, cast_dst = rest[:n_cast], rest[n_cast], rest[n_cast + 1:2 * n_cast + 1]
    ks_aug, vs_aug, vw_aug, s_ref, oslc_ref, psum_ref = rest[2 * n_cast + 1:]
    _cast_blocks(cast_src, cast_dst)
    g = pl.program_id(1)
    i = pl.program_id(2)
    tq = NSA_TQ
    rows = NSA_REP * tq
    t0 = i * tq

    @pl.when(i == 0)
    def _():
        ks_aug[:, 0:HEAD_DIM] = ks_ref[0]
        ks_aug[:, HEAD_DIM:] = kext_ref[...]
        ones = jnp.ones((ks_aug.shape[0], HEAD_DIM), BF16)
        vs_aug[:, 0:HEAD_DIM] = vs_ref[0]
        vs_aug[:, HEAD_DIM:] = ones
        vw_aug[:, 0:HEAD_DIM] = vw_ref[0]
        vw_aug[:, HEAD_DIM:] = ones

    q = q_ref[0]
    q_heads = [q[:, r * HEAD_DIM:(r + 1) * HEAD_DIM] for r in range(NSA_REP)]
    qs = jnp.concatenate(q_heads, axis=0)
    t_col = t0 + lax.broadcasted_iota(jnp.int32, (tq, 1), 0)
    head = [slice(r * tq, (r + 1) * tq) for r in range(NSA_REP)]

    n_cmp = kc_ref.shape[2]
    s_t = _nt_dot(kc_ref[0, 0], qs)
    n_col = lax.broadcasted_iota(jnp.int32, (n_cmp, 1), 0)
    t_row = t0 + (lax.broadcasted_iota(jnp.int32, (1, rows), 1) & (tq - 1))
    cmask = (n_col * CMP_STRIDE + (CMP_BLOCK - 1)) <= t_row
    s_t = jnp.where(cmask, s_t, NEG)
    m = jnp.max(s_t, axis=0, keepdims=True)
    e = jnp.where(cmask, jnp.exp2(s_t - m), 0.0)
    l = jnp.sum(e, axis=0, keepdims=True)
    p_t = e * (1.0 / jnp.maximum(l, 1e-30))
    p_sum_t = p_t[:, head[0]]
    for r in range(1, NSA_REP):
        p_sum_t = p_sum_t + p_t[:, head[r]]
    o_cmp = lax.dot_general(p_t.astype(BF16), vc_ref[0, 0], (((0,), (0,)), ((), ())),
                            preferred_element_type=F32)

    n_slc = n_cmp * CMP_STRIDE // SLC_BLOCK
    per = SLC_BLOCK // CMP_STRIDE
    for hl in range(tq // LANES):
        psum_ref[hl] = p_sum_t[:, hl * LANES:(hl + 1) * LANES]
    rows_k = [jnp.concatenate([psum_ref[hl, pl.ds(k, n_slc, stride=per), :] for hl in range(tq // LANES)],
                              axis=1) for k in range(per)]
    j_t = lax.broadcasted_iota(jnp.int32, (n_slc, tq), 0)
    prev_last = jnp.where(j_t == 0, 0.0, pltpu.roll(rows_k[per - 1], 1, 0))
    imp_t = rows_k[0] + rows_k[1] + rows_k[2] + 0.5 * (rows_k[3] + prev_last)
    t_t = t0 + lax.broadcasted_iota(jnp.int32, (n_slc, tq), 1)
    causal = j_t * SLC_BLOCK <= t_t
    tb = lax.shift_right_logical(t_t, int(math.log2(SLC_BLOCK)))
    forced = jnp.where(j_t == 0, 1e6, jnp.where(j_t == tb, 1e6, jnp.where(j_t == tb - 1, 1e6, imp_t)))
    score = jnp.where(causal, forced, NEG)
    rank = jnp.zeros((n_slc, tq), F32)
    for jp in range(n_slc):
        row = score[jp:jp + 1, :]
        ge = jnp.where(row >= score, 1.0, 0.0)
        gt = jnp.where(row > score, 1.0, 0.0)
        rank = rank + jnp.where(j_t > jp, ge, gt)
    sel_bias = jnp.where(causal, jnp.where(rank < SLC_TOPK, 0.0, NEG), NEG)

    sel_rows = jnp.concatenate([sel_bias, jnp.zeros((LANES - n_slc, tq), F32)], axis=0).T
    q_aug = jnp.concatenate(
        [jnp.concatenate([q_heads[r], (sel_rows + qext_ref[0, r:r + 1, :]).astype(BF16)], axis=1)
         for r in range(NSA_REP)], axis=0)
    ck = SLC_CK
    n_chunks = (t0 + tq + ck - 1) // ck

    def slc_branch(n):
        mx = jnp.full((rows, LANES), NEG, F32)
        for c in range(n):
            cols = slice(c * ck, (c + 1) * ck)
            t = _nt_dot(q_aug, ks_aug[cols, :])
            if c == n - 1:
                pos = c * ck + lax.broadcasted_iota(jnp.int32, (1, ck), 1)
                cb = jnp.where(pos <= t_col, 0.0, NEG)
                t = t + jnp.concatenate([cb] * NSA_REP, axis=0)
            s_ref[:, cols] = t
            mx = jnp.maximum(mx, _lane_fold(t, jnp.maximum))
        m_slc = jnp.max(mx, axis=-1, keepdims=True)
        acc = jnp.zeros((rows, 2 * HEAD_DIM), F32)
        for c in range(n):
            cols = slice(c * ck, (c + 1) * ck)
            p = jnp.exp2(s_ref[:, cols] - m_slc).astype(BF16)
            acc = acc + _dot_row_halves(p, vs_aug[cols, :])
        oslc_ref[...] = acc[:, 0:HEAD_DIM] * (1.0 / acc[:, HEAD_DIM:])

    for n in range(1, ks_aug.shape[0] // ck + 1):
        pl.when(n_chunks == n)(functools.partial(slc_branch, n))
    o_slc = oslc_ref[...]

    slope2 = [slopes_ref[g * NSA_REP + r] * LOG2E for r in range(NSA_REP)]
    k0 = pl.multiple_of(jnp.maximum(t0 - WINDOW, 0), tq)
    pos = k0 + lax.broadcasted_iota(jnp.int32, (1, WIN_SPAN), 1)
    dist = t_col - pos
    wbias = jnp.where(dist >= 0, jnp.where(dist < WINDOW, 0.0, NEG), NEG)
    posrel = (pos - t0).astype(F32)
    kband = kw_ref[0, pl.ds(k0, WIN_SPAN), :]
    vband = vw_aug[pl.ds(k0, WIN_SPAN), :]
    halves = []
    for pair in range(NSA_REP // 2):
        ps = []
        for r in (2 * pair, 2 * pair + 1):
            t = _nt_dot(q_heads[r], kband) + slope2[r] * posrel + wbias
            m = jnp.max(t, axis=-1, keepdims=True)
            ps.append(jnp.exp2(t - m).astype(BF16))
        halves.append(jnp.dot(jnp.concatenate(ps, axis=0), vband, preferred_element_type=F32))
    acc_w = jnp.concatenate(halves, axis=0)
    o_win = acc_w[:, 0:HEAD_DIM] * (1.0 / acc_w[:, HEAD_DIM:])

    sig = jax.nn.sigmoid(gate_ref[0]).astype(BF16)
    gates = jnp.dot(sig, gsel_ref[0], preferred_element_type=F32)
    for r in range(NSA_REP):
        gt = [gates[:, (3 * r + br) * LANES:(3 * r + br + 1) * LANES] for br in range(3)]
        o = gt[0] * o_cmp[head[r]] + gt[1] * o_slc[head[r]] + gt[2] * o_win[head[r]]
        o = o * lax.rsqrt(jnp.mean(o * o, axis=-1, keepdims=True) + RMS_EPS) * gain_ref[0, r:r + 1, :]
        o_ref[0, :, r * HEAD_DIM:(r + 1) * HEAD_DIM] = o.astype(o_ref.dtype)


def _nsa_attention(proj_a, cmp_kv, gates3, gain, slopes, kext, qext, gsel, cast_arrays):
    b, s, _ = proj_a.shape
    n_chunks = cmp_kv.shape[3]
    n_q = s // NSA_TQ
    cast_specs, cast_shapes = _cast_specs(cast_arrays, b * NSA_GROUPS * n_q,
                                          lambda bi, g, i: (bi * NSA_GROUPS + g) * n_q + i)
    col0 = NSA_WIDTH // HEAD_DIM

    def kv_spec(idx):
        return pl.BlockSpec((1, s, HEAD_DIM), lambda bi, g, i, idx=idx: (bi, 0, col0 + idx * NSA_GROUPS + g))

    cmp_spec = pl.BlockSpec((1, 1, n_chunks, HEAD_DIM), lambda bi, g, i: (bi, g, 0, 0))
    in_specs = [
        pl.BlockSpec(memory_space=pltpu.SMEM),
        pl.BlockSpec((1, NSA_TQ, NSA_REP * HEAD_DIM), lambda bi, g, i: (bi, i, g)),
        cmp_spec, cmp_spec,
        kv_spec(2), kv_spec(3), kv_spec(4), kv_spec(5),
        pl.BlockSpec((1, NSA_TQ, LANES), lambda bi, g, i: (bi, i, 0)),
        pl.BlockSpec((1, NSA_REP, HEAD_DIM), lambda bi, g, i: (g, 0, 0)),
        pl.BlockSpec(kext.shape, lambda bi, g, i: (0, 0)),
        pl.BlockSpec((1, NSA_REP, LANES), lambda bi, g, i: (g, 0, 0)),
        pl.BlockSpec((1,) + gsel.shape[1:], lambda bi, g, i: (g, 0, 0)),
    ] + cast_specs
    aug = pltpu.VMEM((s, 2 * HEAD_DIM), BF16)
    return pl.pallas_call(
        functools.partial(_nsa_kernel, n_cast=len(cast_arrays)),
        grid=(b, NSA_GROUPS, n_q),
        in_specs=in_specs,
        out_specs=[pl.BlockSpec((1, NSA_TQ, NSA_REP * HEAD_DIM), lambda bi, g, i: (bi, i, g))] + cast_specs,
        out_shape=[jax.ShapeDtypeStruct((b, s, NSA_WIDTH), BF16)] + cast_shapes,
        scratch_shapes=[aug, aug, aug, pltpu.VMEM((NSA_REP * NSA_TQ, s), F32),
                        pltpu.VMEM((NSA_REP * NSA_TQ, HEAD_DIM), F32),
                        pltpu.VMEM((NSA_TQ // LANES, n_chunks, LANES), F32)],
        compiler_params=_params(3),
        name="nsa_attention",
    )(slopes, proj_a, cmp_kv[0], cmp_kv[1], proj_a, proj_a, proj_a, proj_a, gates3, gain,
      kext, qext, gsel, *cast_arrays)


def _diff_kernel(lq1_ref, lk1_ref, lq2_ref, lk2_ref, q_ref, k_ref, v_ref, g_ref, kext_ref, qext_ref,
                 *rest, n_cast, lambda_init):
    cast_src, o_ref, cast_dst = rest[:n_cast], rest[n_cast], rest[n_cast + 1:2 * n_cast + 1]
    k_aug, s_ref, acc_ref = rest[2 * n_cast + 1:]
    _cast_blocks(cast_src, cast_dst)
    i = pl.program_id(2)
    tq, ck = DIFF_TQ, DIFF_CK
    t0 = i * tq

    @pl.when(i == 0)
    def _():
        for mi in range(2):
            k_aug[mi, :, 0:DIFF_DIM] = k_ref[0, :, mi * DIFF_DIM:(mi + 1) * DIFF_DIM]
            k_aug[mi, :, DIFF_DIM:] = kext_ref[...]

    q = q_ref[0]
    qext = jnp.broadcast_to(qext_ref[0], (tq, LANES)).astype(BF16)
    q_aug = [jnp.concatenate([q[:, mi * DIFF_DIM:(mi + 1) * DIFF_DIM], qext], axis=1) for mi in range(2)]
    t_col = t0 + lax.broadcasted_iota(jnp.int32, (tq, 1), 0)

    def branch(n):
        m_row = []
        for mi in range(2):
            mx = jnp.full((tq, LANES), NEG, F32)
            for c in range(n):
                cols = slice(c * ck, (c + 1) * ck)
                t = _nt_dot(q_aug[mi], k_aug[mi, cols, :])
                if c == n - 1:
                    pos = c * ck + lax.broadcasted_iota(jnp.int32, (1, ck), 1)
                    t = jnp.where(pos <= t_col, t, NEG)
                s_ref[mi, :, cols] = t
                mx = jnp.maximum(mx, _lane_fold(t, jnp.maximum))
            m_row.append(jnp.max(mx, axis=-1, keepdims=True))
        for mi in range(2):
            l_fold = jnp.zeros((tq, LANES), F32)
            acc = jnp.zeros((tq, DIFF_VDIM), F32)
            for c in range(n):
                cols = slice(c * ck, (c + 1) * ck)
                p = jnp.exp2(s_ref[mi, :, cols] - m_row[mi])
                l_fold = l_fold + _lane_fold(p, jnp.add)
                acc = acc + _dot_row_halves(p.astype(BF16), v_ref[0, cols, :])
            acc_ref[mi] = acc * (1.0 / jnp.sum(l_fold, axis=-1, keepdims=True))

    for n in range(1, k_aug.shape[1] // ck + 1):
        pl.when(i + 1 == n)(functools.partial(branch, n))

    lam = (jnp.exp(jnp.sum(lq1_ref[...] * lk1_ref[...], axis=-1, keepdims=True))
           - jnp.exp(jnp.sum(lq2_ref[...] * lk2_ref[...], axis=-1, keepdims=True))
           + lambda_init)
    o = acc_ref[0] - lam * acc_ref[1]
    o = o * lax.rsqrt(jnp.mean(o * o, axis=-1, keepdims=True) + RMS_EPS) * g_ref[...]
    o_ref[0] = (o * (1.0 - lambda_init)).astype(o_ref.dtype)


def _diff_attention(proj_b, lq1, lk1, lq2, lk2, gain, kext, qext, lambda_init, cast_arrays):
    b, s, _ = proj_b.shape
    n_q = s // DIFF_TQ
    cast_specs, cast_shapes = _cast_specs(cast_arrays, b * DIFF_HEADS * n_q,
                                          lambda bi, h, i: (bi * DIFF_HEADS + h) * n_q + i)
    qcol, kcol, vcol = 0, DIFF_HEADS, 2 * DIFF_HEADS
    vec = pl.BlockSpec((1, DIFF_DIM), lambda bi, h, i: (0, 0))
    in_specs = [
        vec, vec, vec, vec,
        pl.BlockSpec((1, DIFF_TQ, DIFF_VDIM), lambda bi, h, i: (bi, i, qcol + h)),
        pl.BlockSpec((1, s, DIFF_VDIM), lambda bi, h, i: (bi, 0, kcol + h)),
        pl.BlockSpec((1, s, DIFF_VDIM), lambda bi, h, i: (bi, 0, vcol + h)),
        pl.BlockSpec((1, DIFF_VDIM), lambda bi, h, i: (0, 0)),
        pl.BlockSpec(kext.shape, lambda bi, h, i: (0, 0)),
        pl.BlockSpec((1, 1, LANES), lambda bi, h, i: (h, 0, 0)),
    ] + cast_specs
    return pl.pallas_call(
        functools.partial(_diff_kernel, n_cast=len(cast_arrays), lambda_init=lambda_init),
        grid=(b, DIFF_HEADS, n_q),
        in_specs=in_specs,
        out_specs=[pl.BlockSpec((1, DIFF_TQ, DIFF_VDIM), lambda bi, h, i: (bi, i, h))] + cast_specs,
        out_shape=[jax.ShapeDtypeStruct((b, s, DIFF_WIDTH), BF16)] + cast_shapes,
        scratch_shapes=[pltpu.VMEM((2, s, 2 * DIFF_DIM), BF16), pltpu.VMEM((2, DIFF_TQ, s), F32),
                        pltpu.VMEM((2, DIFF_TQ, DIFF_VDIM), F32)],
        compiler_params=_params(3),
        name="diff_attention",
    )(lq1.reshape(1, -1), lk1.reshape(1, -1), lq2.reshape(1, -1), lk2.reshape(1, -1),
      proj_b, proj_b, proj_b, gain.reshape(1, -1), kext, qext, *cast_arrays)


def _alibi_slopes(n_heads):
    return np.array([2.0 ** (-8.0 * (h + 1) / n_heads) for h in range(n_heads)], np.float32)


def _round_to_bf16(x):
    bits = np.array(x, np.float32).view(np.uint32)
    bits = (bits + np.uint32(0x7FFF) + ((bits >> np.uint32(16)) & np.uint32(1))) & np.uint32(0xFFFF0000)
    return np.float64(bits.view(np.float32))


def _bf16_pieces(x, n):
    out, rem = [], np.float64(x)
    for _ in range(n):
        piece = _round_to_bf16(rem)
        out.append(piece)
        rem = rem - piece
    return out


def _key_ext(seq):
    ext = np.zeros((seq, LANES), np.float32)
    pos = np.arange(seq)
    ext[pos, pos // SLC_BLOCK] = 1.0
    centred = pos - seq // 2
    hi = (centred // 256) * 256
    lo = centred - hi
    n_slc = seq // SLC_BLOCK
    ext[:, n_slc:n_slc + SLOPE_PIECES] = hi[:, None]
    ext[:, n_slc + SLOPE_PIECES:n_slc + 2 * SLOPE_PIECES] = lo[:, None]
    return ext


def _query_ext(slopes, n_slc):
    ext = np.zeros((len(slopes), LANES), np.float32)
    for h, slope in enumerate(slopes):
        pieces = _bf16_pieces(np.float64(slope) * LOG2E, SLOPE_PIECES)
        ext[h, n_slc:n_slc + SLOPE_PIECES] = pieces
        ext[h, n_slc + SLOPE_PIECES:n_slc + 2 * SLOPE_PIECES] = pieces
    return ext


def _gate_select():
    per = 3 * NSA_REP
    sel = np.zeros((NSA_GROUPS, LANES, per * LANES), np.float32)
    for g in range(NSA_GROUPS):
        for k in range(per):
            sel[g, g * per + k, k * LANES:(k + 1) * LANES] = 1.0
    return sel


def _layer(x, w_in, cmp_pe_k, cmp_w1_k, cmp_w2_k, cmp_pe_v, cmp_w1_v, cmp_w2_v, nsa_out_g,
           lambda_q1, lambda_k1, lambda_q2, lambda_k2, diff_subln_g, w_out, ln1_g, ln1_b,
           w_ff1, w_ff2, ln2_g, ln2_b, lambda_init):
    b, s, d = x.shape
    t = b * s
    x2 = x.reshape(t, d)

    w_t = jnp.swapaxes(w_in, 0, 1)
    x_bf, gates = _gate_cast(x2, w_t, row0=GATE_START)
    proj_a, chunks = _inproj(x_bf, w_t, BF16, row0=0, n=PROJ_A_COLS, scaled_cols=NSA_WIDTH,
                             chunked=(NSA_WIDTH, 2, s), name="in_proj_nsa")
    proj_b = _inproj(x_bf, w_t, BF16, row0=GATE_START, n=PROJ_B_COLS, scaled_cols=DIFF_WIDTH,
                     shift=GATE_COLS, name="in_proj_diff")
    proj_a = proj_a.reshape(b, s, PROJ_A_COLS)
    proj_b = proj_b.reshape(b, s, PROJ_B_COLS)

    pe = jnp.stack([cmp_pe_k, cmp_pe_v]).reshape(2, 1, CMP_BLOCK * HEAD_DIM).astype(BF16)
    w1 = jnp.stack([cmp_w1_k, cmp_w1_v]).astype(BF16)
    w2 = jnp.stack([cmp_w2_k, cmp_w2_v]).astype(BF16)
    cmp_kv = _compress(chunks, pe, w1, w2)

    n_slc = s // SLC_BLOCK
    kext = jnp.asarray(_key_ext(s)).astype(BF16)
    o_nsa, w_ff1_bf, w_out_bf = _nsa_attention(
        proj_a, cmp_kv, gates.reshape(b, s, LANES),
        nsa_out_g.reshape(NSA_GROUPS, NSA_REP, HEAD_DIM),
        jnp.asarray(_alibi_slopes(NSA_HEADS)), kext,
        jnp.asarray(_query_ext(_alibi_slopes(NSA_HEADS), n_slc)).reshape(NSA_GROUPS, NSA_REP, LANES),
        jnp.asarray(_gate_select()).astype(BF16), [w_ff1, w_out])
    o_diff, w_ff2_bf = _diff_attention(
        proj_b, lambda_q1, lambda_k1, lambda_q2, lambda_k2, diff_subln_g, kext,
        jnp.asarray(_query_ext(_alibi_slopes(DIFF_HEADS), n_slc)).reshape(DIFF_HEADS, 1, LANES),
        lambda_init, [w_ff2])

    h1 = _outproj(o_nsa.reshape(t, NSA_WIDTH), o_diff.reshape(t, DIFF_WIDTH), w_out_bf, x2)
    x1, x1_bf = _layer_norm(h1, ln1_g, ln1_b, (F32, BF16))
    hid = _ff1(x1_bf, w_ff1_bf)
    h2 = _ff2(hid, w_ff2_bf, x1)
    (out,) = _layer_norm(h2, ln2_g, ln2_b, (F32,))
    return out.reshape(b, s, d)


def kernel(x, w_in, cmp_pe_k, cmp_w1_k, cmp_w2_k, cmp_pe_v, cmp_w1_v, cmp_w2_v, nsa_out_g,
           lambda_q1, lambda_k1, lambda_q2, lambda_k2, diff_subln_g, w_out, ln1_g, ln1_b,
           w_ff1, w_ff2, ln2_g, ln2_b):
    for l in range(DEPTH):
        lambda_init = 0.8 - 0.6 * math.exp(-0.3 * l)
        x = _layer(x, w_in[l], cmp_pe_k[l], cmp_w1_k[l], cmp_w2_k[l], cmp_pe_v[l], cmp_w1_v[l],
                   cmp_w2_v[l], nsa_out_g[l], lambda_q1[l], lambda_k1[l], lambda_q2[l],
                   lambda_k2[l], diff_subln_g[l], w_out[l], ln1_g[l], ln1_b[l], w_ff1[l],
                   w_ff2[l], ln2_g[l], ln2_b[l], lambda_init)
    return x
```

```python
import functools
import math

import numpy as np
import jax
import jax.numpy as jnp
from jax import lax
from jax.experimental import pallas as pl
from jax.experimental.pallas import tpu as pltpu

F32 = jnp.float32
BF16 = jnp.bfloat16

D_MODEL = 4096
HEAD_DIM = 128
NSA_HEADS = 16
NSA_GROUPS = 4
NSA_REP = 4
CMP_BLOCK = 32
CMP_STRIDE = 16
CMP_HIDDEN = 256
SLC_BLOCK = 64
SLC_TOPK = 16
WINDOW = 512
DIFF_HEADS = 8
DIFF_DIM = 128
DIFF_VDIM = 256
NSA_WIDTH = NSA_HEADS * HEAD_DIM
KV_WIDTH = 6 * NSA_GROUPS * HEAD_DIM
GATE_COLS = 3 * NSA_HEADS
GATE_START = NSA_WIDTH + KV_WIDTH
DIFF_WIDTH = DIFF_HEADS * DIFF_VDIM
PROJ_A_COLS = GATE_START
PROJ_B_COLS = 3 * DIFF_WIDTH
LN_EPS = 1e-5
RMS_EPS = 1e-6
NEG = -1e30
DEPTH = 1
DEEPNORM_ALPHA = (2.0 * DEPTH) ** 0.25
ATT_SCALE = HEAD_DIM ** -0.5
LOG2E = 1.4426950408889634
ATT_SCALE2 = ATT_SCALE * LOG2E

LANES = 128
BF16_SUBLANES = 16
VMEM_LIMIT_BYTES = 56 * 1024 * 1024

MM_BM = 1024
MM_BN = 1024
INPROJ_BN = 512
OUTPROJ_BN = 512
GATE_BM = 512
FF2_BK = 4096
CAST_COLS = 512
LN_ROWS = 256
NSA_TQ = 256
SLC_CK = 512
WIN_SPAN = WINDOW + NSA_TQ
DIFF_TQ = 512
DIFF_CK = 512
SLOPE_PIECES = 4


def _params(n_axes):
    return pltpu.CompilerParams(dimension_semantics=("arbitrary",) * n_axes,
                                vmem_limit_bytes=VMEM_LIMIT_BYTES)


def _load_weight_rows(b_ref, tail_ref, shift, cols):
    if shift:
        return jnp.concatenate([b_ref[shift:, cols], tail_ref[0:shift, cols]], axis=0)
    return b_ref[:, cols]


def _inproj_kernel(*refs, shift, n_inner, scaled_tiles, chunked_tiles):
    refs = list(refs)
    a_ref, b_ref = refs[:2]
    tail_ref = refs[2] if shift else None
    o_ref = refs[3 if shift else 2]
    wb_ref = refs[-1]
    j = pl.program_id(0)
    i = pl.program_id(1)

    def prepare_slice():
        width = wb_ref.shape[1] // n_inner
        cols = pl.ds(pl.multiple_of(i * width, width), width)
        blk = _load_weight_rows(b_ref, tail_ref, shift, cols)
        blk = blk * jnp.where(j < scaled_tiles, ATT_SCALE2, 1.0)
        wb_ref[j % 2, cols, :] = blk.T.astype(BF16)

    pl.when(j == 0)(prepare_slice)

    @pl.when(j > 0)
    def _():
        acc = jnp.dot(a_ref[...], wb_ref[(j + 1) % 2], preferred_element_type=F32)
        o_ref[...] = acc.astype(o_ref.dtype)
        if chunked_tiles:
            for grp in range(refs[-2].shape[0]):
                refs[-2][grp] = acc[:, grp * HEAD_DIM:(grp + 1) * HEAD_DIM]
        prepare_slice()

    if chunked_tiles:
        first, n_ch = chunked_tiles
        c_ref, stage_ref = refs[-3], refs[-2]

        @pl.when((j - 1 >= first) & (j - 1 < first + n_ch))
        def _():
            n_groups, n_rows = c_ref.shape[2], c_ref.shape[3]
            for grp in range(n_groups):
                for l in range(CMP_STRIDE):
                    c_ref[0, 0, grp, :, l * HEAD_DIM:(l + 1) * HEAD_DIM] = stage_ref[
                        grp, pl.ds(l, n_rows, stride=CMP_STRIDE), :].astype(c_ref.dtype)


def _inproj(a, wt, out_dtype, *, row0, n, scaled_cols, shift=0, chunked=None, bn=INPROJ_BN,
            bm=MM_BM, name):
    m, k = a.shape
    n_tiles = n // bn
    n_inner = m // bm
    assert k % ((m // bm) * LANES) == 0
    jb = row0 // bn

    def tile(j):
        return jnp.minimum(j, n_tiles - 1)

    def row_tile(j, i):
        return jnp.where(j == 0, 0, i)

    in_specs = [pl.BlockSpec((bm, k), lambda j, i: (row_tile(j, i), 0)),
                pl.BlockSpec((bn, k), lambda j, i: (jb + tile(j), 0))]
    args = [a, wt]
    if shift:
        per = bn // LANES
        tb = row0 // LANES
        in_specs.append(pl.BlockSpec((LANES, k), lambda j, i: (tb + per * (tile(j) + 1), 0)))
        args.append(wt)
    out_specs = [pl.BlockSpec((bm, bn), lambda j, i: (row_tile(j, i), jnp.maximum(j - 1, 0)))]
    out_shape = [jax.ShapeDtypeStruct((m, n), out_dtype)]
    scratch = [pltpu.VMEM((2, k, bn), BF16)]
    chunked_tiles = None
    if chunked:
        col0, n_kinds, seq = chunked
        first = col0 // bn
        per_seq = seq // bm
        n_rows = bm // CMP_STRIDE
        chunked_tiles = (first, n_kinds)

        def chunk_block(j, i):
            t = j - 1
            ii = jnp.where(t < first, 0, jnp.where(t >= first + n_kinds, n_inner - 1, i))
            return (jnp.clip(t - first, 0, n_kinds - 1), ii // per_seq, 0, ii % per_seq, 0)

        out_specs.append(pl.BlockSpec((1, 1, bn // HEAD_DIM, n_rows, CMP_STRIDE * HEAD_DIM), chunk_block))
        out_shape.append(jax.ShapeDtypeStruct(
            (n_kinds, m // seq, bn // HEAD_DIM, seq // CMP_STRIDE, CMP_STRIDE * HEAD_DIM), out_dtype))
        scratch.insert(0, pltpu.VMEM((bn // HEAD_DIM, bm, HEAD_DIM), F32))
    res = pl.pallas_call(
        functools.partial(_inproj_kernel, shift=shift, n_inner=n_inner,
                          scaled_tiles=scaled_cols // bn, chunked_tiles=chunked_tiles),
        grid=(n_tiles + 1, n_inner),
        in_specs=in_specs,
        out_specs=out_specs,
        out_shape=out_shape,
        scratch_shapes=scratch,
        compiler_params=_params(2),
        name=name,
    )(*args)
    return res if chunked else res[0]


def _gate_cast_kernel(x_ref, wt_ref, xb_ref, g_ref, wb_ref):
    @pl.when(pl.program_id(0) == 0)
    def _():
        k = wb_ref.shape[0]
        for c0 in range(0, k, CAST_COLS):
            wb_ref[c0:c0 + CAST_COLS, :] = wt_ref[:, c0:c0 + CAST_COLS].T.astype(BF16)

    xb = x_ref[...].astype(BF16)
    xb_ref[...] = xb
    g_ref[...] = jnp.dot(xb, wb_ref[...], preferred_element_type=F32)


def _gate_cast(x, wt, *, row0, bm=GATE_BM):
    m, k = x.shape
    return pl.pallas_call(
        _gate_cast_kernel,
        grid=(m // bm,),
        in_specs=[pl.BlockSpec((bm, k), lambda i: (i, 0)),
                  pl.BlockSpec((LANES, k), lambda i: (row0 // LANES, 0))],
        out_specs=[pl.BlockSpec((bm, k), lambda i: (i, 0)),
                   pl.BlockSpec((bm, LANES), lambda i: (i, 0))],
        out_shape=[jax.ShapeDtypeStruct((m, k), BF16), jax.ShapeDtypeStruct((m, LANES), F32)],
        scratch_shapes=[pltpu.VMEM((k, LANES), BF16)],
        compiler_params=_params(1),
        name="gate_proj_cast",
    )(x, wt)


def _mm_kernel(a_ref, b_ref, *rest, n_cast):
    cast_src, o_ref, cast_dst = rest[:n_cast], rest[n_cast], rest[n_cast + 1:]
    acc = jnp.dot(a_ref[...], b_ref[...], preferred_element_type=F32)
    o_ref[...] = jnp.square(jnp.maximum(acc, 0.0)).astype(o_ref.dtype)
    _cast_blocks(cast_src, cast_dst)


def _ff1(a, b, cast_arrays, *, bm=MM_BM, bn=MM_BN):
    m, k = a.shape
    _, n = b.shape
    n_j = n // bn
    cast_specs, cast_shapes = _cast_specs(cast_arrays, (m // bm) * n_j, lambda i, j: i * n_j + j)
    return pl.pallas_call(
        functools.partial(_mm_kernel, n_cast=len(cast_arrays)),
        grid=(m // bm, n_j),
        in_specs=[pl.BlockSpec((bm, k), lambda i, j: (i, 0)),
                  pl.BlockSpec((k, bn), lambda i, j: (0, j))] + cast_specs,
        out_specs=[pl.BlockSpec((bm, bn), lambda i, j: (i, j))] + cast_specs,
        out_shape=[jax.ShapeDtypeStruct((m, n), BF16)] + cast_shapes,
        compiler_params=_params(2),
        name="ff1",
    )(a, b, *cast_arrays)


def _outproj_kernel(a1_ref, a2_ref, b_ref, res_ref, o_ref):
    half = a1_ref.shape[1]
    acc = jnp.dot(a1_ref[...], b_ref[0:half, :], preferred_element_type=F32)
    acc = acc + jnp.dot(a2_ref[...], b_ref[half:2 * half, :], preferred_element_type=F32)
    o_ref[...] = DEEPNORM_ALPHA * res_ref[...] + acc


def _outproj(a1, a2, b, res, *, bm=MM_BM, bn=OUTPROJ_BN):
    m, k1 = a1.shape
    _, n = b.shape
    return pl.pallas_call(
        _outproj_kernel,
        grid=(m // bm, n // bn),
        in_specs=[pl.BlockSpec((bm, k1), lambda i, j: (i, 0)),
                  pl.BlockSpec((bm, k1), lambda i, j: (i, 0)),
                  pl.BlockSpec((2 * k1, bn), lambda i, j: (0, j)),
                  pl.BlockSpec((bm, bn), lambda i, j: (i, j))],
        out_specs=pl.BlockSpec((bm, bn), lambda i, j: (i, j)),
        out_shape=jax.ShapeDtypeStruct((m, n), F32),
        compiler_params=_params(2),
        name="outproj",
    )(a1, a2, b, res)


def _ff2_kernel(a_ref, b_ref, res_ref, o_ref):
    @pl.when(pl.program_id(2) == 0)
    def _():
        o_ref[...] = DEEPNORM_ALPHA * res_ref[...]

    o_ref[...] += jnp.dot(a_ref[...], b_ref[...], preferred_element_type=F32)


def _ff2(a, b, res, *, bm=MM_BM, bn=MM_BN, bk=FF2_BK):
    m, k = a.shape
    _, n = b.shape
    return pl.pallas_call(
        _ff2_kernel,
        grid=(m // bm, n // bn, k // bk),
        in_specs=[pl.BlockSpec((bm, bk), lambda i, j, kk: (i, kk)),
                  pl.BlockSpec((bk, bn), lambda i, j, kk: (kk, j)),
                  pl.BlockSpec((bm, bn), lambda i, j, kk: (i, j))],
        out_specs=pl.BlockSpec((bm, bn), lambda i, j, kk: (i, j)),
        out_shape=jax.ShapeDtypeStruct((m, n), F32),
        compiler_params=_params(3),
        name="ff2",
    )(a, b, res)


def _cast_blocks(src_refs, dst_refs):
    for src, dst in zip(src_refs, dst_refs):
        dst[...] = src[...].astype(dst.dtype)


def _cast_specs(arrays, n_steps, step_of):
    specs, shapes = [], []
    for arr in arrays:
        rows, cols = arr.shape
        assert rows % (n_steps * BF16_SUBLANES) == 0
        blk = (rows // n_steps, cols)
        specs.append(pl.BlockSpec(blk, lambda *ids: (step_of(*ids), 0)))
        shapes.append(jax.ShapeDtypeStruct((rows, cols), BF16))
    return specs, shapes


def _ln_kernel(x_ref, g_ref, b_ref, *o_refs):
    x = x_ref[...]
    mu = jnp.mean(x, axis=-1, keepdims=True)
    xc = x - mu
    var = jnp.mean(xc * xc, axis=-1, keepdims=True)
    y = xc * lax.rsqrt(var + LN_EPS) * g_ref[...] + b_ref[...]
    for o_ref in o_refs:
        o_ref[...] = y.astype(o_ref.dtype)


def _layer_norm(x, g, b, out_dtypes):
    m, d = x.shape
    row_spec = pl.BlockSpec((LN_ROWS, d), lambda i: (i, 0))
    vec_spec = pl.BlockSpec((1, d), lambda i: (0, 0))
    return pl.pallas_call(
        _ln_kernel,
        grid=(m // LN_ROWS,),
        in_specs=[row_spec, vec_spec, vec_spec],
        out_specs=[row_spec for _ in out_dtypes],
        out_shape=[jax.ShapeDtypeStruct((m, d), dt) for dt in out_dtypes],
        compiler_params=_params(1),
        name="layer_norm",
    )(x, g.reshape(1, d), b.reshape(1, d))


def _gelu_tanh(x):
    c = math.sqrt(2.0 / math.pi)
    return 0.5 * x * (1.0 + jnp.tanh(c * (x + 0.044715 * (x * x * x))))


def _compress_kernel(c_ref, pe_ref, w1_ref, w2_ref, o_ref):
    half = CMP_STRIDE * HEAD_DIM
    nb, ng, n_chunks, _ = c_ref.shape[1:]
    rows = nb * ng * n_chunks
    c = c_ref[0].reshape(rows, half)
    y1 = jnp.dot(c, w1_ref[0, 0:half, :], preferred_element_type=F32)
    y2 = jnp.dot(c, w1_ref[0, half:2 * half, :], preferred_element_type=F32)
    pe8 = jnp.broadcast_to(pe_ref[0], (8, 2 * half))
    pb = jnp.dot(pe8, w1_ref[0], preferred_element_type=F32)[0:1, :]
    h = y1 + pltpu.roll(y2, rows - 1, 0) + pb
    a = _gelu_tanh(h).astype(BF16)
    out = jnp.dot(a, w2_ref[0], preferred_element_type=F32)
    row = lax.broadcasted_iota(jnp.int32, out.shape, 0)
    out = jnp.where((row & (n_chunks - 1)) < n_chunks - 1, out, 0.0)
    o_ref[0] = out.astype(o_ref.dtype).reshape(nb, ng, n_chunks, HEAD_DIM)


def _compress(chunks, pe, w1, w2):
    _, b, g, n_chunks, width = chunks.shape
    assert n_chunks & (n_chunks - 1) == 0
    return pl.pallas_call(
        _compress_kernel,
        grid=(2,),
        in_specs=[pl.BlockSpec((1, b, g, n_chunks, width), lambda t: (t, 0, 0, 0, 0)),
                  pl.BlockSpec((1, 1, 2 * width), lambda t: (t, 0, 0)),
                  pl.BlockSpec((1, 2 * width, CMP_HIDDEN), lambda t: (t, 0, 0)),
                  pl.BlockSpec((1, CMP_HIDDEN, HEAD_DIM), lambda t: (t, 0, 0))],
        out_specs=pl.BlockSpec((1, b, g, n_chunks, HEAD_DIM), lambda t: (t, 0, 0, 0, 0)),
        out_shape=jax.ShapeDtypeStruct((2, b, g, n_chunks, HEAD_DIM), BF16),
        compiler_params=_params(1),
        name="compress",
    )(chunks, pe, w1, w2)


def _nt_dot(a, b):
    return lax.dot_general(a, b, (((1,), (1,)), ((), ())), preferred_element_type=F32)


def _dot_row_halves(a, b):
    half = a.shape[0] // 2
    return jnp.concatenate([jnp.dot(a[:half], b, preferred_element_type=F32),
                            jnp.dot(a[half:], b, preferred_element_type=F32)], axis=0)


def _lane_fold(x, op):
    out = x[:, 0:LANES]
    for c0 in range(LANES, x.shape[1], LANES):
        out = op(out, x[:, c0:c0 + LANES])
    return out


def _nsa_kernel(slopes_ref, q_ref, kc_ref, vc_ref, ks_ref, vs_ref, kw_ref, vw_ref,
                gate_ref, gain_ref, kext_ref, qext_ref, gsel_ref, *rest, n_cast):
    cast_src, o_ref, cast_dst = rest[:n_cast], rest[n_cast], rest[n_cast + 1:2 * n_cast + 1]
    ks_aug, vs_aug, vw_aug, s_ref, oslc_ref, psum_ref = rest[2 * n_cast + 1:]
    _cast_blocks(cast_src, cast_dst)
    g = pl.program_id(1)
    i = pl.program_id(2)
    tq = NSA_TQ
    rows = NSA_REP * tq
    t0 = i * tq

    @pl.when(i == 0)
    def _():
        ks_aug[:, 0:HEAD_DIM] = ks_ref[0]
        ks_aug[:, HEAD_DIM:] = kext_ref[...]
        ones = jnp.ones((ks_aug.shape[0], HEAD_DIM), BF16)
        vs_aug[:, 0:HEAD_DIM] = vs_ref[0]
        vs_aug[:, HEAD_DIM:] = ones
        vw_aug[:, 0:HEAD_DIM] = vw_ref[0]
        vw_aug[:, HEAD_DIM:] = ones

    q = q_ref[0]
    q_heads = [q[:, r * HEAD_DIM:(r + 1) * HEAD_DIM] for r in range(NSA_REP)]
    qs = jnp.concatenate(q_heads, axis=0)
    t_col = t0 + lax.broadcasted_iota(jnp.int32, (tq, 1), 0)
    head = [slice(r * tq, (r + 1) * tq) for r in range(NSA_REP)]

    n_cmp = kc_ref.shape[2]
    s_t = _nt_dot(kc_ref[0, 0], qs)
    n_col = lax.broadcasted_iota(jnp.int32, (n_cmp, 1), 0)
    t_row = t0 + (lax.broadcasted_iota(jnp.int32, (1, rows), 1) & (tq - 1))
    cmask = (n_col * CMP_STRIDE + (CMP_BLOCK - 1)) <= t_row
    s_t = jnp.where(cmask, s_t, NEG)
    m = jnp.max(s_t, axis=0, keepdims=True)
    e = jnp.where(cmask, jnp.exp2(s_t - m), 0.0)
    l = jnp.sum(e, axis=0, keepdims=True)
    p_t = e * (1.0 / jnp.maximum(l, 1e-30))
    p_sum_t = p_t[:, head[0]]
    for r in range(1, NSA_REP):
        p_sum_t = p_sum_t + p_t[:, head[r]]
    o_cmp = lax.dot_general(p_t.astype(BF16), vc_ref[0, 0], (((0,), (0,)), ((), ())),
                            preferred_element_type=F32)

    n_slc = n_cmp * CMP_STRIDE // SLC_BLOCK
    per = SLC_BLOCK // CMP_STRIDE
    for hl in range(tq // LANES):
        psum_ref[hl] = p_sum_t[:, hl * LANES:(hl + 1) * LANES]
    rows_k = [jnp.concatenate([psum_ref[hl, pl.ds(k, n_slc, stride=per), :] for hl in range(tq // LANES)],
                              axis=1) for k in range(per)]
    j_t = lax.broadcasted_iota(jnp.int32, (n_slc, tq), 0)
    prev_last = jnp.where(j_t == 0, 0.0, pltpu.roll(rows_k[per - 1], 1, 0))
    imp_t = rows_k[0] + rows_k[1] + rows_k[2] + 0.5 * (rows_k[3] + prev_last)
    t_t = t0 + lax.broadcasted_iota(jnp.int32, (n_slc, tq), 1)
    causal = j_t * SLC_BLOCK <= t_t
    tb = lax.shift_right_logical(t_t, int(math.log2(SLC_BLOCK)))
    forced = jnp.where(j_t == 0, 1e6, jnp.where(j_t == tb, 1e6, jnp.where(j_t == tb - 1, 1e6, imp_t)))
    score = jnp.where(causal, forced, NEG)
    rank = jnp.zeros((n_slc, tq), F32)
    for jp in range(n_slc):
        row = score[jp:jp + 1, :]
        ge = jnp.where(row >= score, 1.0, 0.0)
        gt = jnp.where(row > score, 1.0, 0.0)
        rank = rank + jnp.where(j_t > jp, ge, gt)
    sel_bias = jnp.where(causal, jnp.where(rank < SLC_TOPK, 0.0, NEG), NEG)

    sel_rows = jnp.concatenate([sel_bias, jnp.zeros((LANES - n_slc, tq), F32)], axis=0).T
    q_aug = jnp.concatenate(
        [jnp.concatenate([q_heads[r], (sel_rows + qext_ref[0, r:r + 1, :]).astype(BF16)], axis=1)
         for r in range(NSA_REP)], axis=0)
    ck = SLC_CK
    n_chunks = (t0 + tq + ck - 1) // ck

    def slc_branch(n):
        mx = jnp.full((rows, LANES), NEG, F32)
        for c in range(n):
            cols = slice(c * ck, (c + 1) * ck)
            t = _nt_dot(q_aug, ks_aug[cols, :])
            if c == n - 1:
                pos = c * ck + lax.broadcasted_iota(jnp.int32, (1, ck), 1)
                cb = jnp.where(pos <= t_col, 0.0, NEG)
                t = t + jnp.concatenate([cb] * NSA_REP, axis=0)
            s_ref[:, cols] = t
            mx = jnp.maximum(mx, _lane_fold(t, jnp.maximum))
        m_slc = jnp.max(mx, axis=-1, keepdims=True)
        acc = jnp.zeros((rows, 2 * HEAD_DIM), F32)
        for c in range(n):
            cols = slice(c * ck, (c + 1) * ck)
            p = jnp.exp2(s_ref[:, cols] - m_slc).astype(BF16)
            acc = acc + _dot_row_halves(p, vs_aug[cols, :])
        oslc_ref[...] = acc[:, 0:HEAD_DIM] * (1.0 / acc[:, HEAD_DIM:])

    for n in range(1, ks_aug.shape[0] // ck + 1):
        pl.when(n_chunks == n)(functools.partial(slc_branch, n))
    o_slc = oslc_ref[...]

    slope2 = [slopes_ref[g * NSA_REP + r] * LOG2E for r in range(NSA_REP)]
    k0 = pl.multiple_of(jnp.maximum(t0 - WINDOW, 0), tq)
    pos = k0 + lax.broadcasted_iota(jnp.int32, (1, WIN_SPAN), 1)
    dist = t_col - pos
    wbias = jnp.where(dist >= 0, jnp.where(dist < WINDOW, 0.0, NEG), NEG)
    posrel = (pos - t0).astype(F32)
    kband = kw_ref[0, pl.ds(k0, WIN_SPAN), :]
    vband = vw_aug[pl.ds(k0, WIN_SPAN), :]
    halves = []
    for pair in range(NSA_REP // 2):
        ps = []
        for r in (2 * pair, 2 * pair + 1):
            t = _nt_dot(q_heads[r], kband) + slope2[r] * posrel + wbias
            m = jnp.max(t, axis=-1, keepdims=True)
            ps.append(jnp.exp2(t - m).astype(BF16))
        halves.append(jnp.dot(jnp.concatenate(ps, axis=0), vband, preferred_element_type=F32))
    acc_w = jnp.concatenate(halves, axis=0)
    o_win = acc_w[:, 0:HEAD_DIM] * (1.0 / acc_w[:, HEAD_DIM:])

    sig = jax.nn.sigmoid(gate_ref[0]).astype(BF16)
    gates = jnp.dot(sig, gsel_ref[0], preferred_element_type=F32)
    for r in range(NSA_REP):
        gt = [gates[:, (3 * r + br) * LANES:(3 * r + br + 1) * LANES] for br in range(3)]
        o = gt[0] * o_cmp[head[r]] + gt[1] * o_slc[head[r]] + gt[2] * o_win[head[r]]
        o = o * lax.rsqrt(jnp.mean(o * o, axis=-1, keepdims=True) + RMS_EPS) * gain_ref[0, r:r + 1, :]
        o_ref[0, :, r * HEAD_DIM:(r + 1) * HEAD_DIM] = o.astype(o_ref.dtype)


def _nsa_attention(proj_a, cmp_kv, gates3, gain, slopes, kext, qext, gsel, cast_arrays):
    b, s, _ = proj_a.shape
    n_chunks = cmp_kv.shape[3]
    n_q = s // NSA_TQ
    cast_specs, cast_shapes = _cast_specs(cast_arrays, b * NSA_GROUPS * n_q,
                                          lambda bi, g, i: (bi * NSA_GROUPS + g) * n_q + i)
    col0 = NSA_WIDTH // HEAD_DIM

    def kv_spec(idx):
        return pl.BlockSpec((1, s, HEAD_DIM), lambda bi, g, i, idx=idx: (bi, 0, col0 + idx * NSA_GROUPS + g))

    cmp_spec = pl.BlockSpec((1, 1, n_chunks, HEAD_DIM), lambda bi, g, i: (bi, g, 0, 0))
    in_specs = [
        pl.BlockSpec(memory_space=pltpu.SMEM),
        pl.BlockSpec((1, NSA_TQ, NSA_REP * HEAD_DIM), lambda bi, g, i: (bi, i, g)),
        cmp_spec, cmp_spec,
        kv_spec(2), kv_spec(3), kv_spec(4), kv_spec(5),
        pl.BlockSpec((1, NSA_TQ, LANES), lambda bi, g, i: (bi, i, 0)),
        pl.BlockSpec((1, NSA_REP, HEAD_DIM), lambda bi, g, i: (g, 0, 0)),
        pl.BlockSpec(kext.shape, lambda bi, g, i: (0, 0)),
        pl.BlockSpec((1, NSA_REP, LANES), lambda bi, g, i: (g, 0, 0)),
        pl.BlockSpec((1,) + gsel.shape[1:], lambda bi, g, i: (g, 0, 0)),
    ] + cast_specs
    aug = pltpu.VMEM((s, 2 * HEAD_DIM), BF16)
    return pl.pallas_call(
        functools.partial(_nsa_kernel, n_cast=len(cast_arrays)),
        grid=(b, NSA_GROUPS, n_q),
        in_specs=in_specs,
        out_specs=[pl.BlockSpec((1, NSA_TQ, NSA_REP * HEAD_DIM), lambda bi, g, i: (bi, i, g))] + cast_specs,
        out_shape=[jax.ShapeDtypeStruct((b, s, NSA_WIDTH), BF16)] + cast_shapes,
        scratch_shapes=[aug, aug, aug, pltpu.VMEM((NSA_REP * NSA_TQ, s), F32),
                        pltpu.VMEM((NSA_REP * NSA_TQ, HEAD_DIM), F32),
                        pltpu.VMEM((NSA_TQ // LANES, n_chunks, LANES), F32)],
        compiler_params=_params(3),
        name="nsa_attention",
    )(slopes, proj_a, cmp_kv[0], cmp_kv[1], proj_a, proj_a, proj_a, proj_a, gates3, gain,
      kext, qext, gsel, *cast_arrays)


def _diff_kernel(lq1_ref, lk1_ref, lq2_ref, lk2_ref, q_ref, k_ref, v_ref, g_ref, kext_ref, qext_ref,
                 *rest, n_cast, lambda_init):
    cast_src, o_ref, cast_dst = rest[:n_cast], rest[n_cast], rest[n_cast + 1:2 * n_cast + 1]
    k_aug, s_ref, acc_ref = rest[2 * n_cast + 1:]
    _cast_blocks(cast_src, cast_dst)
    i = pl.program_id(2)
    tq, ck = DIFF_TQ, DIFF_CK
    t0 = i * tq

    @pl.when(i == 0)
    def _():
        for mi in range(2):
            k_aug[mi, :, 0:DIFF_DIM] = k_ref[0, :, mi * DIFF_DIM:(mi + 1) * DIFF_DIM]
            k_aug[mi, :, DIFF_DIM:] = kext_ref[...]

    q = q_ref[0]
    qext = jnp.broadcast_to(qext_ref[0], (tq, LANES)).astype(BF16)
    q_aug = [jnp.concatenate([q[:, mi * DIFF_DIM:(mi + 1) * DIFF_DIM], qext], axis=1) for mi in range(2)]
    t_col = t0 + lax.broadcasted_iota(jnp.int32, (tq, 1), 0)

    def branch(n):
        m_row = []
        for mi in range(2):
            mx = jnp.full((tq, LANES), NEG, F32)
            for c in range(n):
                cols = slice(c * ck, (c + 1) * ck)
                t = _nt_dot(q_aug[mi], k_aug[mi, cols, :])
                if c == n - 1:
                    pos = c * ck + lax.broadcasted_iota(jnp.int32, (1, ck), 1)
                    t = jnp.where(pos <= t_col, t, NEG)
                s_ref[mi, :, cols] = t
                mx = jnp.maximum(mx, _lane_fold(t, jnp.maximum))
            m_row.append(jnp.max(mx, axis=-1, keepdims=True))
        for mi in range(2):
            l_fold = jnp.zeros((tq, LANES), F32)
            acc = jnp.zeros((tq, DIFF_VDIM), F32)
            for c in range(n):
                cols = slice(c * ck, (c + 1) * ck)
                p = jnp.exp2(s_ref[mi, :, cols] - m_row[mi])
                l_fold = l_fold + _lane_fold(p, jnp.add)
                acc = acc + _dot_row_halves(p.astype(BF16), v_ref[0, cols, :])
            acc_ref[mi] = acc * (1.0 / jnp.sum(l_fold, axis=-1, keepdims=True))

    for n in range(1, k_aug.shape[1] // ck + 1):
        pl.when(i + 1 == n)(functools.partial(branch, n))

    lam = (jnp.exp(jnp.sum(lq1_ref[...] * lk1_ref[...], axis=-1, keepdims=True))
           - jnp.exp(jnp.sum(lq2_ref[...] * lk2_ref[...], axis=-1, keepdims=True))
           + lambda_init)
    o = acc_ref[0] - lam * acc_ref[1]
    o = o * lax.rsqrt(jnp.mean(o * o, axis=-1, keepdims=True) + RMS_EPS) * g_ref[...]
    o_ref[0] = (o * (1.0 - lambda_init)).astype(o_ref.dtype)


def _diff_attention(proj_b, lq1, lk1, lq2, lk2, gain, kext, qext, lambda_init, cast_arrays):
    b, s, _ = proj_b.shape
    n_q = s // DIFF_TQ
    cast_specs, cast_shapes = _cast_specs(cast_arrays, b * DIFF_HEADS * n_q,
                                          lambda bi, h, i: (bi * DIFF_HEADS + h) * n_q + i)
    qcol, kcol, vcol = 0, DIFF_HEADS, 2 * DIFF_HEADS
    vec = pl.BlockSpec((1, DIFF_DIM), lambda bi, h, i: (0, 0))
    in_specs = [
        vec, vec, vec, vec,
        pl.BlockSpec((1, DIFF_TQ, DIFF_VDIM), lambda bi, h, i: (bi, i, qcol + h)),
        pl.BlockSpec((1, s, DIFF_VDIM), lambda bi, h, i: (bi, 0, kcol + h)),
        pl.BlockSpec((1, s, DIFF_VDIM), lambda bi, h, i: (bi, 0, vcol + h)),
        pl.BlockSpec((1, DIFF_VDIM), lambda bi, h, i: (0, 0)),
        pl.BlockSpec(kext.shape, lambda bi, h, i: (0, 0)),
        pl.BlockSpec((1, 1, LANES), lambda bi, h, i: (h, 0, 0)),
    ] + cast_specs
    return pl.pallas_call(
        functools.partial(_diff_kernel, n_cast=len(cast_arrays), lambda_init=lambda_init),
        grid=(b, DIFF_HEADS, n_q),
        in_specs=in_specs,
        out_specs=[pl.BlockSpec((1, DIFF_TQ, DIFF_VDIM), lambda bi, h, i: (bi, i, h))] + cast_specs,
        out_shape=[jax.ShapeDtypeStruct((b, s, DIFF_WIDTH), BF16)] + cast_shapes,
        scratch_shapes=[pltpu.VMEM((2, s, 2 * DIFF_DIM), BF16), pltpu.VMEM((2, DIFF_TQ, s), F32),
                        pltpu.VMEM((2, DIFF_TQ, DIFF_VDIM), F32)],
        compiler_params=_params(3),
        name="diff_attention",
    )(lq1.reshape(1, -1), lk1.reshape(1, -1), lq2.reshape(1, -1), lk2.reshape(1, -1),
      proj_b, proj_b, proj_b, gain.reshape(1, -1), kext, qext, *cast_arrays)


def _alibi_slopes(n_heads):
    return np.array([2.0 ** (-8.0 * (h + 1) / n_heads) for h in range(n_heads)], np.float32)


def _round_to_bf16(x):
    bits = np.array(x, np.float32).view(np.uint32)
    bits = (bits + np.uint32(0x7FFF) + ((bits >> np.uint32(16)) & np.uint32(1))) & np.uint32(0xFFFF0000)
    return np.float64(bits.view(np.float32))


def _bf16_pieces(x, n):
    out, rem = [], np.float64(x)
    for _ in range(n):
        piece = _round_to_bf16(rem)
        out.append(piece)
        rem = rem - piece
    return out


def _key_ext(seq):
    ext = np.zeros((seq, LANES), np.float32)
    pos = np.arange(seq)
    ext[pos, pos // SLC_BLOCK] = 1.0
    centred = pos - seq // 2
    hi = (centred // 256) * 256
    lo = centred - hi
    n_slc = seq // SLC_BLOCK
    ext[:, n_slc:n_slc + SLOPE_PIECES] = hi[:, None]
    ext[:, n_slc + SLOPE_PIECES:n_slc + 2 * SLOPE_PIECES] = lo[:, None]
    return ext


def _query_ext(slopes, n_slc):
    ext = np.zeros((len(slopes), LANES), np.float32)
    for h, slope in enumerate(slopes):
        pieces = _bf16_pieces(np.float64(slope) * LOG2E, SLOPE_PIECES)
        ext[h, n_slc:n_slc + SLOPE_PIECES] = pieces
        ext[h, n_slc + SLOPE_PIECES:n_slc + 2 * SLOPE_PIECES] = pieces
    return ext


def _gate_select():
    per = 3 * NSA_REP
    sel = np.zeros((NSA_GROUPS, LANES, per * LANES), np.float32)
    for g in range(NSA_GROUPS):
        for k in range(per):
            sel[g, g * per + k, k * LANES:(k + 1) * LANES] = 1.0
    return sel


def _layer(x, w_in, cmp_pe_k, cmp_w1_k, cmp_w2_k, cmp_pe_v, cmp_w1_v, cmp_w2_v, nsa_out_g,
           lambda_q1, lambda_k1, lambda_q2, lambda_k2, diff_subln_g, w_out, ln1_g, ln1_b,
           w_ff1, w_ff2, ln2_g, ln2_b, lambda_init):
    b, s, d = x.shape
    t = b * s
    x2 = x.reshape(t, d)

    w_t = jnp.swapaxes(w_in, 0, 1)
    x_bf, gates = _gate_cast(x2, w_t, row0=GATE_START)
    proj_a, chunks = _inproj(x_bf, w_t, BF16, row0=0, n=PROJ_A_COLS, scaled_cols=NSA_WIDTH,
                             chunked=(NSA_WIDTH, 2, s), name="in_proj_nsa")
    proj_b = _inproj(x_bf, w_t, BF16, row0=GATE_START, n=PROJ_B_COLS, scaled_cols=DIFF_WIDTH,
                     shift=GATE_COLS, name="in_proj_diff")
    proj_a = proj_a.reshape(b, s, PROJ_A_COLS)
    proj_b = proj_b.reshape(b, s, PROJ_B_COLS)

    pe = jnp.stack([cmp_pe_k, cmp_pe_v]).reshape(2, 1, CMP_BLOCK * HEAD_DIM).astype(BF16)
    w1 = jnp.stack([cmp_w1_k, cmp_w1_v]).astype(BF16)
    w2 = jnp.stack([cmp_w2_k, cmp_w2_v]).astype(BF16)
    cmp_kv = _compress(chunks, pe, w1, w2)

    n_slc = s // SLC_BLOCK
    kext = jnp.asarray(_key_ext(s)).astype(BF16)
    o_nsa, w_ff1_bf, w_out_bf = _nsa_attention(
        proj_a, cmp_kv, gates.reshape(b, s, LANES),
        nsa_out_g.reshape(NSA_GROUPS, NSA_REP, HEAD_DIM),
        jnp.asarray(_alibi_slopes(NSA_HEADS)), kext,
        jnp.asarray(_query_ext(_alibi_slopes(NSA_HEADS), n_slc)).reshape(NSA_GROUPS, NSA_REP, LANES),
        jnp.asarray(_gate_select()).astype(BF16), [w_ff1, w_out])
    (o_diff,) = _diff_attention(
        proj_b, lambda_q1, lambda_k1, lambda_q2, lambda_k2, diff_subln_g, kext,
        jnp.asarray(_query_ext(_alibi_slopes(DIFF_HEADS), n_slc)).reshape(DIFF_HEADS, 1, LANES),
        lambda_init, [])

    h1 = _outproj(o_nsa.reshape(t, NSA_WIDTH), o_diff.reshape(t, DIFF_WIDTH), w_out_bf, x2)
    x1, x1_bf = _layer_norm(h1, ln1_g, ln1_b, (F32, BF16))
    hid, w_ff2_bf = _ff1(x1_bf, w_ff1_bf, [w_ff2])
    h2 = _ff2(hid, w_ff2_bf, x1)
    (out,) = _layer_norm(h2, ln2_g, ln2_b, (F32,))
    return out.reshape(b, s, d)


def kernel(x, w_in, cmp_pe_k, cmp_w1_k, cmp_w2_k, cmp_pe_v, cmp_w1_v, cmp_w2_v, nsa_out_g,
           lambda_q1, lambda_k1, lambda_q2, lambda_k2, diff_subln_g, w_out, ln1_g, ln1_b,
           w_ff1, w_ff2, ln2_g, ln2_b):
    for l in range(DEPTH):
        lambda_init = 0.8 - 0.6 * math.exp(-0.3 * l)
        x = _layer(x, w_in[l], cmp_pe_k[l], cmp_w1_k[l], cmp_w2_k[l], cmp_pe_v[l], cmp_w1_v[l],
                   cmp_w2_v[l], nsa_out_g[l], lambda_q1[l], lambda_k1[l], lambda_q2[l],
                   lambda_k2[l], diff_subln_g[l], w_out[l], ln1_g[l], ln1_b[l], w_ff1[l],
                   w_ff2[l], ln2_g[l], ln2_b[l], lambda_init)
    return x
```
